```python
import jax, jax.numpy as jnp
from jax import lax
import numpy as np

D_MODEL = 1024
BATCH = 2
SEQ = 16384
DEPTH = 4

GRID_W = 64
CTX_LEN = 256
HEAD_DIM = 128
N_Q_HEADS = D_MODEL // HEAD_DIM
N_KV_HEADS = 2
GROUP = N_Q_HEADS // N_KV_HEADS
ATTN_WIDTH = N_Q_HEADS * HEAD_DIM
KV_WIDTH = N_KV_HEADS * HEAD_DIM
POOL_WIDTH = D_MODEL // 2
POOL_WINDOWS = (2, 4, 8, 16)
N_POOL_GROUPS = 4
POOL_GROUP = POOL_WIDTH // N_POOL_GROUPS
N_IN = 2 * ATTN_WIDTH + 2 * KV_WIDTH + 2 * POOL_WIDTH + 2 * D_MODEL
Q_BLOCK = 128
ROPE_THETA = 10000.0
ROT_AXIS = HEAD_DIM // 2
N_FREQ = ROT_AXIS // 2
EPS = 1e-6
ALPHA = (2 * DEPTH) ** 0.25
BETA = (8 * DEPTH) ** -0.25

kernel_name = "hybrid_gqa_pool_parallel_deepnorm"


def rms_norm(x, g):
    xf = x.astype(jnp.float32)
    y = xf * lax.rsqrt(jnp.mean(xf * xf, axis=-1, keepdims=True) + EPS)
    return (y * g.astype(jnp.float32)).astype(x.dtype)


def layer_norm(x, g, b):
    xf = x.astype(jnp.float32)
    mu = jnp.mean(xf, axis=-1, keepdims=True)
    var = jnp.mean(jnp.square(xf - mu), axis=-1, keepdims=True)
    y = (xf - mu) * lax.rsqrt(var + EPS)
    return (y * g.astype(jnp.float32) + b.astype(jnp.float32)).astype(x.dtype)


def rope_tables(L):
    t = jnp.arange(L, dtype=jnp.int32)
    row = (t // GRID_W).astype(jnp.float32)
    col = (t % GRID_W).astype(jnp.float32)
    inv_freq = 1.0 / (ROPE_THETA ** (jnp.arange(N_FREQ, dtype=jnp.float32) / N_FREQ))
    ang = jnp.stack([row[:, None] * inv_freq[None, :], col[:, None] * inv_freq[None, :]], axis=1)
    return jnp.cos(ang), jnp.sin(ang)


def apply_rope(x, cos, sin):
    xf = x.astype(jnp.float32).reshape(x.shape[:-1] + (2, 2, N_FREQ))
    x1 = xf[..., 0, :]
    x2 = xf[..., 1, :]
    out = jnp.stack([x1 * cos - x2 * sin, x1 * sin + x2 * cos], axis=-2)
    return out.reshape(x.shape).astype(x.dtype)


def split_in(h):
    sizes = (ATTN_WIDTH, KV_WIDTH, KV_WIDTH, ATTN_WIDTH, POOL_WIDTH, POOL_WIDTH, D_MODEL, D_MODEL)
    outs = []
    o = 0
    for s in sizes:
        outs.append(h[..., o:o + s])
        o += s
    return outs


def heads(q, k, v, q_g, k_g):
    B, L = q.shape[0], q.shape[1]
    q = rms_norm(q.reshape(B, L, N_KV_HEADS, GROUP, HEAD_DIM), q_g).transpose(0, 2, 3, 1, 4)
    k = rms_norm(k.reshape(B, L, N_KV_HEADS, HEAD_DIM), k_g).transpose(0, 2, 1, 3)
    v = v.reshape(B, L, N_KV_HEADS, HEAD_DIM).transpose(0, 2, 1, 3)
    return q, k, v


def attend(q, k, v):
    s = jnp.einsum('bgrqd,bgkd->bgrqk', q * (HEAD_DIM ** -0.5), k).astype(jnp.float32)
    p = jax.nn.softmax(s, axis=-1).astype(v.dtype)
    return jnp.einsum('bgrqk,bgkd->bgrqd', p, v)


def latent_attention(q, k_all, v_all):
    B, G, R, L, hd = q.shape
    nb = L // Q_BLOCK
    qb = q.reshape(B, G, R, nb, Q_BLOCK, hd).transpose(3, 0, 1, 2, 4, 5)
    ob = lax.map(lambda qblk: attend(qblk, k_all, v_all), qb)
    return ob.transpose(1, 0, 4, 2, 3, 5).reshape(B, L, G * R * hd)


def context_attention(q, k, v):
    B, G, R, T, hd = q.shape
    return attend(q, k, v).transpose(0, 3, 1, 2, 4).reshape(B, T, G * R * hd)


def window_mean(x, w):
    B, L, C = x.shape
    a = w // 2
    b = w - a
    xf = x.astype(jnp.float32)
    xp = jnp.pad(xf, ((0, 0), (a, b), (0, 0)))
    cs = jnp.concatenate([jnp.zeros((B, 1, C), jnp.float32), jnp.cumsum(xp, axis=1)], axis=1)
    s = cs[:, w:w + L] - cs[:, :L]
    t = jnp.arange(L, dtype=jnp.int32)
    cnt = (jnp.minimum(t + b, L) - jnp.maximum(t - a, 0)).astype(jnp.float32)
    return (s / cnt[None, :, None]).astype(x.dtype)


def pool_mixer(xb, w_pool, pool_scale):
    B, L, _ = xb.shape
    parts = []
    for gi, w in enumerate(POOL_WINDOWS):
        xg = xb[..., gi * POOL_GROUP:(gi + 1) * POOL_GROUP]
        parts.append(window_mean(xg, w) - xg)
    m = jnp.stack(parts, axis=2)
    y = jnp.einsum('blgc,gcd->blgd', m, w_pool).reshape(B, L, POOL_WIDTH)
    return y * pool_scale


def merge(o_attn, z_attn, p_pool, z_pool, g_attn, g_pool, w_br_a, w_br_b, w_out):
    ya = (o_attn * jax.nn.silu(z_attn)) @ w_br_a
    yb = (p_pool * jax.nn.silu(z_pool)) @ w_br_b
    y = jax.nn.sigmoid(g_attn) * ya + jax.nn.sigmoid(g_pool) * yb
    return y @ w_out


def layer(x, cx, c, c_ctx, cos, sin, w_mod, b_mod, w_in, q_g, k_g, w_pool, pool_scale,
          w_br_a, w_br_b, w_out, ln_g, ln_b, update_ctx):
    mod = jax.nn.silu(c) @ w_mod + b_mod
    mod_c = jax.nn.silu(c_ctx) @ w_mod + b_mod
    shift, scale, gate = mod[:, None, :D_MODEL], mod[:, None, D_MODEL:2 * D_MODEL], mod[:, None, 2 * D_MODEL:]
    shift_c, scale_c, gate_c = mod_c[:D_MODEL], mod_c[D_MODEL:2 * D_MODEL], mod_c[2 * D_MODEL:]
    u = x * (1.0 + scale) + shift
    uc = cx * (1.0 + scale_c) + shift_c

    q, k, v, za, xb, zb, ga, gb = split_in(u @ w_in)
    qc_, kc_, vc_, zac, xbc, zbc, gac, gbc = split_in(uc @ w_in)

    q, k, v = heads(q, k, v, q_g, k_g)
    q = apply_rope(q, cos, sin)
    k = apply_rope(k, cos, sin)
    qc, kc, vc = heads(qc_, kc_, vc_, q_g, k_g)

    k_all = jnp.concatenate([kc, k], axis=2)
    v_all = jnp.concatenate([vc, v], axis=2)
    o_lat = latent_attention(q, k_all, v_all)
    p_lat = pool_mixer(xb, w_pool, pool_scale)
    y_lat = merge(o_lat, za, p_lat, zb, ga, gb, w_br_a, w_br_b, w_out)
    x_new = layer_norm(ALPHA * x + gate * y_lat, ln_g, ln_b)

    if update_ctx:
        o_ctx = context_attention(qc, kc, vc)
        p_ctx = pool_mixer(xbc, w_pool, pool_scale)
        y_ctx = merge(o_ctx, zac, p_ctx, zbc, gac, gbc, w_br_a, w_br_b, w_out)
        cx = layer_norm(ALPHA * cx + gate_c * y_ctx, ln_g, ln_b)
    return x_new, cx


def setup_inputs(seed: int = 0) -> dict:
    key = jax.random.key(seed)
    ks = jax.random.split(key, 18)
    f32 = jnp.float32
    n = lambda k, s: jax.random.normal(k, s, f32)
    w_in = n(ks[6], (DEPTH, D_MODEL, N_IN)) * D_MODEL ** -0.5
    v_lo = ATTN_WIDTH + KV_WIDTH
    w_in = w_in.at[:, :, v_lo:v_lo + KV_WIDTH].multiply(BETA)
    return {
        "x": n(ks[0], (BATCH, SEQ, D_MODEL)),
        "c": n(ks[1], (BATCH, D_MODEL)),
        "ctx": n(ks[2], (BATCH, CTX_LEN, D_MODEL)),
        "c_ctx": n(ks[3], (D_MODEL,)),
        "w_mod": n(ks[4], (DEPTH, D_MODEL, 3 * D_MODEL)) * 0.5 * D_MODEL ** -0.5,
        "b_mod": n(ks[5], (DEPTH, 3 * D_MODEL)) * 0.01,
        "w_in": w_in,
        "q_norm": 1.0 + 0.02 * n(ks[7], (DEPTH, HEAD_DIM)),
        "k_norm": 1.0 + 0.02 * n(ks[8], (DEPTH, HEAD_DIM)),
        "w_pool": n(ks[9], (DEPTH, N_POOL_GROUPS, POOL_GROUP, POOL_GROUP)) * POOL_GROUP ** -0.5,
        "pool_scale": 1.0 + 0.1 * n(ks[10], (DEPTH, POOL_WIDTH)),
        "w_br_a": n(ks[11], (DEPTH, ATTN_WIDTH, D_MODEL)) * BETA * ATTN_WIDTH ** -0.5,
        "w_br_b": n(ks[12], (DEPTH, POOL_WIDTH, D_MODEL)) * BETA * POOL_WIDTH ** -0.5,
        "w_out": n(ks[13], (DEPTH, D_MODEL, D_MODEL)) * BETA * D_MODEL ** -0.5,
        "ln_g": 1.0 + 0.02 * n(ks[14], (DEPTH, D_MODEL)),
        "ln_b": 0.02 * n(ks[15], (DEPTH, D_MODEL)),
    }


def reference(x, c, ctx, c_ctx, w_mod, b_mod, w_in, q_norm, k_norm, w_pool, pool_scale,
              w_br_a, w_br_b, w_out, ln_g, ln_b):
    L = x.shape[1]
    cos, sin = rope_tables(L)
    cos = cos.astype(x.dtype)
    sin = sin.astype(x.dtype)
    cx = ctx
    for l in range(DEPTH):
        x, cx = layer(x, cx, c, c_ctx, cos, sin, w_mod[l], b_mod[l], w_in[l], q_norm[l], k_norm[l],
                      w_pool[l], pool_scale[l], w_br_a[l], w_br_b[l], w_out[l], ln_g[l], ln_b[l],
                      update_ctx=(l < DEPTH - 1))
    return x
```

```python
import functools
import math

import jax
import jax.numpy as jnp
from jax import lax
from jax.experimental import pallas as pl
from jax.experimental.pallas import tpu as pltpu

F32 = jnp.float32
BF16 = jnp.bfloat16

HEAD_DIM = 128
N_KV_HEADS = 2
GROUP = 4
GRID_W = 64
POOL_WINDOWS = (2, 4, 8, 16)
POOL_GROUP = 128
POOL_HALO = 8
ROPE_THETA = 10000.0
EPS = 1e-6

TOKEN_TILE = 256
KV_SUB = 256
KV_SUBS_PER_STEP = 2
VMEM_LIMIT_BYTES = 56 * 1024 * 1024

Q_SCALE = HEAD_DIM ** -0.5 * math.log2(math.e)


def _sigmoid(v):
    return 1.0 / (1.0 + jnp.exp(-v))


def _mod_kernel(c_ref, w_ref, b_ref, o_ref):
    cv = c_ref[...]
    s = cv * _sigmoid(cv)
    o_ref[0] = jnp.dot(s, w_ref[0], preferred_element_type=F32) + b_ref[0]


def _modulation(c_rows, w_mod, b_mod):
    depth, d, d3 = w_mod.shape
    rows = c_rows.shape[0]
    return pl.pallas_call(
        _mod_kernel,
        grid=(depth,),
        in_specs=[
            pl.BlockSpec((rows, d), lambda l: (0, 0)),
            pl.BlockSpec((1, d, d3), lambda l: (l, 0, 0)),
            pl.BlockSpec((1, 1, d3), lambda l: (l, 0, 0)),
        ],
        out_specs=pl.BlockSpec((1, rows, d3), lambda l: (l, 0, 0)),
        out_shape=jax.ShapeDtypeStruct((depth, rows, d3), F32),
        compiler_params=pltpu.CompilerParams(
            dimension_semantics=("arbitrary",), vmem_limit_bytes=VMEM_LIMIT_BYTES),
        name="modulation",
    )(c_rows, w_mod, b_mod.reshape(depth, 1, d3))


def _proj_kernel(x_ref, mod_ref, w_ref, qg_ref, kg_ref, cos_ref, slo_ref, shi_ref,
                 qT_ref, k_ref, vT_ref, sza_ref, xb_ref, szb_ref, sga_ref, sgb_ref,
                 *, d_model, attn_w, kv_w, pool_w):
    x = x_ref[0]
    mod = mod_ref[0, 0]
    shift = mod[:, :d_model]
    scale = mod[:, d_model:2 * d_model]
    u = (x * (1.0 + scale) + shift).astype(BF16)

    cos = cos_ref[...]
    slo = slo_ref[...]
    shi = shi_ref[...]

    def norm_rope(h, g):
        ms = jnp.mean(h * h, axis=-1, keepdims=True)
        y = h * lax.rsqrt(ms + EPS) * g
        return y * cos + pltpu.roll(y, HEAD_DIM - 32, 1) * slo + pltpu.roll(y, 32, 1) * shi

    o = 0
    hq = jnp.dot(u, w_ref[:, o:o + attn_w], preferred_element_type=F32)
    o += attn_w
    qg = qg_ref[...]
    for h in range(attn_w // HEAD_DIM):
        qh = norm_rope(hq[:, h * HEAD_DIM:(h + 1) * HEAD_DIM], qg) * Q_SCALE
        qT_ref[0, h * HEAD_DIM:(h + 1) * HEAD_DIM, :] = qh.T.astype(BF16)

    hkv = jnp.dot(u, w_ref[:, o:o + 2 * kv_w], preferred_element_type=F32)
    o += 2 * kv_w
    kg = kg_ref[...]
    for g in range(kv_w // HEAD_DIM):
        kh = norm_rope(hkv[:, g * HEAD_DIM:(g + 1) * HEAD_DIM], kg)
        k_ref[0, :, g * HEAD_DIM:(g + 1) * HEAD_DIM] = kh.astype(BF16)
        vh = hkv[:, kv_w + g * HEAD_DIM:kv_w + (g + 1) * HEAD_DIM]
        vT_ref[0, g, 0] = vh.T.astype(BF16)

    za = jnp.dot(u, w_ref[:, o:o + attn_w], preferred_element_type=F32)
    o += attn_w
    sza_ref[0] = (za * _sigmoid(za)).astype(BF16)

    xb_ref[0] = jnp.dot(u, w_ref[:, o:o + pool_w], preferred_element_type=F32)
    o += pool_w

    zb = jnp.dot(u, w_ref[:, o:o + pool_w], preferred_element_type=F32)
    o += pool_w
    szb_ref[0] = (zb * _sigmoid(zb)).astype(BF16)

    ga = jnp.dot(u, w_ref[:, o:o + d_model], preferred_element_type=F32)
    o += d_model
    sga_ref[0] = _sigmoid(ga).astype(BF16)

    gb = jnp.dot(u, w_ref[:, o:o + d_model], preferred_element_type=F32)
    sgb_ref[0] = _sigmoid(gb).astype(BF16)


def _project(x_all, mod4, w_in, qg, kg, cos_t, slo_t, shi_t, *, attn_w, kv_w, pool_w):
    b, t_all, d = x_all.shape
    n_in = w_in.shape[1]
    tl = TOKEN_TILE
    nt = t_all // tl
    n_kv = kv_w // HEAD_DIM
    row = lambda bi, ti: (bi, ti, 0)
    const2 = lambda bi, ti: (0, 0)
    out_shape = (
        jax.ShapeDtypeStruct((b, attn_w, t_all), BF16),
        jax.ShapeDtypeStruct((b, t_all, kv_w), BF16),
        jax.ShapeDtypeStruct((b, n_kv, nt, HEAD_DIM, tl), BF16),
        jax.ShapeDtypeStruct((b, t_all, attn_w), BF16),
        jax.ShapeDtypeStruct((b, t_all, pool_w), F32),
        jax.ShapeDtypeStruct((b, t_all, pool_w), BF16),
        jax.ShapeDtypeStruct((b, t_all, d), BF16),
        jax.ShapeDtypeStruct((b, t_all, d), BF16),
    )
    out_specs = (
        pl.BlockSpec((1, attn_w, tl), lambda bi, ti: (bi, 0, ti)),
        pl.BlockSpec((1, tl, kv_w), row),
        pl.BlockSpec((1, n_kv, 1, HEAD_DIM, tl), lambda bi, ti: (bi, 0, ti, 0, 0)),
        pl.BlockSpec((1, tl, attn_w), row),
        pl.BlockSpec((1, tl, pool_w), row),
        pl.BlockSpec((1, tl, pool_w), row),
        pl.BlockSpec((1, tl, d), row),
        pl.BlockSpec((1, tl, d), row),
    )
    in_specs = [
        pl.BlockSpec((1, tl, d), row),
        pl.BlockSpec((1, 1, 1, 3 * d), lambda bi, ti: (bi, jnp.minimum(ti, 1), 0, 0)),
        pl.BlockSpec((d, n_in), const2),
        pl.BlockSpec((1, HEAD_DIM), const2),
        pl.BlockSpec((1, HEAD_DIM), const2),
        pl.BlockSpec((tl, HEAD_DIM), lambda bi, ti: (ti, 0)),
        pl.BlockSpec((tl, HEAD_DIM), lambda bi, ti: (ti, 0)),
        pl.BlockSpec((tl, HEAD_DIM), lambda bi, ti: (ti, 0)),
    ]
    return pl.pallas_call(
        functools.partial(_proj_kernel, d_model=d, attn_w=attn_w, kv_w=kv_w, pool_w=pool_w),
        grid=(b, nt),
        in_specs=in_specs,
        out_specs=out_specs,
        out_shape=out_shape,
        compiler_params=pltpu.CompilerParams(
            dimension_semantics=("arbitrary", "arbitrary"), vmem_limit_bytes=VMEM_LIMIT_BYTES),
        name="projection",
    )(x_all, mod4, w_in, qg, kg, cos_t, slo_t, shi_t)


def _attn_kernel(qT_ref, k_ref, vT_ref, o_ref, acc_ref, m_ref, l_ref, *, n_latent_steps):
    t = pl.program_id(2)
    acc_ref[...] = jnp.zeros_like(acc_ref)
    m_ref[...] = jnp.full_like(m_ref, -1e30)
    l_ref[...] = jnp.zeros_like(l_ref)

    def chunk(j0, nj):
        start = pl.multiple_of(j0 * KV_SUB, KV_SUB)
        kc = k_ref[0, pl.ds(start, nj * KV_SUB), :]
        for r in range(GROUP):
            qT = qT_ref[0, r * HEAD_DIM:(r + 1) * HEAD_DIM, :]
            s = jnp.dot(kc, qT, preferred_element_type=F32)
            m_old = m_ref[r]
            m_new = jnp.maximum(m_old, jnp.max(s, axis=0, keepdims=True))
            alpha = jnp.exp2(m_old - m_new)
            p = jnp.exp2(s - m_new)
            l_ref[r] = alpha * l_ref[r] + jnp.sum(p, axis=0, keepdims=True)
            pb = p.astype(BF16)
            pv = jnp.dot(vT_ref[0, 0, j0], pb[0:KV_SUB], preferred_element_type=F32)
            for jj in range(1, nj):
                pv += jnp.dot(vT_ref[0, 0, j0 + jj], pb[jj * KV_SUB:(jj + 1) * KV_SUB],
                              preferred_element_type=F32)
            acc_ref[r] = alpha * acc_ref[r] + pv
            m_ref[r] = m_new

    chunk(0, 1)
    n_steps = jnp.where(t == 0, 0, n_latent_steps)

    def body(i, carry):
        chunk(1 + i * KV_SUBS_PER_STEP, KV_SUBS_PER_STEP)
        return carry

    lax.fori_loop(0, n_steps, body, 0)

    for r in range(GROUP):
        o = acc_ref[r] * (1.0 / l_ref[r])
        o_ref[0, :, r * HEAD_DIM:(r + 1) * HEAD_DIM] = o.T.astype(BF16)


def _attention(qT, k, vT):
    b, attn_w, t_all = qT.shape
    n_kv, nt = vT.shape[1], vT.shape[2]
    tl = TOKEN_TILE
    gw = GROUP * HEAD_DIM
    n_latent_steps = (nt - 1) // KV_SUBS_PER_STEP
    return pl.pallas_call(
        functools.partial(_attn_kernel, n_latent_steps=n_latent_steps),
        grid=(b, n_kv, nt),
        in_specs=[
            pl.BlockSpec((1, gw, tl), lambda bi, g, ti: (bi, g, ti)),
            pl.BlockSpec((1, t_all, HEAD_DIM), lambda bi, g, ti: (bi, 0, g)),
            pl.BlockSpec((1, 1, nt, HEAD_DIM, KV_SUB), lambda bi, g, ti: (bi, g, 0, 0, 0)),
        ],
        out_specs=pl.BlockSpec((1, tl, gw), lambda bi, g, ti: (bi, ti, g)),
        out_shape=jax.ShapeDtypeStruct((b, t_all, attn_w), BF16),
        scratch_shapes=[
            pltpu.VMEM((GROUP, HEAD_DIM, tl), F32),
            pltpu.VMEM((GROUP, 1, tl), F32),
            pltpu.VMEM((GROUP, 1, tl), F32),
        ],
        compiler_params=pltpu.CompilerParams(
            dimension_semantics=("arbitrary", "arbitrary", "arbitrary"),
            vmem_limit_bytes=VMEM_LIMIT_BYTES),
        name="attention",
    )(qT, k, vT)


def _merge_kernel(o_ref, sza_ref, xb_ref, xbp_ref, xbn_ref, szb_ref, sga_ref, sgb_ref, x_ref,
                  mod_ref, wpool_ref, pscale_ref, wa_ref, wb_ref, wo_ref, lng_ref, lnb_ref,
                  out_ref, ext_ref, *, d_model, n_tiles, ctx_len, seq_len, alpha_res):
    t = pl.program_id(1)
    tl = TOKEN_TILE
    h = POOL_HALO

    prev_ok = t >= 2
    next_ok = jnp.logical_and(t >= 1, t <= n_tiles - 2)
    ext_ref[0:h, :] = jnp.where(prev_ok, xbp_ref[0], 0.0)
    ext_ref[h:h + tl, :] = xb_ref[0]
    ext_ref[h + tl:2 * h + tl, :] = jnp.where(next_ok, xbn_ref[0], 0.0)

    pos = lax.broadcasted_iota(jnp.int32, (tl, 1), 0) + jnp.where(t == 0, 0, (t - 1) * tl)
    seq = jnp.where(t == 0, ctx_len, seq_len)

    parts = []
    for gi, w in enumerate(POOL_WINDOWS):
        a = w // 2
        bb = w - a
        lanes = slice(gi * POOL_GROUP, (gi + 1) * POOL_GROUP)
        acc = ext_ref[h - a:h - a + tl, lanes]
        for j in range(-a + 1, bb):
            acc = acc + ext_ref[h + j:h + j + tl, lanes]
        cnt = (jnp.minimum(pos + bb, seq) - jnp.maximum(pos - a, 0)).astype(F32)
        m = acc / cnt - ext_ref[h:h + tl, lanes]
        parts.append(jnp.dot(m.astype(BF16), wpool_ref[gi], preferred_element_type=F32))
    p_pool = jnp.concatenate(parts, axis=-1) * pscale_ref[...]

    b_in = (p_pool * szb_ref[0].astype(F32)).astype(BF16)
    yb = jnp.dot(b_in, wb_ref[...], preferred_element_type=F32)
    a_in = o_ref[0] * sza_ref[0]
    ya = jnp.dot(a_in, wa_ref[...], preferred_element_type=F32)
    y = (sga_ref[0].astype(F32) * ya + sgb_ref[0].astype(F32) * yb).astype(BF16)
    yo = jnp.dot(y, wo_ref[...], preferred_element_type=F32)

    gate = mod_ref[0, 0][:, 2 * d_model:]
    r = alpha_res * x_ref[0] + gate * yo
    mu = jnp.mean(r, axis=-1, keepdims=True)
    rc = r - mu
    var = jnp.mean(rc * rc, axis=-1, keepdims=True)
    out_ref[0] = rc * lax.rsqrt(var + EPS) * lng_ref[...] + lnb_ref[...]


def _merge(o, sza, xb, szb, sga, sgb, x_all, mod4, wpool, pscale, wa, wb, wo, lng, lnb,
           *, ctx_len, alpha_res):
    b, t_all, d = x_all.shape
    attn_w = o.shape[2]
    pool_w = xb.shape[2]
    tl = TOKEN_TILE
    nt = t_all // tl
    hb = tl // POOL_HALO
    n_hblocks = t_all // POOL_HALO
    row = lambda bi, ti: (bi, ti, 0)
    const2 = lambda bi, ti: (0, 0)
    in_specs = [
        pl.BlockSpec((1, tl, attn_w), row),
        pl.BlockSpec((1, tl, attn_w), row),
        pl.BlockSpec((1, tl, pool_w), row),
        pl.BlockSpec((1, POOL_HALO, pool_w), lambda bi, ti: (bi, jnp.maximum(ti * hb - 1, 0), 0)),
        pl.BlockSpec((1, POOL_HALO, pool_w),
                     lambda bi, ti: (bi, jnp.minimum((ti + 1) * hb, n_hblocks - 1), 0)),
        pl.BlockSpec((1, tl, pool_w), row),
        pl.BlockSpec((1, tl, d), row),
        pl.BlockSpec((1, tl, d), row),
        pl.BlockSpec((1, tl, d), row),
        pl.BlockSpec((1, 1, 1, 3 * d), lambda bi, ti: (bi, jnp.minimum(ti, 1), 0, 0)),
        pl.BlockSpec(wpool.shape, lambda bi, ti: (0, 0, 0)),
        pl.BlockSpec((1, pool_w), const2),
        pl.BlockSpec(wa.shape, const2),
        pl.BlockSpec(wb.shape, const2),
        pl.BlockSpec(wo.shape, const2),
        pl.BlockSpec((1, d), const2),
        pl.BlockSpec((1, d), const2),
    ]
    return pl.pallas_call(
        functools.partial(_merge_kernel, d_model=d, n_tiles=nt, ctx_len=ctx_len,
                          seq_len=t_all - ctx_len, alpha_res=alpha_res),
        grid=(b, nt),
        in_specs=in_specs,
        out_specs=pl.BlockSpec((1, tl, d), row),
        out_shape=jax.ShapeDtypeStruct((b, t_all, d), F32),
        scratch_shapes=[pltpu.VMEM((tl + 2 * POOL_HALO, pool_w), F32)],
        compiler_params=pltpu.CompilerParams(
            dimension_semantics=("arbitrary", "arbitrary"), vmem_limit_bytes=VMEM_LIMIT_BYTES),
        name="merge",
    )(o, sza, xb, xb, xb, szb, sga, sgb, x_all, mod4, wpool, pscale, wa, wb, wo, lng, lnb)


def _rope_tables(seq_len, ctx_len):
    n_freq = HEAD_DIM // 4
    tpos = jnp.arange(seq_len, dtype=jnp.int32)
    rowp = (tpos // GRID_W).astype(F32)
    colp = (tpos % GRID_W).astype(F32)
    inv_freq = 1.0 / (ROPE_THETA ** (jnp.arange(n_freq, dtype=F32) / n_freq))
    ar = rowp[:, None] * inv_freq[None, :]
    ac = colp[:, None] * inv_freq[None, :]
    ang = jnp.concatenate([ar, ar, ac, ac], axis=-1)
    cos = jnp.cos(ang)
    sin = jnp.sin(ang)
    lane = jnp.arange(HEAD_DIM)
    lower = ((lane // n_freq) % 2 == 0)[None, :]
    slo = jnp.where(lower, -sin, 0.0)
    shi = jnp.where(lower, 0.0, sin)
    ones = jnp.ones((ctx_len, HEAD_DIM), F32)
    zeros = jnp.zeros((ctx_len, HEAD_DIM), F32)
    return (jnp.concatenate([ones, cos], axis=0),
            jnp.concatenate([zeros, slo], axis=0),
            jnp.concatenate([zeros, shi], axis=0))


def kernel(x, c, ctx, c_ctx, w_mod, b_mod, w_in, q_norm, k_norm, w_pool, pool_scale,
           w_br_a, w_br_b, w_out, ln_g, ln_b):
    b, seq_len, d = x.shape
    ctx_len = ctx.shape[1]
    depth = w_in.shape[0]
    attn_w = w_br_a.shape[1]
    pool_w = w_br_b.shape[1]
    kv_w = (w_in.shape[2] - 2 * attn_w - 2 * pool_w - 2 * d) // 2
    assert ctx_len == TOKEN_TILE and seq_len % (TOKEN_TILE * KV_SUBS_PER_STEP) == 0
    assert attn_w // HEAD_DIM == GROUP * N_KV_HEADS and kv_w == N_KV_HEADS * HEAD_DIM
    alpha_res = (2 * depth) ** 0.25

    x_all = jnp.concatenate([ctx, x], axis=1)

    c_rows = jnp.zeros((8, d), F32).at[0].set(c_ctx).at[1:1 + b].set(c)
    mod = _modulation(c_rows, w_mod, b_mod)

    cos_t, slo_t, shi_t = _rope_tables(seq_len, ctx_len)

    w_in_b = w_in.astype(BF16)
    w_pool_b = w_pool.astype(BF16)
    w_a_b = w_br_a.astype(BF16)
    w_b_b = w_br_b.astype(BF16)
    w_o_b = w_out.astype(BF16)

    for l in range(depth):
        mod_ctx = jnp.broadcast_to(mod[l, 0][None], (b, 3 * d))
        mod4 = jnp.stack([mod_ctx, mod[l, 1:1 + b]], axis=1).reshape(b, 2, 1, 3 * d)
        qT, k, vT, sza, xb, szb, sga, sgb = _project(
            x_all, mod4, w_in_b[l], q_norm[l].reshape(1, HEAD_DIM), k_norm[l].reshape(1, HEAD_DIM),
            cos_t, slo_t, shi_t, attn_w=attn_w, kv_w=kv_w, pool_w=pool_w)
        o = _attention(qT, k, vT)
        x_all = _merge(o, sza, xb, szb, sga, sgb, x_all, mod4, w_pool_b[l],
                       pool_scale[l].reshape(1, pool_w), w_a_b[l], w_b_b[l], w_o_b[l],
                       ln_g[l].reshape(1, d), ln_b[l].reshape(1, d),
                       ctx_len=ctx_len, alpha_res=alpha_res)
    return x_all[:, ctx_len:]
```

```python
import functools
import math

import jax
import jax.numpy as jnp
from jax import lax
from jax.experimental import pallas as pl
from jax.experimental.pallas import tpu as pltpu

F32 = jnp.float32
BF16 = jnp.bfloat16

HEAD_DIM = 128
N_KV_HEADS = 2
GROUP = 4
GRID_W = 64
POOL_WINDOWS = (2, 4, 8, 16)
POOL_GROUP = 128
POOL_HALO = 8
ROPE_THETA = 10000.0
EPS = 1e-6

TOKEN_TILE = 256
KV_SUB = 256
KV_SUBS_PER_STEP = 2
VMEM_LIMIT_BYTES = 56 * 1024 * 1024

Q_SCALE = HEAD_DIM ** -0.5 * math.log2(math.e)
REDUCE_CHAINS = 4
LAGGED_SCORE_BOUND = 48.0


def _sigmoid(v):
    return 1.0 / (1.0 + jnp.exp(-v))


def _mod_kernel(c_ref, w_ref, b_ref, o_ref):
    cv = c_ref[...]
    s = cv * _sigmoid(cv)
    o_ref[0] = jnp.dot(s, w_ref[0], preferred_element_type=F32) + b_ref[0]


def _modulation(c_rows, w_mod, b_mod):
    depth, d, d3 = w_mod.shape
    rows = c_rows.shape[0]
    return pl.pallas_call(
        _mod_kernel,
        grid=(depth,),
        in_specs=[
            pl.BlockSpec((rows, d), lambda l: (0, 0)),
            pl.BlockSpec((1, d, d3), lambda l: (l, 0, 0)),
            pl.BlockSpec((1, 1, d3), lambda l: (l, 0, 0)),
        ],
        out_specs=pl.BlockSpec((1, rows, d3), lambda l: (l, 0, 0)),
        out_shape=jax.ShapeDtypeStruct((depth, rows, d3), F32),
        compiler_params=pltpu.CompilerParams(
            dimension_semantics=("arbitrary",), vmem_limit_bytes=VMEM_LIMIT_BYTES),
        name="modulation",
    )(c_rows, w_mod, b_mod.reshape(depth, 1, d3))


def _proj_kernel(x_ref, mod_ref, w_ref, qg_ref, kg_ref, cos_ref, slo_ref, shi_ref,
                 qT_ref, k_ref, vT_ref, sza_ref, xb_ref, szb_ref, sga_ref, sgb_ref,
                 *, d_model, attn_w, kv_w, pool_w):
    x = x_ref[0]
    mod = mod_ref[0, 0]
    shift = mod[:, :d_model]
    scale = mod[:, d_model:2 * d_model]
    u = (x * (1.0 + scale) + shift).astype(BF16)

    cos = cos_ref[...]
    slo = slo_ref[...]
    shi = shi_ref[...]

    def norm_rope(h, g):
        ms = jnp.mean(h * h, axis=-1, keepdims=True)
        y = h * lax.rsqrt(ms + EPS) * g
        return y * cos + pltpu.roll(y, HEAD_DIM - 32, 1) * slo + pltpu.roll(y, 32, 1) * shi

    o = 0
    hq = jnp.dot(u, w_ref[:, o:o + attn_w], preferred_element_type=F32)
    o += attn_w
    qg = qg_ref[...]
    for h in range(attn_w // HEAD_DIM):
        qh = norm_rope(hq[:, h * HEAD_DIM:(h + 1) * HEAD_DIM], qg) * Q_SCALE
        qT_ref[0, h * HEAD_DIM:(h + 1) * HEAD_DIM, :] = qh.T.astype(BF16)

    hkv = jnp.dot(u, w_ref[:, o:o + 2 * kv_w], preferred_element_type=F32)
    o += 2 * kv_w
    kg = kg_ref[...]
    for g in range(kv_w // HEAD_DIM):
        kh = norm_rope(hkv[:, g * HEAD_DIM:(g + 1) * HEAD_DIM], kg)
        k_ref[0, :, g * HEAD_DIM:(g + 1) * HEAD_DIM] = kh.astype(BF16)
        vh = hkv[:, kv_w + g * HEAD_DIM:kv_w + (g + 1) * HEAD_DIM]
        vT_ref[0, g, 0] = vh.T.astype(BF16)

    za = jnp.dot(u, w_ref[:, o:o + attn_w], preferred_element_type=F32)
    o += attn_w
    sza_ref[0] = (za * _sigmoid(za)).astype(BF16)

    xb_ref[0] = jnp.dot(u, w_ref[:, o:o + pool_w], preferred_element_type=F32)
    o += pool_w

    zb = jnp.dot(u, w_ref[:, o:o + pool_w], preferred_element_type=F32)
    o += pool_w
    szb_ref[0] = (zb * _sigmoid(zb)).astype(BF16)

    ga = jnp.dot(u, w_ref[:, o:o + d_model], preferred_element_type=F32)
    o += d_model
    sga_ref[0] = _sigmoid(ga).astype(BF16)

    gb = jnp.dot(u, w_ref[:, o:o + d_model], preferred_element_type=F32)
    sgb_ref[0] = _sigmoid(gb).astype(BF16)


def _project(x_all, mod4, w_in, qg, kg, cos_t, slo_t, shi_t, *, attn_w, kv_w, pool_w):
    b, t_all, d = x_all.shape
    n_in = w_in.shape[1]
    tl = TOKEN_TILE
    nt = t_all // tl
    n_kv = kv_w // HEAD_DIM
    row = lambda bi, ti: (bi, ti, 0)
    const2 = lambda bi, ti: (0, 0)
    out_shape = (
        jax.ShapeDtypeStruct((b, attn_w, t_all), BF16),
        jax.ShapeDtypeStruct((b, t_all, kv_w), BF16),
        jax.ShapeDtypeStruct((b, n_kv, nt, HEAD_DIM, tl), BF16),
        jax.ShapeDtypeStruct((b, t_all, attn_w), BF16),
        jax.ShapeDtypeStruct((b, t_all, pool_w), F32),
        jax.ShapeDtypeStruct((b, t_all, pool_w), BF16),
        jax.ShapeDtypeStruct((b, t_all, d), BF16),
        jax.ShapeDtypeStruct((b, t_all, d), BF16),
    )
    out_specs = (
        pl.BlockSpec((1, attn_w, tl), lambda bi, ti: (bi, 0, ti)),
        pl.BlockSpec((1, tl, kv_w), row),
        pl.BlockSpec((1, n_kv, 1, HEAD_DIM, tl), lambda bi, ti: (bi, 0, ti, 0, 0)),
        pl.BlockSpec((1, tl, attn_w), row),
        pl.BlockSpec((1, tl, pool_w), row),
        pl.BlockSpec((1, tl, pool_w), row),
        pl.BlockSpec((1, tl, d), row),
        pl.BlockSpec((1, tl, d), row),
    )
    in_specs = [
        pl.BlockSpec((1, tl, d), row),
        pl.BlockSpec((1, 1, 1, 3 * d), lambda bi, ti: (bi, jnp.minimum(ti, 1), 0, 0)),
        pl.BlockSpec((d, n_in), const2),
        pl.BlockSpec((1, HEAD_DIM), const2),
        pl.BlockSpec((1, HEAD_DIM), const2),
        pl.BlockSpec((tl, HEAD_DIM), lambda bi, ti: (ti, 0)),
        pl.BlockSpec((tl, HEAD_DIM), lambda bi, ti: (ti, 0)),
        pl.BlockSpec((tl, HEAD_DIM), lambda bi, ti: (ti, 0)),
    ]
    return pl.pallas_call(
        functools.partial(_proj_kernel, d_model=d, attn_w=attn_w, kv_w=kv_w, pool_w=pool_w),
        grid=(b, nt),
        in_specs=in_specs,
        out_specs=out_specs,
        out_shape=out_shape,
        compiler_params=pltpu.CompilerParams(
            dimension_semantics=("arbitrary", "arbitrary"), vmem_limit_bytes=VMEM_LIMIT_BYTES),
        name="projection",
    )(x_all, mod4, w_in, qg, kg, cos_t, slo_t, shi_t)


def _col_reduce(x, op):
    nblk = x.shape[0] // 8
    accs = [x[8 * j:8 * j + 8] for j in range(min(REDUCE_CHAINS, nblk))]
    for i in range(len(accs), nblk):
        accs[i % REDUCE_CHAINS] = op(accs[i % REDUCE_CHAINS], x[8 * i:8 * i + 8])
    while len(accs) > 1:
        accs = [op(accs[2 * i], accs[2 * i + 1]) for i in range(len(accs) // 2)]
    return accs[0]


def _attn_kernel(qT_ref, k_ref, vT_ref, o_ref, acc_ref, m_ref, l_ref, *, n_latent_steps, lagged):
    t = pl.program_id(2)

    def chunk(j0, nj, first):
        start = pl.multiple_of(j0 * KV_SUB, KV_SUB)
        kc = k_ref[0, pl.ds(start, nj * KV_SUB), :]

        def scores(r):
            qT = qT_ref[0, r * HEAD_DIM:(r + 1) * HEAD_DIM, :]
            return jnp.dot(kc, qT, preferred_element_type=F32)

        s_next = scores(0)
        for r in range(GROUP):
            s = s_next
            if r + 1 < GROUP:
                s_next = scores(r + 1)
            cmax =jnp.max(_col_reduce(s, jnp.maximum), axis=0, keepdims=True)
            if first:
                m_old = m_new = cmax
            else:
                m_old = m_ref[r]
                m_new = jnp.maximum(m_old, cmax)
            p = jnp.exp2(s - (m_old if lagged else m_new))
            lsum = jnp.sum(_col_reduce(p, jnp.add), axis=0, keepdims=True)
            pb = p.astype(BF16)
            pv = jnp.dot(vT_ref[0, 0, j0], pb[0:KV_SUB], preferred_element_type=F32)
            for jj in range(1, nj):
                pv += jnp.dot(vT_ref[0, 0, j0 + jj], pb[jj * KV_SUB:(jj + 1) * KV_SUB],
                              preferred_element_type=F32)
            if first:
                l_ref[r] = lsum
                acc_ref[r] = pv
            else:
                alpha = jnp.exp2(m_old - m_new)
                if lagged:
                    l_ref[r] = alpha * (l_ref[r] + lsum)
                    acc_ref[r] = alpha * (acc_ref[r] + pv)
                else:
                    l_ref[r] = alpha * l_ref[r] + lsum
                    acc_ref[r] = alpha * acc_ref[r] + pv
            m_ref[r] = m_new

    chunk(0, 1, True)
    n_steps = jnp.where(t == 0, 0, n_latent_steps)

    def body(i, carry):
        chunk(1 + i * KV_SUBS_PER_STEP, KV_SUBS_PER_STEP, False)
        return carry

    lax.fori_loop(0, n_steps, body, 0)

    for r in range(GROUP):
        o = acc_ref[r] * (1.0 / l_ref[r])
        o_ref[0, :, r * HEAD_DIM:(r + 1) * HEAD_DIM] = o.T.astype(BF16)


def _attention(qT, k, vT, *, lagged):
    b, attn_w, t_all = qT.shape
    n_kv, nt = vT.shape[1], vT.shape[2]
    tl = TOKEN_TILE
    gw = GROUP * HEAD_DIM
    n_latent_steps = (nt - 1) // KV_SUBS_PER_STEP
    return pl.pallas_call(
        functools.partial(_attn_kernel, n_latent_steps=n_latent_steps, lagged=lagged),
        grid=(b, n_kv, nt),
        in_specs=[
            pl.BlockSpec((1, gw, tl), lambda bi, g, ti: (bi, g, ti)),
            pl.BlockSpec((1, t_all, HEAD_DIM), lambda bi, g, ti: (bi, 0, g)),
            pl.BlockSpec((1, 1, nt, HEAD_DIM, KV_SUB), lambda bi, g, ti: (bi, g, 0, 0, 0)),
        ],
        out_specs=pl.BlockSpec((1, tl, gw), lambda bi, g, ti: (bi, ti, g)),
        out_shape=jax.ShapeDtypeStruct((b, t_all, attn_w), BF16),
        scratch_shapes=[
            pltpu.VMEM((GROUP, HEAD_DIM, tl), F32),
            pltpu.VMEM((GROUP, 1, tl), F32),
            pltpu.VMEM((GROUP, 1, tl), F32),
        ],
        compiler_params=pltpu.CompilerParams(
            dimension_semantics=("arbitrary", "arbitrary", "arbitrary"),
            vmem_limit_bytes=VMEM_LIMIT_BYTES),
        name="attention_lagged" if lagged else "attention_exact",
    )(qT, k, vT)


def _attention_dispatch(qT, k, vT, q_gain, k_gain):
    score_bound = HEAD_DIM * Q_SCALE * jnp.max(jnp.abs(q_gain)) * jnp.max(jnp.abs(k_gain))
    return lax.cond(score_bound < LAGGED_SCORE_BOUND,
                    functools.partial(_attention, lagged=True),
                    functools.partial(_attention, lagged=False),
                    qT, k, vT)


def _merge_kernel(o_ref, sza_ref, xb_ref, xbp_ref, xbn_ref, szb_ref, sga_ref, sgb_ref, x_ref,
                  mod_ref, wpool_ref, pscale_ref, wa_ref, wb_ref, wo_ref, lng_ref, lnb_ref,
                  out_ref, ext_ref, *, d_model, n_tiles, ctx_len, seq_len, alpha_res):
    t = pl.program_id(1)
    tl = TOKEN_TILE
    h = POOL_HALO

    prev_ok = t >= 2
    next_ok = jnp.logical_and(t >= 1, t <= n_tiles - 2)
    ext_ref[0:h, :] = jnp.where(prev_ok, xbp_ref[0], 0.0)
    ext_ref[h:h + tl, :] = xb_ref[0]
    ext_ref[h + tl:2 * h + tl, :] = jnp.where(next_ok, xbn_ref[0], 0.0)

    pos = lax.broadcasted_iota(jnp.int32, (tl, 1), 0) + jnp.where(t == 0, 0, (t - 1) * tl)
    seq = jnp.where(t == 0, ctx_len, seq_len)

    parts = []
    for gi, w in enumerate(POOL_WINDOWS):
        a = w // 2
        bb = w - a
        lanes = slice(gi * POOL_GROUP, (gi + 1) * POOL_GROUP)
        acc = ext_ref[h - a:h - a + tl, lanes]
        for j in range(-a + 1, bb):
            acc = acc + ext_ref[h + j:h + j + tl, lanes]
        cnt = (jnp.minimum(pos + bb, seq) - jnp.maximum(pos - a, 0)).astype(F32)
        m = acc / cnt - ext_ref[h:h + tl, lanes]
        parts.append(jnp.dot(m.astype(BF16), wpool_ref[gi], preferred_element_type=F32))
    p_pool = jnp.concatenate(parts, axis=-1) * pscale_ref[...]

    b_in = (p_pool * szb_ref[0].astype(F32)).astype(BF16)
    yb = jnp.dot(b_in, wb_ref[...], preferred_element_type=F32)
    a_in = o_ref[0] * sza_ref[0]
    ya = jnp.dot(a_in, wa_ref[...], preferred_element_type=F32)
    y = (sga_ref[0].astype(F32) * ya + sgb_ref[0].astype(F32) * yb).astype(BF16)
    yo = jnp.dot(y, wo_ref[...], preferred_element_type=F32)

    gate = mod_ref[0, 0][:, 2 * d_model:]
    r = alpha_res * x_ref[0] + gate * yo
    mu = jnp.mean(r, axis=-1, keepdims=True)
    rc = r - mu
    var = jnp.mean(rc * rc, axis=-1, keepdims=True)
    out_ref[0] = rc * lax.rsqrt(var + EPS) * lng_ref[...] + lnb_ref[...]


def _merge(o, sza, xb, szb, sga, sgb, x_all, mod4, wpool, pscale, wa, wb, wo, lng, lnb,
           *, ctx_len, alpha_res):
    b, t_all, d = x_all.shape
    attn_w = o.shape[2]
    pool_w = xb.shape[2]
    tl = TOKEN_TILE
    nt = t_all // tl
    hb = tl // POOL_HALO
    n_hblocks = t_all // POOL_HALO
    row = lambda bi, ti: (bi, ti, 0)
    const2 = lambda bi, ti: (0, 0)
    in_specs = [
        pl.BlockSpec((1, tl, attn_w), row),
        pl.BlockSpec((1, tl, attn_w), row),
        pl.BlockSpec((1, tl, pool_w), row),
        pl.BlockSpec((1, POOL_HALO, pool_w), lambda bi, ti: (bi, jnp.maximum(ti * hb - 1, 0), 0)),
        pl.BlockSpec((1, POOL_HALO, pool_w),
                     lambda bi, ti: (bi, jnp.minimum((ti + 1) * hb, n_hblocks - 1), 0)),
        pl.BlockSpec((1, tl, pool_w), row),
        pl.BlockSpec((1, tl, d), row),
        pl.BlockSpec((1, tl, d), row),
        pl.BlockSpec((1, tl, d), row),
        pl.BlockSpec((1, 1, 1, 3 * d), lambda bi, ti: (bi, jnp.minimum(ti, 1), 0, 0)),
        pl.BlockSpec(wpool.shape, lambda bi, ti: (0, 0, 0)),
        pl.BlockSpec((1, pool_w), const2),
        pl.BlockSpec(wa.shape, const2),
        pl.BlockSpec(wb.shape, const2),
        pl.BlockSpec(wo.shape, const2),
        pl.BlockSpec((1, d), const2),
        pl.BlockSpec((1, d), const2),
    ]
    return pl.pallas_call(
        functools.partial(_merge_kernel, d_model=d, n_tiles=nt, ctx_len=ctx_len,
                          seq_len=t_all - ctx_len, alpha_res=alpha_res),
        grid=(b, nt),
        in_specs=in_specs,
        out_specs=pl.BlockSpec((1, tl, d), row),
        out_shape=jax.ShapeDtypeStruct((b, t_all, d), F32),
        scratch_shapes=[pltpu.VMEM((tl + 2 * POOL_HALO, pool_w), F32)],
        compiler_params=pltpu.CompilerParams(
            dimension_semantics=("arbitrary", "arbitrary"), vmem_limit_bytes=VMEM_LIMIT_BYTES),
        name="merge",
    )(o, sza, xb, xb, xb, szb, sga, sgb, x_all, mod4, wpool, pscale, wa, wb, wo, lng, lnb)


def _rope_tables(seq_len, ctx_len):
    n_freq = HEAD_DIM // 4
    tpos = jnp.arange(seq_len, dtype=jnp.int32)
    rowp = (tpos // GRID_W).astype(F32)
    colp = (tpos % GRID_W).astype(F32)
    inv_freq = 1.0 / (ROPE_THETA ** (jnp.arange(n_freq, dtype=F32) / n_freq))
    ar = rowp[:, None] * inv_freq[None, :]
    ac = colp[:, None] * inv_freq[None, :]
    ang = jnp.concatenate([ar, ar, ac, ac], axis=-1)
    cos = jnp.cos(ang)
    sin = jnp.sin(ang)
    lane = jnp.arange(HEAD_DIM)
    lower = ((lane // n_freq) % 2 == 0)[None, :]
    slo = jnp.where(lower, -sin, 0.0)
    shi = jnp.where(lower, 0.0, sin)
    ones = jnp.ones((ctx_len, HEAD_DIM), F32)
    zeros = jnp.zeros((ctx_len, HEAD_DIM), F32)
    return (jnp.concatenate([ones, cos], axis=0),
            jnp.concatenate([zeros, slo], axis=0),
            jnp.concatenate([zeros, shi], axis=0))


def kernel(x, c, ctx, c_ctx, w_mod, b_mod, w_in, q_norm, k_norm, w_pool, pool_scale,
           w_br_a, w_br_b, w_out, ln_g, ln_b):
    b, seq_len, d = x.shape
    ctx_len = ctx.shape[1]
    depth = w_in.shape[0]
    attn_w = w_br_a.shape[1]
    pool_w = w_br_b.shape[1]
    kv_w = (w_in.shape[2] - 2 * attn_w - 2 * pool_w - 2 * d) // 2
    assert ctx_len == TOKEN_TILE and seq_len % (TOKEN_TILE * KV_SUBS_PER_STEP) == 0
    assert attn_w // HEAD_DIM == GROUP * N_KV_HEADS and kv_w == N_KV_HEADS * HEAD_DIM
    alpha_res = (2 * depth) ** 0.25

    x_all = jnp.concatenate([ctx, x], axis=1)

    c_rows = jnp.zeros((8, d), F32).at[0].set(c_ctx).at[1:1 + b].set(c)
    mod = _modulation(c_rows, w_mod, b_mod)

    cos_t, slo_t, shi_t = _rope_tables(seq_len, ctx_len)

    w_in_b = w_in.astype(BF16)
    w_pool_b = w_pool.astype(BF16)
    w_a_b = w_br_a.astype(BF16)
    w_b_b = w_br_b.astype(BF16)
    w_o_b = w_out.astype(BF16)

    for l in range(depth):
        mod_ctx = jnp.broadcast_to(mod[l, 0][None], (b, 3 * d))
        mod4 = jnp.stack([mod_ctx, mod[l, 1:1 + b]], axis=1).reshape(b, 2, 1, 3 * d)
        qT, k, vT, sza, xb, szb, sga, sgb = _project(
            x_all, mod4, w_in_b[l], q_norm[l].reshape(1, HEAD_DIM), k_norm[l].reshape(1, HEAD_DIM),
            cos_t, slo_t, shi_t, attn_w=attn_w, kv_w=kv_w, pool_w=pool_w)
        o = _attention_dispatch(qT, k, vT, q_norm[l], k_norm[l])
        x_all = _merge(o, sza, xb, szb, sga, sgb, x_all, mod4, w_pool_b[l],
                       pool_scale[l].reshape(1, pool_w), w_a_b[l], w_b_b[l], w_o_b[l],
                       ln_g[l].reshape(1, d), ln_b[l].reshape(1, d),
                       ctx_len=ctx_len, alpha_res=alpha_res)
    return x_all[:, ctx_len:]
```

```python
import functools
import math

import jax
import jax.numpy as jnp
from jax import lax
from jax.experimental import pallas as pl
from jax.experimental.pallas import tpu as pltpu

F32 = jnp.float32
BF16 = jnp.bfloat16

HEAD_DIM = 128
N_KV_HEADS = 2
GROUP = 4
GRID_W = 64
POOL_WINDOWS = (2, 4, 8, 16)
POOL_GROUP = 128
POOL_HALO = 8
ROPE_THETA = 10000.0
EPS = 1e-6

TOKEN_TILE = 256
KV_SUB = 256
KV_SUBS_PER_STEP = 8
VMEM_LIMIT_BYTES = 56 * 1024 * 1024

Q_SCALE = HEAD_DIM ** -0.5 * math.log2(math.e)
REDUCE_CHAINS = 4
LAGGED_SCORE_BOUND = 48.0


def _sigmoid(v):
    return 1.0 / (1.0 + jnp.exp(-v))


def _mod_kernel(c_ref, w_ref, b_ref, o_ref):
    cv = c_ref[...]
    s = cv * _sigmoid(cv)
    o_ref[0] = jnp.dot(s, w_ref[0], preferred_element_type=F32) + b_ref[0]


def _modulation(c_rows, w_mod, b_mod):
    depth, d, d3 = w_mod.shape
    rows = c_rows.shape[0]
    return pl.pallas_call(
        _mod_kernel,
        grid=(depth,),
        in_specs=[
            pl.BlockSpec((rows, d), lambda l: (0, 0)),
            pl.BlockSpec((1, d, d3), lambda l: (l, 0, 0)),
            pl.BlockSpec((1, 1, d3), lambda l: (l, 0, 0)),
        ],
        out_specs=pl.BlockSpec((1, rows, d3), lambda l: (l, 0, 0)),
        out_shape=jax.ShapeDtypeStruct((depth, rows, d3), F32),
        compiler_params=pltpu.CompilerParams(
            dimension_semantics=("arbitrary",), vmem_limit_bytes=VMEM_LIMIT_BYTES),
        name="modulation",
    )(c_rows, w_mod, b_mod.reshape(depth, 1, d3))


def _proj_kernel(x_ref, mod_ref, w_ref, qg_ref, kg_ref, cos_ref, slo_ref, shi_ref,
                 qT_ref, k_ref, vT_ref, sza_ref, xb_ref, szb_ref, sga_ref, sgb_ref,
                 *, d_model, attn_w, kv_w, pool_w):
    x = x_ref[0]
    mod = mod_ref[0, 0]
    shift = mod[:, :d_model]
    scale = mod[:, d_model:2 * d_model]
    u = (x * (1.0 + scale) + shift).astype(BF16)

    cos = cos_ref[...]
    slo = slo_ref[...]
    shi = shi_ref[...]

    def norm_rope(h, g):
        ms = jnp.mean(h * h, axis=-1, keepdims=True)
        y = h * lax.rsqrt(ms + EPS) * g
        return y * cos + pltpu.roll(y, HEAD_DIM - 32, 1) * slo + pltpu.roll(y, 32, 1) * shi

    o = 0
    hq = jnp.dot(u, w_ref[:, o:o + attn_w], preferred_element_type=F32)
    o += attn_w
    qg = qg_ref[...]
    for h in range(attn_w // HEAD_DIM):
        qh = norm_rope(hq[:, h * HEAD_DIM:(h + 1) * HEAD_DIM], qg) * Q_SCALE
        qT_ref[0, h * HEAD_DIM:(h + 1) * HEAD_DIM, :] = qh.T.astype(BF16)

    hkv = jnp.dot(u, w_ref[:, o:o + 2 * kv_w], preferred_element_type=F32)
    o += 2 * kv_w
    kg = kg_ref[...]
    for g in range(kv_w // HEAD_DIM):
        kh = norm_rope(hkv[:, g * HEAD_DIM:(g + 1) * HEAD_DIM], kg)
        k_ref[0, :, g * HEAD_DIM:(g + 1) * HEAD_DIM] = kh.astype(BF16)
        vh = hkv[:, kv_w + g * HEAD_DIM:kv_w + (g + 1) * HEAD_DIM]
        vT_ref[0, g, 0] = vh.T.astype(BF16)

    za = jnp.dot(u, w_ref[:, o:o + attn_w], preferred_element_type=F32)
    o += attn_w
    sza_ref[0] = (za * _sigmoid(za)).astype(BF16)

    xb_ref[0] = jnp.dot(u, w_ref[:, o:o + pool_w], preferred_element_type=F32)
    o += pool_w

    zb = jnp.dot(u, w_ref[:, o:o + pool_w], preferred_element_type=F32)
    o += pool_w
    szb_ref[0] = (zb * _sigmoid(zb)).astype(BF16)

    ga = jnp.dot(u, w_ref[:, o:o + d_model], preferred_element_type=F32)
    o += d_model
    sga_ref[0] = _sigmoid(ga).astype(BF16)

    gb = jnp.dot(u, w_ref[:, o:o + d_model], preferred_element_type=F32)
    sgb_ref[0] = _sigmoid(gb).astype(BF16)


def _project(x_all, mod4, w_in, qg, kg, cos_t, slo_t, shi_t, *, attn_w, kv_w, pool_w):
    b, t_all, d = x_all.shape
    n_in = w_in.shape[1]
    tl = TOKEN_TILE
    nt = t_all // tl
    n_kv = kv_w // HEAD_DIM
    row = lambda bi, ti: (bi, ti, 0)
    const2 = lambda bi, ti: (0, 0)
    out_shape = (
        jax.ShapeDtypeStruct((b, attn_w, t_all), BF16),
        jax.ShapeDtypeStruct((b, t_all, kv_w), BF16),
        jax.ShapeDtypeStruct((b, n_kv, nt, HEAD_DIM, tl), BF16),
        jax.ShapeDtypeStruct((b, t_all, attn_w), BF16),
        jax.ShapeDtypeStruct((b, t_all, pool_w), F32),
        jax.ShapeDtypeStruct((b, t_all, pool_w), BF16),
        jax.ShapeDtypeStruct((b, t_all, d), BF16),
        jax.ShapeDtypeStruct((b, t_all, d), BF16),
    )
    out_specs = (
        pl.BlockSpec((1, attn_w, tl), lambda bi, ti: (bi, 0, ti)),
        pl.BlockSpec((1, tl, kv_w), row),
        pl.BlockSpec((1, n_kv, 1, HEAD_DIM, tl), lambda bi, ti: (bi, 0, ti, 0, 0)),
        pl.BlockSpec((1, tl, attn_w), row),
        pl.BlockSpec((1, tl, pool_w), row),
        pl.BlockSpec((1, tl, pool_w), row),
        pl.BlockSpec((1, tl, d), row),
        pl.BlockSpec((1, tl, d), row),
    )
    in_specs = [
        pl.BlockSpec((1, tl, d), row),
        pl.BlockSpec((1, 1, 1, 3 * d), lambda bi, ti: (bi, jnp.minimum(ti, 1), 0, 0)),
        pl.BlockSpec((d, n_in), const2),
        pl.BlockSpec((1, HEAD_DIM), const2),
        pl.BlockSpec((1, HEAD_DIM), const2),
        pl.BlockSpec((tl, HEAD_DIM), lambda bi, ti: (ti, 0)),
        pl.BlockSpec((tl, HEAD_DIM), lambda bi, ti: (ti, 0)),
        pl.BlockSpec((tl, HEAD_DIM), lambda bi, ti: (ti, 0)),
    ]
    return pl.pallas_call(
        functools.partial(_proj_kernel, d_model=d, attn_w=attn_w, kv_w=kv_w, pool_w=pool_w),
        grid=(b, nt),
        in_specs=in_specs,
        out_specs=out_specs,
        out_shape=out_shape,
        compiler_params=pltpu.CompilerParams(
            dimension_semantics=("arbitrary", "arbitrary"), vmem_limit_bytes=VMEM_LIMIT_BYTES),
        name="projection",
    )(x_all, mod4, w_in, qg, kg, cos_t, slo_t, shi_t)


def _col_reduce(x, op):
    nblk = x.shape[0] // 8
    accs = [x[8 * j:8 * j + 8] for j in range(min(REDUCE_CHAINS, nblk))]
    for i in range(len(accs), nblk):
        accs[i % REDUCE_CHAINS] = op(accs[i % REDUCE_CHAINS], x[8 * i:8 * i + 8])
    while len(accs) > 1:
        accs = [op(accs[2 * i], accs[2 * i + 1]) for i in range(len(accs) // 2)]
    return accs[0]


def _attn_kernel(qT_ref, k_ref, vT_ref, o_ref, acc_ref, m_ref, l_ref, *, n_latent_steps, lagged):
    t = pl.program_id(2)

    def chunk(j0, nj, first):
        start = pl.multiple_of(j0 * KV_SUB, KV_SUB)
        kc = k_ref[0, pl.ds(start, nj * KV_SUB), :]

        def scores(r):
            qT = qT_ref[0, r * HEAD_DIM:(r + 1) * HEAD_DIM, :]
            return jnp.dot(kc, qT, preferred_element_type=F32)

        s_next = scores(0)
        for r in range(GROUP):
            s = s_next
            if r + 1 < GROUP:
                s_next = scores(r + 1)
            cmax =jnp.max(_col_reduce(s, jnp.maximum), axis=0, keepdims=True)
            if first:
                m_old = m_new = cmax
            else:
                m_old = m_ref[r]
                m_new = jnp.maximum(m_old, cmax)
            p = jnp.exp2(s - (m_old if lagged else m_new))
            lsum = jnp.sum(_col_reduce(p, jnp.add), axis=0, keepdims=True)
            pb = p.astype(BF16)
            pv = jnp.dot(vT_ref[0, 0, j0], pb[0:KV_SUB], preferred_element_type=F32)
            for jj in range(1, nj):
                pv += jnp.dot(vT_ref[0, 0, j0 + jj], pb[jj * KV_SUB:(jj + 1) * KV_SUB],
                              preferred_element_type=F32)
            if first:
                l_ref[r] = lsum
                acc_ref[r] = pv
            else:
                alpha = jnp.exp2(m_old - m_new)
                if lagged:
                    l_ref[r] = alpha * (l_ref[r] + lsum)
                    acc_ref[r] = alpha * (acc_ref[r] + pv)
                else:
                    l_ref[r] = alpha * l_ref[r] + lsum
                    acc_ref[r] = alpha * acc_ref[r] + pv
            m_ref[r] = m_new

    chunk(0, 1, True)
    n_steps = jnp.where(t == 0, 0, n_latent_steps)

    def body(i, carry):
        chunk(1 + i * KV_SUBS_PER_STEP, KV_SUBS_PER_STEP, False)
        return carry

    lax.fori_loop(0, n_steps, body, 0)

    for r in range(GROUP):
        o = acc_ref[r] * (1.0 / l_ref[r])
        o_ref[0, :, r * HEAD_DIM:(r + 1) * HEAD_DIM] = o.T.astype(BF16)


def _attention(qT, k, vT, *, lagged):
    b, attn_w, t_all = qT.shape
    n_kv, nt = vT.shape[1], vT.shape[2]
    tl = TOKEN_TILE
    gw = GROUP * HEAD_DIM
    n_latent_steps = (nt - 1) // KV_SUBS_PER_STEP
    return pl.pallas_call(
        functools.partial(_attn_kernel, n_latent_steps=n_latent_steps, lagged=lagged),
        grid=(b, n_kv, nt),
        in_specs=[
            pl.BlockSpec((1, gw, tl), lambda bi, g, ti: (bi, g, ti)),
            pl.BlockSpec((1, t_all, HEAD_DIM), lambda bi, g, ti: (bi, 0, g)),
            pl.BlockSpec((1, 1, nt, HEAD_DIM, KV_SUB), lambda bi, g, ti: (bi, g, 0, 0, 0)),
        ],
        out_specs=pl.BlockSpec((1, tl, gw), lambda bi, g, ti: (bi, ti, g)),
        out_shape=jax.ShapeDtypeStruct((b, t_all, attn_w), BF16),
        scratch_shapes=[
            pltpu.VMEM((GROUP, HEAD_DIM, tl), F32),
            pltpu.VMEM((GROUP, 1, tl), F32),
            pltpu.VMEM((GROUP, 1, tl), F32),
        ],
        compiler_params=pltpu.CompilerParams(
            dimension_semantics=("arbitrary", "arbitrary", "arbitrary"),
            vmem_limit_bytes=VMEM_LIMIT_BYTES),
        name="attention_lagged" if lagged else "attention_exact",
    )(qT, k, vT)


def _attention_dispatch(qT, k, vT, q_gain, k_gain):
    score_bound = HEAD_DIM * Q_SCALE * jnp.max(jnp.abs(q_gain)) * jnp.max(jnp.abs(k_gain))
    return lax.cond(score_bound < LAGGED_SCORE_BOUND,
                    functools.partial(_attention, lagged=True),
                    functools.partial(_attention, lagged=False),
                    qT, k, vT)


def _merge_kernel(o_ref, sza_ref, xb_ref, xbp_ref, xbn_ref, szb_ref, sga_ref, sgb_ref, x_ref,
                  mod_ref, wpool_ref, pscale_ref, wa_ref, wb_ref, wo_ref, lng_ref, lnb_ref,
                  out_ref, ext_ref, *, d_model, n_tiles, ctx_len, seq_len, alpha_res):
    t = pl.program_id(1)
    tl = TOKEN_TILE
    h = POOL_HALO

    prev_ok = t >= 2
    next_ok = jnp.logical_and(t >= 1, t <= n_tiles - 2)
    ext_ref[0:h, :] = jnp.where(prev_ok, xbp_ref[0], 0.0)
    ext_ref[h:h + tl, :] = xb_ref[0]
    ext_ref[h + tl:2 * h + tl, :] = jnp.where(next_ok, xbn_ref[0], 0.0)

    pos = lax.broadcasted_iota(jnp.int32, (tl, 1), 0) + jnp.where(t == 0, 0, (t - 1) * tl)
    seq = jnp.where(t == 0, ctx_len, seq_len)

    parts = []
    for gi, w in enumerate(POOL_WINDOWS):
        a = w // 2
        bb = w - a
        lanes = slice(gi * POOL_GROUP, (gi + 1) * POOL_GROUP)
        acc = ext_ref[h - a:h - a + tl, lanes]
        for j in range(-a + 1, bb):
            acc = acc + ext_ref[h + j:h + j + tl, lanes]
        cnt = (jnp.minimum(pos + bb, seq) - jnp.maximum(pos - a, 0)).astype(F32)
        m = acc / cnt - ext_ref[h:h + tl, lanes]
        parts.append(jnp.dot(m.astype(BF16), wpool_ref[gi], preferred_element_type=F32))
    p_pool = jnp.concatenate(parts, axis=-1) * pscale_ref[...]

    b_in = (p_pool * szb_ref[0].astype(F32)).astype(BF16)
    yb = jnp.dot(b_in, wb_ref[...], preferred_element_type=F32)
    a_in = o_ref[0] * sza_ref[0]
    ya = jnp.dot(a_in, wa_ref[...], preferred_element_type=F32)
    y = (sga_ref[0].astype(F32) * ya + sgb_ref[0].astype(F32) * yb).astype(BF16)
    yo = jnp.dot(y, wo_ref[...], preferred_element_type=F32)

    gate = mod_ref[0, 0][:, 2 * d_model:]
    r = alpha_res * x_ref[0] + gate * yo
    mu = jnp.mean(r, axis=-1, keepdims=True)
    rc = r - mu
    var = jnp.mean(rc * rc, axis=-1, keepdims=True)
    out_ref[0] = rc * lax.rsqrt(var + EPS) * lng_ref[...] + lnb_ref[...]


def _merge(o, sza, xb, szb, sga, sgb, x_all, mod4, wpool, pscale, wa, wb, wo, lng, lnb,
           *, ctx_len, alpha_res):
    b, t_all, d = x_all.shape
    attn_w = o.shape[2]
    pool_w = xb.shape[2]
    tl = TOKEN_TILE
    nt = t_all // tl
    hb = tl // POOL_HALO
    n_hblocks = t_all // POOL_HALO
    row = lambda bi, ti: (bi, ti, 0)
    const2 = lambda bi, ti: (0, 0)
    in_specs = [
        pl.BlockSpec((1, tl, attn_w), row),
        pl.BlockSpec((1, tl, attn_w), row),
        pl.BlockSpec((1, tl, pool_w), row),
        pl.BlockSpec((1, POOL_HALO, pool_w), lambda bi, ti: (bi, jnp.maximum(ti * hb - 1, 0), 0)),
        pl.BlockSpec((1, POOL_HALO, pool_w),
                     lambda bi, ti: (bi, jnp.minimum((ti + 1) * hb, n_hblocks - 1), 0)),
        pl.BlockSpec((1, tl, pool_w), row),
        pl.BlockSpec((1, tl, d), row),
        pl.BlockSpec((1, tl, d), row),
        pl.BlockSpec((1, tl, d), row),
        pl.BlockSpec((1, 1, 1, 3 * d), lambda bi, ti: (bi, jnp.minimum(ti, 1), 0, 0)),
        pl.BlockSpec(wpool.shape, lambda bi, ti: (0, 0, 0)),
        pl.BlockSpec((1, pool_w), const2),
        pl.BlockSpec(wa.shape, const2),
        pl.BlockSpec(wb.shape, const2),
        pl.BlockSpec(wo.shape, const2),
        pl.BlockSpec((1, d), const2),
        pl.BlockSpec((1, d), const2),
    ]
    return pl.pallas_call(
        functools.partial(_merge_kernel, d_model=d, n_tiles=nt, ctx_len=ctx_len,
                          seq_len=t_all - ctx_len, alpha_res=alpha_res),
        grid=(b, nt),
        in_specs=in_specs,
        out_specs=pl.BlockSpec((1, tl, d), row),
        out_shape=jax.ShapeDtypeStruct((b, t_all, d), F32),
        scratch_shapes=[pltpu.VMEM((tl + 2 * POOL_HALO, pool_w), F32)],
        compiler_params=pltpu.CompilerParams(
            dimension_semantics=("arbitrary", "arbitrary"), vmem_limit_bytes=VMEM_LIMIT_BYTES),
        name="merge",
    )(o, sza, xb, xb, xb, szb, sga, sgb, x_all, mod4, wpool, pscale, wa, wb, wo, lng, lnb)


def _rope_tables(seq_len, ctx_len):
    n_freq = HEAD_DIM // 4
    tpos = jnp.arange(seq_len, dtype=jnp.int32)
    rowp = (tpos // GRID_W).astype(F32)
    colp = (tpos % GRID_W).astype(F32)
    inv_freq = 1.0 / (ROPE_THETA ** (jnp.arange(n_freq, dtype=F32) / n_freq))
    ar = rowp[:, None] * inv_freq[None, :]
    ac = colp[:, None] * inv_freq[None, :]
    ang = jnp.concatenate([ar, ar, ac, ac], axis=-1)
    cos = jnp.cos(ang)
    sin = jnp.sin(ang)
    lane = jnp.arange(HEAD_DIM)
    lower = ((lane // n_freq) % 2 == 0)[None, :]
    slo = jnp.where(lower, -sin, 0.0)
    shi = jnp.where(lower, 0.0, sin)
    ones = jnp.ones((ctx_len, HEAD_DIM), F32)
    zeros = jnp.zeros((ctx_len, HEAD_DIM), F32)
    return (jnp.concatenate([ones, cos], axis=0),
            jnp.concatenate([zeros, slo], axis=0),
            jnp.concatenate([zeros, shi], axis=0))


def kernel(x, c, ctx, c_ctx, w_mod, b_mod, w_in, q_norm, k_norm, w_pool, pool_scale,
           w_br_a, w_br_b, w_out, ln_g, ln_b):
    b, seq_len, d = x.shape
    ctx_len = ctx.shape[1]
    depth = w_in.shape[0]
    attn_w = w_br_a.shape[1]
    pool_w = w_br_b.shape[1]
    kv_w = (w_in.shape[2] - 2 * attn_w - 2 * pool_w - 2 * d) // 2
    assert ctx_len == TOKEN_TILE and seq_len % (TOKEN_TILE * KV_SUBS_PER_STEP) == 0
    assert attn_w // HEAD_DIM == GROUP * N_KV_HEADS and kv_w == N_KV_HEADS * HEAD_DIM
    alpha_res = (2 * depth) ** 0.25

    x_all = jnp.concatenate([ctx, x], axis=1)

    c_rows = jnp.zeros((8, d), F32).at[0].set(c_ctx).at[1:1 + b].set(c)
    mod = _modulation(c_rows, w_mod, b_mod)

    cos_t, slo_t, shi_t = _rope_tables(seq_len, ctx_len)

    w_in_b = w_in.astype(BF16)
    w_pool_b = w_pool.astype(BF16)
    w_a_b = w_br_a.astype(BF16)
    w_b_b = w_br_b.astype(BF16)
    w_o_b = w_out.astype(BF16)

    for l in range(depth):
        mod_ctx = jnp.broadcast_to(mod[l, 0][None], (b, 3 * d))
        mod4 = jnp.stack([mod_ctx, mod[l, 1:1 + b]], axis=1).reshape(b, 2, 1, 3 * d)
        qT, k, vT, sza, xb, szb, sga, sgb = _project(
            x_all, mod4, w_in_b[l], q_norm[l].reshape(1, HEAD_DIM), k_norm[l].reshape(1, HEAD_DIM),
            cos_t, slo_t, shi_t, attn_w=attn_w, kv_w=kv_w, pool_w=pool_w)
        o = _attention_dispatch(qT, k, vT, q_norm[l], k_norm[l])
        x_all = _merge(o, sza, xb, szb, sga, sgb, x_all, mod4, w_pool_b[l],
                       pool_scale[l].reshape(1, pool_w), w_a_b[l], w_b_b[l], w_o_b[l],
                       ln_g[l].reshape(1, d), ln_b[l].reshape(1, d),
                       ctx_len=ctx_len, alpha_res=alpha_res)
    return x_all[:, ctx_len:]
```

```python
import functools
import math

import jax
import jax.numpy as jnp
from jax import lax
from jax.experimental import pallas as pl
from jax.experimental.pallas import tpu as pltpu

F32 = jnp.float32
BF16 = jnp.bfloat16

HEAD_DIM = 128
MXU_DIM = 256
N_KV_HEADS = 2
GROUP = 4
GRID_W = 64
POOL_WINDOWS = (2, 4, 8, 16)
POOL_GROUP = 128
POOL_HALO = 8
ROPE_THETA = 10000.0
EPS = 1e-6

TOKEN_TILE = 256
KV_SUB = 256
KV_SUBS_PER_STEP = 8
VMEM_LIMIT_BYTES = 56 * 1024 * 1024

Q_SCALE = HEAD_DIM ** -0.5 * math.log2(math.e)
REDUCE_CHAINS = 4
LAGGED_SCORE_BOUND = 48.0


def _sigmoid(v):
    return 1.0 / (1.0 + jnp.exp(-v))


def _mod_kernel(c_ref, w_ref, b_ref, o_ref):
    cv = c_ref[...]
    s = cv * _sigmoid(cv)
    o_ref[0] = jnp.dot(s, w_ref[0], preferred_element_type=F32) + b_ref[0]


def _modulation(c_rows, w_mod, b_mod):
    depth, d, d3 = w_mod.shape
    rows = c_rows.shape[0]
    return pl.pallas_call(
        _mod_kernel,
        grid=(depth,),
        in_specs=[
            pl.BlockSpec((rows, d), lambda l: (0, 0)),
            pl.BlockSpec((1, d, d3), lambda l: (l, 0, 0)),
            pl.BlockSpec((1, 1, d3), lambda l: (l, 0, 0)),
        ],
        out_specs=pl.BlockSpec((1, rows, d3), lambda l: (l, 0, 0)),
        out_shape=jax.ShapeDtypeStruct((depth, rows, d3), F32),
        compiler_params=pltpu.CompilerParams(
            dimension_semantics=("arbitrary",), vmem_limit_bytes=VMEM_LIMIT_BYTES),
        name="modulation",
    )(c_rows, w_mod, b_mod.reshape(depth, 1, d3))


def _proj_kernel(x_ref, mod_ref, w_ref, qg_ref, kg_ref, cos_ref, slo_ref, shi_ref,
                 qT_ref, k_ref, vT_ref, sza_ref, xb_ref, szb_ref, sga_ref, sgb_ref,
                 *, d_model, attn_w, kv_w, pool_w):
    x = x_ref[0]
    mod = mod_ref[0, 0]
    shift = mod[:, :d_model]
    scale = mod[:, d_model:2 * d_model]
    u = (x * (1.0 + scale) + shift).astype(BF16)

    cos = cos_ref[...]
    slo = slo_ref[...]
    shi = shi_ref[...]

    def norm_rope(h, g):
        ms = jnp.mean(h * h, axis=-1, keepdims=True)
        y = h * lax.rsqrt(ms + EPS) * g
        return y * cos + pltpu.roll(y, HEAD_DIM - 32, 1) * slo + pltpu.roll(y, 32, 1) * shi

    o = 0
    hq = jnp.dot(u, w_ref[:, o:o + attn_w], preferred_element_type=F32)
    o += attn_w
    qg = qg_ref[...]
    for h in range(attn_w // HEAD_DIM):
        qh = norm_rope(hq[:, h * HEAD_DIM:(h + 1) * HEAD_DIM], qg) * Q_SCALE
        qT_ref[0, h * HEAD_DIM:(h + 1) * HEAD_DIM, :] = qh.T.astype(BF16)

    hkv = jnp.dot(u, w_ref[:, o:o + 2 * kv_w], preferred_element_type=F32)
    o += 2 * kv_w
    kg = kg_ref[...]
    for g in range(kv_w // HEAD_DIM):
        kh = norm_rope(hkv[:, g * HEAD_DIM:(g + 1) * HEAD_DIM], kg)
        k_ref[0, :, g * MXU_DIM:g * MXU_DIM + HEAD_DIM] = kh.astype(BF16)
        lane = lax.broadcasted_iota(jnp.int32, (x.shape[0], MXU_DIM - HEAD_DIM), 1)
        k_ref[0, :, g * MXU_DIM + HEAD_DIM:(g + 1) * MXU_DIM] = (lane == 0).astype(BF16)
        vh = hkv[:, kv_w + g * HEAD_DIM:kv_w + (g + 1) * HEAD_DIM]
        vT_ref[0, g, 0] = vh.T.astype(BF16)

    za = jnp.dot(u, w_ref[:, o:o + attn_w], preferred_element_type=F32)
    o += attn_w
    sza_ref[0] = (za * _sigmoid(za)).astype(BF16)

    xb_ref[0] = jnp.dot(u, w_ref[:, o:o + pool_w], preferred_element_type=F32)
    o += pool_w

    zb = jnp.dot(u, w_ref[:, o:o + pool_w], preferred_element_type=F32)
    o += pool_w
    szb_ref[0] = (zb * _sigmoid(zb)).astype(BF16)

    ga = jnp.dot(u, w_ref[:, o:o + d_model], preferred_element_type=F32)
    o += d_model
    sga_ref[0] = _sigmoid(ga).astype(BF16)

    gb = jnp.dot(u, w_ref[:, o:o + d_model], preferred_element_type=F32)
    sgb_ref[0] = _sigmoid(gb).astype(BF16)


def _project(x_all, mod4, w_in, qg, kg, cos_t, slo_t, shi_t, *, attn_w, kv_w, pool_w):
    b, t_all, d = x_all.shape
    n_in = w_in.shape[1]
    tl = TOKEN_TILE
    nt = t_all // tl
    n_kv = kv_w // HEAD_DIM
    row = lambda bi, ti: (bi, ti, 0)
    const2 = lambda bi, ti: (0, 0)
    out_shape = (
        jax.ShapeDtypeStruct((b, attn_w, t_all), BF16),
        jax.ShapeDtypeStruct((b, t_all, n_kv * MXU_DIM), BF16),
        jax.ShapeDtypeStruct((b, n_kv, nt, HEAD_DIM, tl), BF16),
        jax.ShapeDtypeStruct((b, t_all, attn_w), BF16),
        jax.ShapeDtypeStruct((b, t_all, pool_w), F32),
        jax.ShapeDtypeStruct((b, t_all, pool_w), BF16),
        jax.ShapeDtypeStruct((b, t_all, d), BF16),
        jax.ShapeDtypeStruct((b, t_all, d), BF16),
    )
    out_specs = (
        pl.BlockSpec((1, attn_w, tl), lambda bi, ti: (bi, 0, ti)),
        pl.BlockSpec((1, tl, n_kv * MXU_DIM), row),
        pl.BlockSpec((1, n_kv, 1, HEAD_DIM, tl), lambda bi, ti: (bi, 0, ti, 0, 0)),
        pl.BlockSpec((1, tl, attn_w), row),
        pl.BlockSpec((1, tl, pool_w), row),
        pl.BlockSpec((1, tl, pool_w), row),
        pl.BlockSpec((1, tl, d), row),
        pl.BlockSpec((1, tl, d), row),
    )
    in_specs = [
        pl.BlockSpec((1, tl, d), row),
        pl.BlockSpec((1, 1, 1, 3 * d), lambda bi, ti: (bi, jnp.minimum(ti, 1), 0, 0)),
        pl.BlockSpec((d, n_in), const2),
        pl.BlockSpec((1, HEAD_DIM), const2),
        pl.BlockSpec((1, HEAD_DIM), const2),
        pl.BlockSpec((tl, HEAD_DIM), lambda bi, ti: (ti, 0)),
        pl.BlockSpec((tl, HEAD_DIM), lambda bi, ti: (ti, 0)),
        pl.BlockSpec((tl, HEAD_DIM), lambda bi, ti: (ti, 0)),
    ]
    return pl.pallas_call(
        functools.partial(_proj_kernel, d_model=d, attn_w=attn_w, kv_w=kv_w, pool_w=pool_w),
        grid=(b, nt),
        in_specs=in_specs,
        out_specs=out_specs,
        out_shape=out_shape,
        compiler_params=pltpu.CompilerParams(
            dimension_semantics=("arbitrary", "arbitrary"), vmem_limit_bytes=VMEM_LIMIT_BYTES),
        name="projection",
    )(x_all, mod4, w_in, qg, kg, cos_t, slo_t, shi_t)


def _col_reduce(x, op):
    nblk = x.shape[0] // 8
    accs = [x[8 * j:8 * j + 8] for j in range(min(REDUCE_CHAINS, nblk))]
    for i in range(len(accs), nblk):
        accs[i % REDUCE_CHAINS] = op(accs[i % REDUCE_CHAINS], x[8 * i:8 * i + 8])
    while len(accs) > 1:
        accs = [op(accs[2 * i], accs[2 * i + 1]) for i in range(len(accs) // 2)]
    return accs[0]


def _attn_kernel(qT_ref, k_ref, vT_ref, o_ref, w_ref, s_ref, acc_ref, m_ref, l_ref,
                 *, n_latent_steps, lagged):
    t = pl.program_id(2)
    tl = o_ref.shape[1]
    ref_rows = 16

    for r in range(GROUP):
        w_ref[r, 0:HEAD_DIM, :] = qT_ref[0, r * HEAD_DIM:(r + 1) * HEAD_DIM, :]
        w_ref[r, HEAD_DIM:, :] = jnp.zeros((MXU_DIM - HEAD_DIM, tl), BF16)

    def set_reference(r, m):
        row = lax.broadcasted_iota(jnp.int32, (ref_rows, tl), 0)
        w_ref[r, HEAD_DIM:HEAD_DIM + ref_rows, :] = jnp.where(row == 0, -m, 0.0).astype(BF16)

    def keys(j0, nj):
        start = pl.multiple_of(j0 * KV_SUB, KV_SUB)
        return k_ref[0, pl.ds(start, nj * KV_SUB), :]

    def scores(kc, r):
        return jnp.dot(kc, w_ref[r], preferred_element_type=F32)

    def head_update(r, s, j0, nj, first):
        cmax = jnp.max(_col_reduce(s, jnp.maximum), axis=0, keepdims=True)
        if first:
            m_old = m_new = cmax.astype(BF16).astype(F32) if lagged else cmax
            p = jnp.exp2(s - m_new)
        elif lagged:
            m_old = m_ref[r]
            m_new = (m_old + jnp.maximum(cmax, 0.0)).astype(BF16).astype(F32)
            p = jnp.exp2(s)
        else:
            m_old = m_ref[r]
            m_new = jnp.maximum(m_old, cmax)
            p = jnp.exp2(s - m_new)
        lsum = jnp.sum(_col_reduce(p, jnp.add), axis=0, keepdims=True)
        pb = p.astype(BF16)
        pv = jnp.dot(vT_ref[0, 0, j0], pb[0:KV_SUB], preferred_element_type=F32)
        for jj in range(1, nj):
            pv += jnp.dot(vT_ref[0, 0, j0 + jj], pb[jj * KV_SUB:(jj + 1) * KV_SUB],
                          preferred_element_type=F32)
        if first:
            l_ref[r] = lsum
            acc_ref[r] = pv
        else:
            alpha = jnp.exp2(m_old - m_new)
            if lagged:
                l_ref[r] = alpha * (l_ref[r] + lsum)
                acc_ref[r] = alpha * (acc_ref[r] + pv)
            else:
                l_ref[r] = alpha * l_ref[r] + lsum
                acc_ref[r] = alpha * acc_ref[r] + pv
        m_ref[r] = m_new
        if lagged:
            set_reference(r, m_new)

    nj = KV_SUBS_PER_STEP

    kc = keys(0, 1)
    s = scores(kc, 0)
    for r in range(GROUP):
        s_next = scores(kc, r + 1) if r + 1 < GROUP else scores(keys(1, nj), 0)
        head_update(r, s, 0, 1, True)
        s = s_next
    s_ref[...] = s
    n_steps = jnp.where(t == 0, 0, n_latent_steps)

    def body(i, carry):
        j0 = 1 + i * nj
        j0_next = jnp.minimum(j0 + nj, 1 + (n_latent_steps - 1) * nj)
        kc = keys(j0, nj)
        s = s_ref[...]
        for r in range(GROUP):
            s_next = scores(kc, r + 1) if r + 1 < GROUP else scores(keys(j0_next, nj), 0)
            head_update(r, s, j0, nj, False)
            s = s_next
        s_ref[...] = s
        return carry

    lax.fori_loop(0, n_steps, body, 0)

    for r in range(GROUP):
        o = acc_ref[r] * (1.0 / l_ref[r])
        o_ref[0, :, r * HEAD_DIM:(r + 1) * HEAD_DIM] = o.T.astype(BF16)


def _attention(qT, k, vT, *, lagged):
    b, attn_w, t_all = qT.shape
    n_kv, nt = vT.shape[1], vT.shape[2]
    tl = TOKEN_TILE
    gw = GROUP * HEAD_DIM
    n_latent_steps = (nt - 1) // KV_SUBS_PER_STEP
    return pl.pallas_call(
        functools.partial(_attn_kernel, n_latent_steps=n_latent_steps, lagged=lagged),
        grid=(b, n_kv, nt),
        in_specs=[
            pl.BlockSpec((1, gw, tl), lambda bi, g, ti: (bi, g, ti)),
            pl.BlockSpec((1, t_all, MXU_DIM), lambda bi, g, ti: (bi, 0, g)),
            pl.BlockSpec((1, 1, nt, HEAD_DIM, KV_SUB), lambda bi, g, ti: (bi, g, 0, 0, 0)),
        ],
        out_specs=pl.BlockSpec((1, tl, gw), lambda bi, g, ti: (bi, ti, g)),
        out_shape=jax.ShapeDtypeStruct((b, t_all, attn_w), BF16),
        scratch_shapes=[
            pltpu.VMEM((GROUP, MXU_DIM, tl), BF16),
            pltpu.VMEM((KV_SUBS_PER_STEP * KV_SUB, tl), F32),
            pltpu.VMEM((GROUP, HEAD_DIM, tl), F32),
            pltpu.VMEM((GROUP, 1, tl), F32),
            pltpu.VMEM((GROUP, 1, tl), F32),
        ],
        compiler_params=pltpu.CompilerParams(
            dimension_semantics=("arbitrary", "arbitrary", "arbitrary"),
            vmem_limit_bytes=VMEM_LIMIT_BYTES),
        name="attention_lagged" if lagged else "attention_exact",
    )(qT, k, vT)


def _attention_dispatch(qT, k, vT, q_gain, k_gain):
    score_bound = HEAD_DIM * Q_SCALE * jnp.max(jnp.abs(q_gain)) * jnp.max(jnp.abs(k_gain))
    return lax.cond(score_bound < LAGGED_SCORE_BOUND,
                    functools.partial(_attention, lagged=True),
                    functools.partial(_attention, lagged=False),
                    qT, k, vT)


def _merge_kernel(o_ref, sza_ref, xb_ref, xbp_ref, xbn_ref, szb_ref, sga_ref, sgb_ref, x_ref,
                  mod_ref, wpool_ref, pscale_ref, wa_ref, wb_ref, wo_ref, lng_ref, lnb_ref,
                  out_ref, ext_ref, *, d_model, n_tiles, ctx_len, seq_len, alpha_res):
    t = pl.program_id(1)
    tl = TOKEN_TILE
    h = POOL_HALO

    prev_ok = t >= 2
    next_ok = jnp.logical_and(t >= 1, t <= n_tiles - 2)
    ext_ref[0:h, :] = jnp.where(prev_ok, xbp_ref[0], 0.0)
    ext_ref[h:h + tl, :] = xb_ref[0]
    ext_ref[h + tl:2 * h + tl, :] = jnp.where(next_ok, xbn_ref[0], 0.0)

    pos = lax.broadcasted_iota(jnp.int32, (tl, 1), 0) + jnp.where(t == 0, 0, (t - 1) * tl)
    seq = jnp.where(t == 0, ctx_len, seq_len)

    parts = []
    for gi, w in enumerate(POOL_WINDOWS):
        a = w // 2
        bb = w - a
        lanes = slice(gi * POOL_GROUP, (gi + 1) * POOL_GROUP)
        acc = ext_ref[h - a:h - a + tl, lanes]
        for j in range(-a + 1, bb):
            acc = acc + ext_ref[h + j:h + j + tl, lanes]
        cnt = (jnp.minimum(pos + bb, seq) - jnp.maximum(pos - a, 0)).astype(F32)
        m = acc / cnt - ext_ref[h:h + tl, lanes]
        parts.append(jnp.dot(m.astype(BF16), wpool_ref[gi], preferred_element_type=F32))
    p_pool = jnp.concatenate(parts, axis=-1) * pscale_ref[...]

    b_in = (p_pool * szb_ref[0].astype(F32)).astype(BF16)
    yb = jnp.dot(b_in, wb_ref[...], preferred_element_type=F32)
    a_in = o_ref[0] * sza_ref[0]
    ya = jnp.dot(a_in, wa_ref[...], preferred_element_type=F32)
    y = (sga_ref[0].astype(F32) * ya + sgb_ref[0].astype(F32) * yb).astype(BF16)
    yo = jnp.dot(y, wo_ref[...], preferred_element_type=F32)

    gate = mod_ref[0, 0][:, 2 * d_model:]
    r = alpha_res * x_ref[0] + gate * yo
    mu = jnp.mean(r, axis=-1, keepdims=True)
    rc = r - mu
    var = jnp.mean(rc * rc, axis=-1, keepdims=True)
    out_ref[0] = rc * lax.rsqrt(var + EPS) * lng_ref[...] + lnb_ref[...]


def _merge(o, sza, xb, szb, sga, sgb, x_all, mod4, wpool, pscale, wa, wb, wo, lng, lnb,
           *, ctx_len, alpha_res):
    b, t_all, d = x_all.shape
    attn_w = o.shape[2]
    pool_w = xb.shape[2]
    tl = TOKEN_TILE
    nt = t_all // tl
    hb = tl // POOL_HALO
    n_hblocks = t_all // POOL_HALO
    row = lambda bi, ti: (bi, ti, 0)
    const2 = lambda bi, ti: (0, 0)
    in_specs = [
        pl.BlockSpec((1, tl, attn_w), row),
        pl.BlockSpec((1, tl, attn_w), row),
        pl.BlockSpec((1, tl, pool_w), row),
        pl.BlockSpec((1, POOL_HALO, pool_w), lambda bi, ti: (bi, jnp.maximum(ti * hb - 1, 0), 0)),
        pl.BlockSpec((1, POOL_HALO, pool_w),
                     lambda bi, ti: (bi, jnp.minimum((ti + 1) * hb, n_hblocks - 1), 0)),
        pl.BlockSpec((1, tl, pool_w), row),
        pl.BlockSpec((1, tl, d), row),
        pl.BlockSpec((1, tl, d), row),
        pl.BlockSpec((1, tl, d), row),
        pl.BlockSpec((1, 1, 1, 3 * d), lambda bi, ti: (bi, jnp.minimum(ti, 1), 0, 0)),
        pl.BlockSpec(wpool.shape, lambda bi, ti: (0, 0, 0)),
        pl.BlockSpec((1, pool_w), const2),
        pl.BlockSpec(wa.shape, const2),
        pl.BlockSpec(wb.shape, const2),
        pl.BlockSpec(wo.shape, const2),
        pl.BlockSpec((1, d), const2),
        pl.BlockSpec((1, d), const2),
    ]
    return pl.pallas_call(
        functools.partial(_merge_kernel, d_model=d, n_tiles=nt, ctx_len=ctx_len,
                          seq_len=t_all - ctx_len, alpha_res=alpha_res),
        grid=(b, nt),
        in_specs=in_specs,
        out_specs=pl.BlockSpec((1, tl, d), row),
        out_shape=jax.ShapeDtypeStruct((b, t_all, d), F32),
        scratch_shapes=[pltpu.VMEM((tl + 2 * POOL_HALO, pool_w), F32)],
        compiler_params=pltpu.CompilerParams(
            dimension_semantics=("arbitrary", "arbitrary"), vmem_limit_bytes=VMEM_LIMIT_BYTES),
        name="merge",
    )(o, sza, xb, xb, xb, szb, sga, sgb, x_all, mod4, wpool, pscale, wa, wb, wo, lng, lnb)


def _rope_tables(seq_len, ctx_len):
    n_freq = HEAD_DIM // 4
    tpos = jnp.arange(seq_len, dtype=jnp.int32)
    rowp = (tpos // GRID_W).astype(F32)
    colp = (tpos % GRID_W).astype(F32)
    inv_freq = 1.0 / (ROPE_THETA ** (jnp.arange(n_freq, dtype=F32) / n_freq))
    ar = rowp[:, None] * inv_freq[None, :]
    ac = colp[:, None] * inv_freq[None, :]
    ang = jnp.concatenate([ar, ar, ac, ac], axis=-1)
    cos = jnp.cos(ang)
    sin = jnp.sin(ang)
    lane = jnp.arange(HEAD_DIM)
    lower = ((lane // n_freq) % 2 == 0)[None, :]
    slo = jnp.where(lower, -sin, 0.0)
    shi = jnp.where(lower, 0.0, sin)
    ones = jnp.ones((ctx_len, HEAD_DIM), F32)
    zeros = jnp.zeros((ctx_len, HEAD_DIM), F32)
    return (jnp.concatenate([ones, cos], axis=0),
            jnp.concatenate([zeros, slo], axis=0),
            jnp.concatenate([zeros, shi], axis=0))


def kernel(x, c, ctx, c_ctx, w_mod, b_mod, w_in, q_norm, k_norm, w_pool, pool_scale,
           w_br_a, w_br_b, w_out, ln_g, ln_b):
    b, seq_len, d = x.shape
    ctx_len = ctx.shape[1]
    depth = w_in.shape[0]
    attn_w = w_br_a.shape[1]
    pool_w = w_br_b.shape[1]
    kv_w = (w_in.shape[2] - 2 * attn_w - 2 * pool_w - 2 * d) // 2
    assert ctx_len == TOKEN_TILE and seq_len % (TOKEN_TILE * KV_SUBS_PER_STEP) == 0
    assert attn_w // HEAD_DIM == GROUP * N_KV_HEADS and kv_w == N_KV_HEADS * HEAD_DIM
    alpha_res = (2 * depth) ** 0.25

    x_all = jnp.concatenate([ctx, x], axis=1)

    c_rows = jnp.zeros((8, d), F32).at[0].set(c_ctx).at[1:1 + b].set(c)
    mod = _modulation(c_rows, w_mod, b_mod)

    cos_t, slo_t, shi_t = _rope_tables(seq_len, ctx_len)

    w_in_b = w_in.astype(BF16)
    w_pool_b = w_pool.astype(BF16)
    w_a_b = w_br_a.astype(BF16)
    w_b_b = w_br_b.astype(BF16)
    w_o_b = w_out.astype(BF16)

    for l in range(depth):
        mod_ctx = jnp.broadcast_to(mod[l, 0][None], (b, 3 * d))
        mod4 = jnp.stack([mod_ctx, mod[l, 1:1 + b]], axis=1).reshape(b, 2, 1, 3 * d)
        qT, k, vT, sza, xb, szb, sga, sgb = _project(
            x_all, mod4, w_in_b[l], q_norm[l].reshape(1, HEAD_DIM), k_norm[l].reshape(1, HEAD_DIM),
            cos_t, slo_t, shi_t, attn_w=attn_w, kv_w=kv_w, pool_w=pool_w)
        o = _attention_dispatch(qT, k, vT, q_norm[l], k_norm[l])
        x_all = _merge(o, sza, xb, szb, sga, sgb, x_all, mod4, w_pool_b[l],
                       pool_scale[l].reshape(1, pool_w), w_a_b[l], w_b_b[l], w_o_b[l],
                       ln_g[l].reshape(1, d), ln_b[l].reshape(1, d),
                       ctx_len=ctx_len, alpha_res=alpha_res)
    return x_all[:, ctx_len:]
```

```python
import functools
import math

import jax
import jax.numpy as jnp
from jax import lax
from jax.experimental import pallas as pl
from jax.experimental.pallas import tpu as pltpu

F32 = jnp.float32
BF16 = jnp.bfloat16

HEAD_DIM = 128
MXU_DIM = 256
N_KV_HEADS = 2
GROUP = 4
GRID_W = 64
POOL_WINDOWS = (2, 4, 8, 16)
POOL_GROUP = 128
POOL_HALO = 8
ROPE_THETA = 10000.0
EPS = 1e-6

TOKEN_TILE = 256
KV_SUB = 256
KV_SUBS_PER_STEP = 8
STEP_UNROLL = 4
VMEM_LIMIT_BYTES = 56 * 1024 * 1024

Q_SCALE = HEAD_DIM ** -0.5 * math.log2(math.e)
REDUCE_CHAINS = 4
LAGGED_SCORE_BOUND = 48.0


def _sigmoid(v):
    return 1.0 / (1.0 + jnp.exp(-v))


def _mod_kernel(c_ref, w_ref, b_ref, o_ref):
    cv = c_ref[...]
    s = cv * _sigmoid(cv)
    o_ref[0] = jnp.dot(s, w_ref[0], preferred_element_type=F32) + b_ref[0]


def _modulation(c_rows, w_mod, b_mod):
    depth, d, d3 = w_mod.shape
    rows = c_rows.shape[0]
    return pl.pallas_call(
        _mod_kernel,
        grid=(depth,),
        in_specs=[
            pl.BlockSpec((rows, d), lambda l: (0, 0)),
            pl.BlockSpec((1, d, d3), lambda l: (l, 0, 0)),
            pl.BlockSpec((1, 1, d3), lambda l: (l, 0, 0)),
        ],
        out_specs=pl.BlockSpec((1, rows, d3), lambda l: (l, 0, 0)),
        out_shape=jax.ShapeDtypeStruct((depth, rows, d3), F32),
        compiler_params=pltpu.CompilerParams(
            dimension_semantics=("arbitrary",), vmem_limit_bytes=VMEM_LIMIT_BYTES),
        name="modulation",
    )(c_rows, w_mod, b_mod.reshape(depth, 1, d3))


def _proj_kernel(x_ref, mod_ref, w_ref, qg_ref, kg_ref, cos_ref, slo_ref, shi_ref,
                 qT_ref, k_ref, vT_ref, sza_ref, xb_ref, szb_ref, sga_ref, sgb_ref,
                 *, d_model, attn_w, kv_w, pool_w):
    x = x_ref[0]
    mod = mod_ref[0, 0]
    shift = mod[:, :d_model]
    scale = mod[:, d_model:2 * d_model]
    u = (x * (1.0 + scale) + shift).astype(BF16)

    cos = cos_ref[...]
    slo = slo_ref[...]
    shi = shi_ref[...]

    def norm_rope(h, g):
        ms = jnp.mean(h * h, axis=-1, keepdims=True)
        y = h * lax.rsqrt(ms + EPS) * g
        return y * cos + pltpu.roll(y, HEAD_DIM - 32, 1) * slo + pltpu.roll(y, 32, 1) * shi

    o = 0
    hq = jnp.dot(u, w_ref[:, o:o + attn_w], preferred_element_type=F32)
    o += attn_w
    qg = qg_ref[...]
    for h in range(attn_w // HEAD_DIM):
        qh = norm_rope(hq[:, h * HEAD_DIM:(h + 1) * HEAD_DIM], qg) * Q_SCALE
        qT_ref[0, h * HEAD_DIM:(h + 1) * HEAD_DIM, :] = qh.T.astype(BF16)

    hkv = jnp.dot(u, w_ref[:, o:o + 2 * kv_w], preferred_element_type=F32)
    o += 2 * kv_w
    kg = kg_ref[...]
    for g in range(kv_w // HEAD_DIM):
        kh = norm_rope(hkv[:, g * HEAD_DIM:(g + 1) * HEAD_DIM], kg)
        k_ref[0, :, g * MXU_DIM:g * MXU_DIM + HEAD_DIM] = kh.astype(BF16)
        lane = lax.broadcasted_iota(jnp.int32, (x.shape[0], MXU_DIM - HEAD_DIM), 1)
        k_ref[0, :, g * MXU_DIM + HEAD_DIM:(g + 1) * MXU_DIM] = (lane == 0).astype(BF16)
        vh = hkv[:, kv_w + g * HEAD_DIM:kv_w + (g + 1) * HEAD_DIM]
        vT_ref[0, g, 0] = vh.T.astype(BF16)

    za = jnp.dot(u, w_ref[:, o:o + attn_w], preferred_element_type=F32)
    o += attn_w
    sza_ref[0] = (za * _sigmoid(za)).astype(BF16)

    xb_ref[0] = jnp.dot(u, w_ref[:, o:o + pool_w], preferred_element_type=F32)
    o += pool_w

    zb = jnp.dot(u, w_ref[:, o:o + pool_w], preferred_element_type=F32)
    o += pool_w
    szb_ref[0] = (zb * _sigmoid(zb)).astype(BF16)

    ga = jnp.dot(u, w_ref[:, o:o + d_model], preferred_element_type=F32)
    o += d_model
    sga_ref[0] = _sigmoid(ga).astype(BF16)

    gb = jnp.dot(u, w_ref[:, o:o + d_model], preferred_element_type=F32)
    sgb_ref[0] = _sigmoid(gb).astype(BF16)


def _project(x_all, mod4, w_in, qg, kg, cos_t, slo_t, shi_t, *, attn_w, kv_w, pool_w):
    b, t_all, d = x_all.shape
    n_in = w_in.shape[1]
    tl = TOKEN_TILE
    nt = t_all // tl
    n_kv = kv_w // HEAD_DIM
    row = lambda bi, ti: (bi, ti, 0)
    const2 = lambda bi, ti: (0, 0)
    out_shape = (
        jax.ShapeDtypeStruct((b, attn_w, t_all), BF16),
        jax.ShapeDtypeStruct((b, t_all, n_kv * MXU_DIM), BF16),
        jax.ShapeDtypeStruct((b, n_kv, nt, HEAD_DIM, tl), BF16),
        jax.ShapeDtypeStruct((b, t_all, attn_w), BF16),
        jax.ShapeDtypeStruct((b, t_all, pool_w), F32),
        jax.ShapeDtypeStruct((b, t_all, pool_w), BF16),
        jax.ShapeDtypeStruct((b, t_all, d), BF16),
        jax.ShapeDtypeStruct((b, t_all, d), BF16),
    )
    out_specs = (
        pl.BlockSpec((1, attn_w, tl), lambda bi, ti: (bi, 0, ti)),
        pl.BlockSpec((1, tl, n_kv * MXU_DIM), row),
        pl.BlockSpec((1, n_kv, 1, HEAD_DIM, tl), lambda bi, ti: (bi, 0, ti, 0, 0)),
        pl.BlockSpec((1, tl, attn_w), row),
        pl.BlockSpec((1, tl, pool_w), row),
        pl.BlockSpec((1, tl, pool_w), row),
        pl.BlockSpec((1, tl, d), row),
        pl.BlockSpec((1, tl, d), row),
    )
    in_specs = [
        pl.BlockSpec((1, tl, d), row),
        pl.BlockSpec((1, 1, 1, 3 * d), lambda bi, ti: (bi, jnp.minimum(ti, 1), 0, 0)),
        pl.BlockSpec((d, n_in), const2),
        pl.BlockSpec((1, HEAD_DIM), const2),
        pl.BlockSpec((1, HEAD_DIM), const2),
        pl.BlockSpec((tl, HEAD_DIM), lambda bi, ti: (ti, 0)),
        pl.BlockSpec((tl, HEAD_DIM), lambda bi, ti: (ti, 0)),
        pl.BlockSpec((tl, HEAD_DIM), lambda bi, ti: (ti, 0)),
    ]
    return pl.pallas_call(
        functools.partial(_proj_kernel, d_model=d, attn_w=attn_w, kv_w=kv_w, pool_w=pool_w),
        grid=(b, nt),
        in_specs=in_specs,
        out_specs=out_specs,
        out_shape=out_shape,
        compiler_params=pltpu.CompilerParams(
            dimension_semantics=("arbitrary", "arbitrary"), vmem_limit_bytes=VMEM_LIMIT_BYTES),
        name="projection",
    )(x_all, mod4, w_in, qg, kg, cos_t, slo_t, shi_t)


def _col_reduce(x, op):
    nblk = x.shape[0] // 8
    accs = [x[8 * j:8 * j + 8] for j in range(min(REDUCE_CHAINS, nblk))]
    for i in range(len(accs), nblk):
        accs[i % REDUCE_CHAINS] = op(accs[i % REDUCE_CHAINS], x[8 * i:8 * i + 8])
    while len(accs) > 1:
        accs = [op(accs[2 * i], accs[2 * i + 1]) for i in range(len(accs) // 2)]
    return accs[0]


def _attn_kernel(qT_ref, k_ref, vT_ref, o_ref, w_ref, s_ref, acc_ref, m_ref, l_ref,
                 *, n_latent_steps, lagged):
    t = pl.program_id(2)
    tl = o_ref.shape[1]
    ref_rows = 16

    for r in range(GROUP):
        w_ref[r, 0:HEAD_DIM, :] = qT_ref[0, r * HEAD_DIM:(r + 1) * HEAD_DIM, :]
        w_ref[r, HEAD_DIM:, :] = jnp.zeros((MXU_DIM - HEAD_DIM, tl), BF16)

    def set_reference(r, m):
        row = lax.broadcasted_iota(jnp.int32, (ref_rows, tl), 0)
        w_ref[r, HEAD_DIM:HEAD_DIM + ref_rows, :] = jnp.where(row == 0, -m, 0.0).astype(BF16)

    def keys(j0, nj):
        start = pl.multiple_of(j0 * KV_SUB, KV_SUB)
        return k_ref[0, pl.ds(start, nj * KV_SUB), :]

    def scores(kc, r):
        return jnp.dot(kc, w_ref[r], preferred_element_type=F32)

    def head_update(r, s, j0, nj, first):
        cmax = jnp.max(_col_reduce(s, jnp.maximum), axis=0, keepdims=True)
        if first:
            m_old = m_new = cmax.astype(BF16).astype(F32) if lagged else cmax
            p = jnp.exp2(s - m_new)
        elif lagged:
            m_old = m_ref[r]
            m_new = (m_old + jnp.maximum(cmax, 0.0)).astype(BF16).astype(F32)
            p = jnp.exp2(s)
        else:
            m_old = m_ref[r]
            m_new = jnp.maximum(m_old, cmax)
            p = jnp.exp2(s - m_new)
        lsum = jnp.sum(_col_reduce(p, jnp.add), axis=0, keepdims=True)
        pb = p.astype(BF16)
        pv = jnp.dot(vT_ref[0, 0, j0], pb[0:KV_SUB], preferred_element_type=F32)
        for jj in range(1, nj):
            pv += jnp.dot(vT_ref[0, 0, j0 + jj], pb[jj * KV_SUB:(jj + 1) * KV_SUB],
                          preferred_element_type=F32)
        if first:
            l_ref[r] = lsum
            acc_ref[r] = pv
        else:
            alpha = jnp.exp2(m_old - m_new)
            if lagged:
                l_ref[r] = alpha * (l_ref[r] + lsum)
                acc_ref[r] = alpha * (acc_ref[r] + pv)
            else:
                l_ref[r] = alpha * l_ref[r] + lsum
                acc_ref[r] = alpha * acc_ref[r] + pv
        m_ref[r] = m_new
        if lagged:
            set_reference(r, m_new)

    nj = KV_SUBS_PER_STEP
    unroll = STEP_UNROLL if lagged and n_latent_steps % STEP_UNROLL == 0 else 1

    kc = keys(0, 1)
    s = scores(kc, 0)
    for r in range(GROUP):
        s_next = scores(kc, r + 1) if r + 1 < GROUP else scores(keys(1, nj), 0)
        head_update(r, s, 0, 1, True)
        s = s_next
    s_ref[...] = s

    def body(i, carry):
        s = s_ref[...]
        for u in range(unroll):
            j0 = 1 + (i * unroll + u) * nj
            j0_next = jnp.minimum(j0 + nj, 1 + (n_latent_steps - 1) * nj)
            kc = keys(j0, nj)
            for r in range(GROUP):
                s_next = scores(kc, r + 1) if r + 1 < GROUP else scores(keys(j0_next, nj), 0)
                head_update(r, s, j0, nj, False)
                s = s_next
        s_ref[...] = s
        return carry

    @pl.when(t > 0)
    def _latent_keys():
        lax.fori_loop(0, n_latent_steps // unroll, body, 0)

    for r in range(GROUP):
        o = acc_ref[r] * (1.0 / l_ref[r])
        o_ref[0, :, r * HEAD_DIM:(r + 1) * HEAD_DIM] = o.T.astype(BF16)


def _attention(qT, k, vT, *, lagged):
    b, attn_w, t_all = qT.shape
    n_kv, nt = vT.shape[1], vT.shape[2]
    tl = TOKEN_TILE
    gw = GROUP * HEAD_DIM
    n_latent_steps = (nt - 1) // KV_SUBS_PER_STEP
    return pl.pallas_call(
        functools.partial(_attn_kernel, n_latent_steps=n_latent_steps, lagged=lagged),
        grid=(b, n_kv, nt),
        in_specs=[
            pl.BlockSpec((1, gw, tl), lambda bi, g, ti: (bi, g, ti)),
            pl.BlockSpec((1, t_all, MXU_DIM), lambda bi, g, ti: (bi, 0, g)),
            pl.BlockSpec((1, 1, nt, HEAD_DIM, KV_SUB), lambda bi, g, ti: (bi, g, 0, 0, 0)),
        ],
        out_specs=pl.BlockSpec((1, tl, gw), lambda bi, g, ti: (bi, ti, g)),
        out_shape=jax.ShapeDtypeStruct((b, t_all, attn_w), BF16),
        scratch_shapes=[
            pltpu.VMEM((GROUP, MXU_DIM, tl), BF16),
            pltpu.VMEM((KV_SUBS_PER_STEP * KV_SUB, tl), F32),
            pltpu.VMEM((GROUP, HEAD_DIM, tl), F32),
            pltpu.VMEM((GROUP, 1, tl), F32),
            pltpu.VMEM((GROUP, 1, tl), F32),
        ],
        compiler_params=pltpu.CompilerParams(
            dimension_semantics=("arbitrary", "arbitrary", "arbitrary"),
            vmem_limit_bytes=VMEM_LIMIT_BYTES),
        name="attention_lagged" if lagged else "attention_exact",
    )(qT, k, vT)


def _attention_dispatch(qT, k, vT, q_gain, k_gain):
    score_bound = HEAD_DIM * Q_SCALE * jnp.max(jnp.abs(q_gain)) * jnp.max(jnp.abs(k_gain))
    return lax.cond(score_bound < LAGGED_SCORE_BOUND,
                    functools.partial(_attention, lagged=True),
                    functools.partial(_attention, lagged=False),
                    qT, k, vT)


def _merge_kernel(o_ref, sza_ref, xb_ref, xbp_ref, xbn_ref, szb_ref, sga_ref, sgb_ref, x_ref,
                  mod_ref, wpool_ref, pscale_ref, wa_ref, wb_ref, wo_ref, lng_ref, lnb_ref,
                  out_ref, ext_ref, *, d_model, n_tiles, ctx_len, seq_len, alpha_res):
    t = pl.program_id(1)
    tl = TOKEN_TILE
    h = POOL_HALO

    prev_ok = t >= 2
    next_ok = jnp.logical_and(t >= 1, t <= n_tiles - 2)
    ext_ref[0:h, :] = jnp.where(prev_ok, xbp_ref[0], 0.0)
    ext_ref[h:h + tl, :] = xb_ref[0]
    ext_ref[h + tl:2 * h + tl, :] = jnp.where(next_ok, xbn_ref[0], 0.0)

    pos = lax.broadcasted_iota(jnp.int32, (tl, 1), 0) + jnp.where(t == 0, 0, (t - 1) * tl)
    seq = jnp.where(t == 0, ctx_len, seq_len)

    parts = []
    for gi, w in enumerate(POOL_WINDOWS):
        a = w // 2
        bb = w - a
        lanes = slice(gi * POOL_GROUP, (gi + 1) * POOL_GROUP)
        acc = ext_ref[h - a:h - a + tl, lanes]
        for j in range(-a + 1, bb):
            acc = acc + ext_ref[h + j:h + j + tl, lanes]
        cnt = (jnp.minimum(pos + bb, seq) - jnp.maximum(pos - a, 0)).astype(F32)
        m = acc / cnt - ext_ref[h:h + tl, lanes]
        parts.append(jnp.dot(m.astype(BF16), wpool_ref[gi], preferred_element_type=F32))
    p_pool = jnp.concatenate(parts, axis=-1) * pscale_ref[...]

    b_in = (p_pool * szb_ref[0].astype(F32)).astype(BF16)
    yb = jnp.dot(b_in, wb_ref[...], preferred_element_type=F32)
    a_in = o_ref[0] * sza_ref[0]
    ya = jnp.dot(a_in, wa_ref[...], preferred_element_type=F32)
    y = (sga_ref[0].astype(F32) * ya + sgb_ref[0].astype(F32) * yb).astype(BF16)
    yo = jnp.dot(y, wo_ref[...], preferred_element_type=F32)

    gate = mod_ref[0, 0][:, 2 * d_model:]
    r = alpha_res * x_ref[0] + gate * yo
    mu = jnp.mean(r, axis=-1, keepdims=True)
    rc = r - mu
    var = jnp.mean(rc * rc, axis=-1, keepdims=True)
    out_ref[0] = rc * lax.rsqrt(var + EPS) * lng_ref[...] + lnb_ref[...]


def _merge(o, sza, xb, szb, sga, sgb, x_all, mod4, wpool, pscale, wa, wb, wo, lng, lnb,
           *, ctx_len, alpha_res):
    b, t_all, d = x_all.shape
    attn_w = o.shape[2]
    pool_w = xb.shape[2]
    tl = TOKEN_TILE
    nt = t_all // tl
    hb = tl // POOL_HALO
    n_hblocks = t_all // POOL_HALO
    row = lambda bi, ti: (bi, ti, 0)
    const2 = lambda bi, ti: (0, 0)
    in_specs = [
        pl.BlockSpec((1, tl, attn_w), row),
        pl.BlockSpec((1, tl, attn_w), row),
        pl.BlockSpec((1, tl, pool_w), row),
        pl.BlockSpec((1, POOL_HALO, pool_w), lambda bi, ti: (bi, jnp.maximum(ti * hb - 1, 0), 0)),
        pl.BlockSpec((1, POOL_HALO, pool_w),
                     lambda bi, ti: (bi, jnp.minimum((ti + 1) * hb, n_hblocks - 1), 0)),
        pl.BlockSpec((1, tl, pool_w), row),
        pl.BlockSpec((1, tl, d), row),
        pl.BlockSpec((1, tl, d), row),
        pl.BlockSpec((1, tl, d), row),
        pl.BlockSpec((1, 1, 1, 3 * d), lambda bi, ti: (bi, jnp.minimum(ti, 1), 0, 0)),
        pl.BlockSpec(wpool.shape, lambda bi, ti: (0, 0, 0)),
        pl.BlockSpec((1, pool_w), const2),
        pl.BlockSpec(wa.shape, const2),
        pl.BlockSpec(wb.shape, const2),
        pl.BlockSpec(wo.shape, const2),
        pl.BlockSpec((1, d), const2),
        pl.BlockSpec((1, d), const2),
    ]
    return pl.pallas_call(
        functools.partial(_merge_kernel, d_model=d, n_tiles=nt, ctx_len=ctx_len,
                          seq_len=t_all - ctx_len, alpha_res=alpha_res),
        grid=(b, nt),
        in_specs=in_specs,
        out_specs=pl.BlockSpec((1, tl, d), row),
        out_shape=jax.ShapeDtypeStruct((b, t_all, d), F32),
        scratch_shapes=[pltpu.VMEM((tl + 2 * POOL_HALO, pool_w), F32)],
        compiler_params=pltpu.CompilerParams(
            dimension_semantics=("arbitrary", "arbitrary"), vmem_limit_bytes=VMEM_LIMIT_BYTES),
        name="merge",
    )(o, sza, xb, xb, xb, szb, sga, sgb, x_all, mod4, wpool, pscale, wa, wb, wo, lng, lnb)


def _rope_tables(seq_len, ctx_len):
    n_freq = HEAD_DIM // 4
    tpos = jnp.arange(seq_len, dtype=jnp.int32)
    rowp = (tpos // GRID_W).astype(F32)
    colp = (tpos % GRID_W).astype(F32)
    inv_freq = 1.0 / (ROPE_THETA ** (jnp.arange(n_freq, dtype=F32) / n_freq))
    ar = rowp[:, None] * inv_freq[None, :]
    ac = colp[:, None] * inv_freq[None, :]
    ang = jnp.concatenate([ar, ar, ac, ac], axis=-1)
    cos = jnp.cos(ang)
    sin = jnp.sin(ang)
    lane = jnp.arange(HEAD_DIM)
    lower = ((lane // n_freq) % 2 == 0)[None, :]
    slo = jnp.where(lower, -sin, 0.0)
    shi = jnp.where(lower, 0.0, sin)
    ones = jnp.ones((ctx_len, HEAD_DIM), F32)
    zeros = jnp.zeros((ctx_len, HEAD_DIM), F32)
    return (jnp.concatenate([ones, cos], axis=0),
            jnp.concatenate([zeros, slo], axis=0),
            jnp.concatenate([zeros, shi], axis=0))


def kernel(x, c, ctx, c_ctx, w_mod, b_mod, w_in, q_norm, k_norm, w_pool, pool_scale,
           w_br_a, w_br_b, w_out, ln_g, ln_b):
    b, seq_len, d = x.shape
    ctx_len = ctx.shape[1]
    depth = w_in.shape[0]
    attn_w = w_br_a.shape[1]
    pool_w = w_br_b.shape[1]
    kv_w = (w_in.shape[2] - 2 * attn_w - 2 * pool_w - 2 * d) // 2
    assert ctx_len == TOKEN_TILE and seq_len % (TOKEN_TILE * KV_SUBS_PER_STEP) == 0
    assert attn_w // HEAD_DIM == GROUP * N_KV_HEADS and kv_w == N_KV_HEADS * HEAD_DIM
    alpha_res = (2 * depth) ** 0.25

    x_all = jnp.concatenate([ctx, x], axis=1)

    c_rows = jnp.zeros((8, d), F32).at[0].set(c_ctx).at[1:1 + b].set(c)
    mod = _modulation(c_rows, w_mod, b_mod)

    cos_t, slo_t, shi_t = _rope_tables(seq_len, ctx_len)

    w_in_b = w_in.astype(BF16)
    w_pool_b = w_pool.astype(BF16)
    w_a_b = w_br_a.astype(BF16)
    w_b_b = w_br_b.astype(BF16)
    w_o_b = w_out.astype(BF16)

    for l in range(depth):
        mod_ctx = jnp.broadcast_to(mod[l, 0][None], (b, 3 * d))
        mod4 = jnp.stack([mod_ctx, mod[l, 1:1 + b]], axis=1).reshape(b, 2, 1, 3 * d)
        qT, k, vT, sza, xb, szb, sga, sgb = _project(
            x_all, mod4, w_in_b[l], q_norm[l].reshape(1, HEAD_DIM), k_norm[l].reshape(1, HEAD_DIM),
            cos_t, slo_t, shi_t, attn_w=attn_w, kv_w=kv_w, pool_w=pool_w)
        o = _attention_dispatch(qT, k, vT, q_norm[l], k_norm[l])
        x_all = _merge(o, sza, xb, szb, sga, sgb, x_all, mod4, w_pool_b[l],
                       pool_scale[l].reshape(1, pool_w), w_a_b[l], w_b_b[l], w_o_b[l],
                       ln_g[l].reshape(1, d), ln_b[l].reshape(1, d),
                       ctx_len=ctx_len, alpha_res=alpha_res)
    return x_all[:, ctx_len:]
```

```python
import functools
import math

import jax
import jax.numpy as jnp
from jax import lax
from jax.experimental import pallas as pl
from jax.experimental.pallas import tpu as pltpu

F32 = jnp.float32
BF16 = jnp.bfloat16

HEAD_DIM = 128
N_KV_HEADS = 2
GROUP = 4
GRID_W = 64
POOL_WINDOWS = (2, 4, 8, 16)
POOL_GROUP = 128
POOL_HALO = 8
ROPE_THETA = 10000.0
EPS = 1e-6

TOKEN_TILE = 256
KV_SUB = 256
KV_SUBS_PER_STEP = 8
STEP_UNROLL = 4
VMEM_LIMIT_BYTES = 56 * 1024 * 1024

Q_SCALE = HEAD_DIM ** -0.5 * math.log2(math.e)
REDUCE_CHAINS = 4
BOUNDED_SCORE_LIMIT = 48.0


def _sigmoid(v):
    return 1.0 / (1.0 + jnp.exp(-v))


def _mod_kernel(c_ref, w_ref, b_ref, o_ref):
    cv = c_ref[...]
    s = cv * _sigmoid(cv)
    o_ref[0] = jnp.dot(s, w_ref[0], preferred_element_type=F32) + b_ref[0]


def _modulation(c_rows, w_mod, b_mod):
    depth, d, d3 = w_mod.shape
    rows = c_rows.shape[0]
    return pl.pallas_call(
        _mod_kernel,
        grid=(depth,),
        in_specs=[
            pl.BlockSpec((rows, d), lambda l: (0, 0)),
            pl.BlockSpec((1, d, d3), lambda l: (l, 0, 0)),
            pl.BlockSpec((1, 1, d3), lambda l: (l, 0, 0)),
        ],
        out_specs=pl.BlockSpec((1, rows, d3), lambda l: (l, 0, 0)),
        out_shape=jax.ShapeDtypeStruct((depth, rows, d3), F32),
        compiler_params=pltpu.CompilerParams(
            dimension_semantics=("arbitrary",), vmem_limit_bytes=VMEM_LIMIT_BYTES),
        name="modulation",
    )(c_rows, w_mod, b_mod.reshape(depth, 1, d3))


def _proj_kernel(x_ref, mod_ref, w_ref, qg_ref, kg_ref, cos_ref, slo_ref, shi_ref,
                 qT_ref, k_ref, vT_ref, sza_ref, xb_ref, szb_ref, sga_ref, sgb_ref,
                 *, d_model, attn_w, kv_w, pool_w):
    x = x_ref[0]
    mod = mod_ref[0, 0]
    shift = mod[:, :d_model]
    scale = mod[:, d_model:2 * d_model]
    u = (x * (1.0 + scale) + shift).astype(BF16)

    cos = cos_ref[...]
    slo = slo_ref[...]
    shi = shi_ref[...]

    def norm_rope(h, g):
        ms = jnp.mean(h * h, axis=-1, keepdims=True)
        y = h * lax.rsqrt(ms + EPS) * g
        return y * cos + pltpu.roll(y, HEAD_DIM - 32, 1) * slo + pltpu.roll(y, 32, 1) * shi

    o = 0
    hq = jnp.dot(u, w_ref[:, o:o + attn_w], preferred_element_type=F32)
    o += attn_w
    qg = qg_ref[...]
    for h in range(attn_w // HEAD_DIM):
        qh = norm_rope(hq[:, h * HEAD_DIM:(h + 1) * HEAD_DIM], qg) * Q_SCALE
        qT_ref[0, h * HEAD_DIM:(h + 1) * HEAD_DIM, :] = qh.T.astype(BF16)

    hkv = jnp.dot(u, w_ref[:, o:o + 2 * kv_w], preferred_element_type=F32)
    o += 2 * kv_w
    kg = kg_ref[...]
    for g in range(kv_w // HEAD_DIM):
        kh = norm_rope(hkv[:, g * HEAD_DIM:(g + 1) * HEAD_DIM], kg)
        k_ref[0, :, g * HEAD_DIM:(g + 1) * HEAD_DIM] = kh.astype(BF16)
        vh = hkv[:, kv_w + g * HEAD_DIM:kv_w + (g + 1) * HEAD_DIM]
        vT_ref[0, g, 0] = vh.T.astype(BF16)

    za = jnp.dot(u, w_ref[:, o:o + attn_w], preferred_element_type=F32)
    o += attn_w
    sza_ref[0] = (za * _sigmoid(za)).astype(BF16)

    xb_ref[0] = jnp.dot(u, w_ref[:, o:o + pool_w], preferred_element_type=F32)
    o += pool_w

    zb = jnp.dot(u, w_ref[:, o:o + pool_w], preferred_element_type=F32)
    o += pool_w
    szb_ref[0] = (zb * _sigmoid(zb)).astype(BF16)

    ga = jnp.dot(u, w_ref[:, o:o + d_model], preferred_element_type=F32)
    o += d_model
    sga_ref[0] = _sigmoid(ga).astype(BF16)

    gb = jnp.dot(u, w_ref[:, o:o + d_model], preferred_element_type=F32)
    sgb_ref[0] = _sigmoid(gb).astype(BF16)


def _project(x_all, mod4, w_in, qg, kg, cos_t, slo_t, shi_t, *, attn_w, kv_w, pool_w):
    b, t_all, d = x_all.shape
    n_in = w_in.shape[1]
    tl = TOKEN_TILE
    nt = t_all // tl
    n_kv = kv_w // HEAD_DIM
    row = lambda bi, ti: (bi, ti, 0)
    const2 = lambda bi, ti: (0, 0)
    out_shape = (
        jax.ShapeDtypeStruct((b, attn_w, t_all), BF16),
        jax.ShapeDtypeStruct((b, t_all, kv_w), BF16),
        jax.ShapeDtypeStruct((b, n_kv, nt, HEAD_DIM, tl), BF16),
        jax.ShapeDtypeStruct((b, t_all, attn_w), BF16),
        jax.ShapeDtypeStruct((b, t_all, pool_w), F32),
        jax.ShapeDtypeStruct((b, t_all, pool_w), BF16),
        jax.ShapeDtypeStruct((b, t_all, d), BF16),
        jax.ShapeDtypeStruct((b, t_all, d), BF16),
    )
    out_specs = (
        pl.BlockSpec((1, attn_w, tl), lambda bi, ti: (bi, 0, ti)),
        pl.BlockSpec((1, tl, kv_w), row),
        pl.BlockSpec((1, n_kv, 1, HEAD_DIM, tl), lambda bi, ti: (bi, 0, ti, 0, 0)),
        pl.BlockSpec((1, tl, attn_w), row),
        pl.BlockSpec((1, tl, pool_w), row),
        pl.BlockSpec((1, tl, pool_w), row),
        pl.BlockSpec((1, tl, d), row),
        pl.BlockSpec((1, tl, d), row),
    )
    in_specs = [
        pl.BlockSpec((1, tl, d), row),
        pl.BlockSpec((1, 1, 1, 3 * d), lambda bi, ti: (bi, jnp.minimum(ti, 1), 0, 0)),
        pl.BlockSpec((d, n_in), const2),
        pl.BlockSpec((1, HEAD_DIM), const2),
        pl.BlockSpec((1, HEAD_DIM), const2),
        pl.BlockSpec((tl, HEAD_DIM), lambda bi, ti: (ti, 0)),
        pl.BlockSpec((tl, HEAD_DIM), lambda bi, ti: (ti, 0)),
        pl.BlockSpec((tl, HEAD_DIM), lambda bi, ti: (ti, 0)),
    ]
    return pl.pallas_call(
        functools.partial(_proj_kernel, d_model=d, attn_w=attn_w, kv_w=kv_w, pool_w=pool_w),
        grid=(b, nt),
        in_specs=in_specs,
        out_specs=out_specs,
        out_shape=out_shape,
        compiler_params=pltpu.CompilerParams(
            dimension_semantics=("arbitrary", "arbitrary"), vmem_limit_bytes=VMEM_LIMIT_BYTES),
        name="projection",
    )(x_all, mod4, w_in, qg, kg, cos_t, slo_t, shi_t)


def _col_reduce(x, op):
    nblk = x.shape[0] // 8
    accs = [x[8 * j:8 * j + 8] for j in range(min(REDUCE_CHAINS, nblk))]
    for i in range(len(accs), nblk):
        accs[i % REDUCE_CHAINS] = op(accs[i % REDUCE_CHAINS], x[8 * i:8 * i + 8])
    while len(accs) > 1:
        accs = [op(accs[2 * i], accs[2 * i + 1]) for i in range(len(accs) // 2)]
    return accs[0]


def _attn_kernel(qT_ref, k_ref, vT_ref, o_ref, s_ref, acc_ref, m_ref, l_ref,
                 *, n_latent_steps, bounded):
    t = pl.program_id(2)

    def keys(j0, nj):
        start = pl.multiple_of(j0 * KV_SUB, KV_SUB)
        return k_ref[0, pl.ds(start, nj * KV_SUB), :]

    def scores(kc, r):
        qT = qT_ref[0, r * HEAD_DIM:(r + 1) * HEAD_DIM, :]
        return jnp.dot(kc, qT, preferred_element_type=F32)

    def head_update(r, s, j0, nj, first):
        if bounded:
            p = jnp.exp2(s)
        else:
            cmax = jnp.max(_col_reduce(s, jnp.maximum), axis=0, keepdims=True)
            m_new = cmax if first else jnp.maximum(m_ref[r], cmax)
            p = jnp.exp2(s - m_new)
        lsum = jnp.sum(_col_reduce(p, jnp.add), axis=0, keepdims=True)
        pb = p.astype(BF16)
        pv = jnp.dot(vT_ref[0, 0, j0], pb[0:KV_SUB], preferred_element_type=F32)
        for jj in range(1, nj):
            pv += jnp.dot(vT_ref[0, 0, j0 + jj], pb[jj * KV_SUB:(jj + 1) * KV_SUB],
                          preferred_element_type=F32)
        if first:
            l_ref[r] = lsum
            acc_ref[r] = pv
        elif bounded:
            l_ref[r] += lsum
            acc_ref[r] += pv
        else:
            alpha = jnp.exp2(m_ref[r] - m_new)
            l_ref[r] = alpha * l_ref[r] + lsum
            acc_ref[r] = alpha * acc_ref[r] + pv
        if not bounded:
            m_ref[r] = m_new

    nj = KV_SUBS_PER_STEP
    half = nj // 2
    unroll = STEP_UNROLL if bounded and n_latent_steps % STEP_UNROLL == 0 else 1

    kc = keys(0, 1)
    s = scores(kc, 0)
    for r in range(GROUP):
        if r + 1 < GROUP:
            s_next = scores(kc, r + 1)
        else:
            s_ref[0:half * KV_SUB, :] = scores(keys(1, half), 0)
            s_ref[half * KV_SUB:, :] = scores(keys(1 + half, nj - half), 0)
        head_update(r, s, 0, 1, True)
        s = s_next

    def body(i, carry):
        s = s_ref[...]
        for u in range(unroll):
            j0 = 1 + (i * unroll + u) * nj
            j0_next = jnp.minimum(j0 + nj, 1 + (n_latent_steps - 1) * nj)
            kc = keys(j0, nj)
            for r in range(GROUP):
                s_next = scores(kc, r + 1) if r + 1 < GROUP else scores(keys(j0_next, nj), 0)
                head_update(r, s, j0, nj, False)
                s = s_next
        s_ref[...] = s
        return carry

    @pl.when(t > 0)
    def _latent_keys():
        lax.fori_loop(0, n_latent_steps // unroll, body, 0)

    for r in range(GROUP):
        o = acc_ref[r] * (1.0 / l_ref[r])
        o_ref[0, :, r * HEAD_DIM:(r + 1) * HEAD_DIM] = o.T.astype(BF16)


def _attention(qT, k, vT, *, bounded):
    b, attn_w, t_all = qT.shape
    n_kv, nt = vT.shape[1], vT.shape[2]
    tl = TOKEN_TILE
    gw = GROUP * HEAD_DIM
    n_latent_steps = (nt - 1) // KV_SUBS_PER_STEP
    return pl.pallas_call(
        functools.partial(_attn_kernel, n_latent_steps=n_latent_steps, bounded=bounded),
        grid=(b, n_kv, nt),
        in_specs=[
            pl.BlockSpec((1, gw, tl), lambda bi, g, ti: (bi, g, ti)),
            pl.BlockSpec((1, t_all, HEAD_DIM), lambda bi, g, ti: (bi, 0, g)),
            pl.BlockSpec((1, 1, nt, HEAD_DIM, KV_SUB), lambda bi, g, ti: (bi, g, 0, 0, 0)),
        ],
        out_specs=pl.BlockSpec((1, tl, gw), lambda bi, g, ti: (bi, ti, g)),
        out_shape=jax.ShapeDtypeStruct((b, t_all, attn_w), BF16),
        scratch_shapes=[
            pltpu.VMEM((KV_SUBS_PER_STEP * KV_SUB, tl), F32),
            pltpu.VMEM((GROUP, HEAD_DIM, tl), F32),
            pltpu.VMEM((GROUP, 1, tl), F32),
            pltpu.VMEM((GROUP, 1, tl), F32),
        ],
        compiler_params=pltpu.CompilerParams(
            dimension_semantics=("arbitrary", "arbitrary", "arbitrary"),
            vmem_limit_bytes=VMEM_LIMIT_BYTES),
        name="attention_bounded" if bounded else "attention_online",
    )(qT, k, vT)


def _attention_dispatch(qT, k, vT, q_gain, k_gain):
    score_bound = HEAD_DIM * Q_SCALE * jnp.max(jnp.abs(q_gain)) * jnp.max(jnp.abs(k_gain))
    return lax.cond(score_bound < BOUNDED_SCORE_LIMIT,
                    functools.partial(_attention, bounded=True),
                    functools.partial(_attention, bounded=False),
                    qT, k, vT)


def _merge_kernel(o_ref, sza_ref, xb_ref, xbp_ref, xbn_ref, szb_ref, sga_ref, sgb_ref, x_ref,
                  mod_ref, wpool_ref, pscale_ref, wa_ref, wb_ref, wo_ref, lng_ref, lnb_ref,
                  out_ref, ext_ref, *, d_model, n_tiles, ctx_len, seq_len, alpha_res):
    t = pl.program_id(1)
    tl = TOKEN_TILE
    h = POOL_HALO

    prev_ok = t >= 2
    next_ok = jnp.logical_and(t >= 1, t <= n_tiles - 2)
    ext_ref[0:h, :] = jnp.where(prev_ok, xbp_ref[0], 0.0)
    ext_ref[h:h + tl, :] = xb_ref[0]
    ext_ref[h + tl:2 * h + tl, :] = jnp.where(next_ok, xbn_ref[0], 0.0)

    pos = lax.broadcasted_iota(jnp.int32, (tl, 1), 0) + jnp.where(t == 0, 0, (t - 1) * tl)
    seq = jnp.where(t == 0, ctx_len, seq_len)

    parts = []
    for gi, w in enumerate(POOL_WINDOWS):
        a = w // 2
        bb = w - a
        lanes = slice(gi * POOL_GROUP, (gi + 1) * POOL_GROUP)
        acc = ext_ref[h - a:h - a + tl, lanes]
        for j in range(-a + 1, bb):
            acc = acc + ext_ref[h + j:h + j + tl, lanes]
        cnt = (jnp.minimum(pos + bb, seq) - jnp.maximum(pos - a, 0)).astype(F32)
        m = acc / cnt - ext_ref[h:h + tl, lanes]
        parts.append(jnp.dot(m.astype(BF16), wpool_ref[gi], preferred_element_type=F32))
    p_pool = jnp.concatenate(parts, axis=-1) * pscale_ref[...]

    b_in = (p_pool * szb_ref[0].astype(F32)).astype(BF16)
    yb = jnp.dot(b_in, wb_ref[...], preferred_element_type=F32)
    a_in = o_ref[0] * sza_ref[0]
    ya = jnp.dot(a_in, wa_ref[...], preferred_element_type=F32)
    y = (sga_ref[0].astype(F32) * ya + sgb_ref[0].astype(F32) * yb).astype(BF16)
    yo = jnp.dot(y, wo_ref[...], preferred_element_type=F32)

    gate = mod_ref[0, 0][:, 2 * d_model:]
    r = alpha_res * x_ref[0] + gate * yo
    mu = jnp.mean(r, axis=-1, keepdims=True)
    rc = r - mu
    var = jnp.mean(rc * rc, axis=-1, keepdims=True)
    out_ref[0] = rc * lax.rsqrt(var + EPS) * lng_ref[...] + lnb_ref[...]


def _merge(o, sza, xb, szb, sga, sgb, x_all, mod4, wpool, pscale, wa, wb, wo, lng, lnb,
           *, ctx_len, alpha_res):
    b, t_all, d = x_all.shape
    attn_w = o.shape[2]
    pool_w = xb.shape[2]
    tl = TOKEN_TILE
    nt = t_all // tl
    hb = tl // POOL_HALO
    n_hblocks = t_all // POOL_HALO
    row = lambda bi, ti: (bi, ti, 0)
    const2 = lambda bi, ti: (0, 0)
    in_specs = [
        pl.BlockSpec((1, tl, attn_w), row),
        pl.BlockSpec((1, tl, attn_w), row),
        pl.BlockSpec((1, tl, pool_w), row),
        pl.BlockSpec((1, POOL_HALO, pool_w), lambda bi, ti: (bi, jnp.maximum(ti * hb - 1, 0), 0)),
        pl.BlockSpec((1, POOL_HALO, pool_w),
                     lambda bi, ti: (bi, jnp.minimum((ti + 1) * hb, n_hblocks - 1), 0)),
        pl.BlockSpec((1, tl, pool_w), row),
        pl.BlockSpec((1, tl, d), row),
        pl.BlockSpec((1, tl, d), row),
        pl.BlockSpec((1, tl, d), row),
        pl.BlockSpec((1, 1, 1, 3 * d), lambda bi, ti: (bi, jnp.minimum(ti, 1), 0, 0)),
        pl.BlockSpec(wpool.shape, lambda bi, ti: (0, 0, 0)),
        pl.BlockSpec((1, pool_w), const2),
        pl.BlockSpec(wa.shape, const2),
        pl.BlockSpec(wb.shape, const2),
        pl.BlockSpec(wo.shape, const2),
        pl.BlockSpec((1, d), const2),
        pl.BlockSpec((1, d), const2),
    ]
    return pl.pallas_call(
        functools.partial(_merge_kernel, d_model=d, n_tiles=nt, ctx_len=ctx_len,
                          seq_len=t_all - ctx_len, alpha_res=alpha_res),
        grid=(b, nt),
        in_specs=in_specs,
        out_specs=pl.BlockSpec((1, tl, d), row),
        out_shape=jax.ShapeDtypeStruct((b, t_all, d), F32),
        scratch_shapes=[pltpu.VMEM((tl + 2 * POOL_HALO, pool_w), F32)],
        compiler_params=pltpu.CompilerParams(
            dimension_semantics=("arbitrary", "arbitrary"), vmem_limit_bytes=VMEM_LIMIT_BYTES),
        name="merge",
    )(o, sza, xb, xb, xb, szb, sga, sgb, x_all, mod4, wpool, pscale, wa, wb, wo, lng, lnb)


def _rope_tables(seq_len, ctx_len):
    n_freq = HEAD_DIM // 4
    tpos = jnp.arange(seq_len, dtype=jnp.int32)
    rowp = (tpos // GRID_W).astype(F32)
    colp = (tpos % GRID_W).astype(F32)
    inv_freq = 1.0 / (ROPE_THETA ** (jnp.arange(n_freq, dtype=F32) / n_freq))
    ar = rowp[:, None] * inv_freq[None, :]
    ac = colp[:, None] * inv_freq[None, :]
    ang = jnp.concatenate([ar, ar, ac, ac], axis=-1)
    cos = jnp.cos(ang)
    sin = jnp.sin(ang)
    lane = jnp.arange(HEAD_DIM)
    lower = ((lane // n_freq) % 2 == 0)[None, :]
    slo = jnp.where(lower, -sin, 0.0)
    shi = jnp.where(lower, 0.0, sin)
    ones = jnp.ones((ctx_len, HEAD_DIM), F32)
    zeros = jnp.zeros((ctx_len, HEAD_DIM), F32)
    return (jnp.concatenate([ones, cos], axis=0),
            jnp.concatenate([zeros, slo], axis=0),
            jnp.concatenate([zeros, shi], axis=0))


def kernel(x, c, ctx, c_ctx, w_mod, b_mod, w_in, q_norm, k_norm, w_pool, pool_scale,
           w_br_a, w_br_b, w_out, ln_g, ln_b):
    b, seq_len, d = x.shape
    ctx_len = ctx.shape[1]
    depth = w_in.shape[0]
    attn_w = w_br_a.shape[1]
    pool_w = w_br_b.shape[1]
    kv_w = (w_in.shape[2] - 2 * attn_w - 2 * pool_w - 2 * d) // 2
    assert ctx_len == TOKEN_TILE and seq_len % (TOKEN_TILE * KV_SUBS_PER_STEP) == 0
    assert attn_w // HEAD_DIM == GROUP * N_KV_HEADS and kv_w == N_KV_HEADS * HEAD_DIM
    alpha_res = (2 * depth) ** 0.25

    x_all = jnp.concatenate([ctx, x], axis=1)

    c_rows = jnp.zeros((8, d), F32).at[0].set(c_ctx).at[1:1 + b].set(c)
    mod = _modulation(c_rows, w_mod, b_mod)

    cos_t, slo_t, shi_t = _rope_tables(seq_len, ctx_len)

    w_in_b = w_in.astype(BF16)
    w_pool_b = w_pool.astype(BF16)
    w_a_b = w_br_a.astype(BF16)
    w_b_b = w_br_b.astype(BF16)
    w_o_b = w_out.astype(BF16)

    for l in range(depth):
        mod_ctx = jnp.broadcast_to(mod[l, 0][None], (b, 3 * d))
        mod4 = jnp.stack([mod_ctx, mod[l, 1:1 + b]], axis=1).reshape(b, 2, 1, 3 * d)
        qT, k, vT, sza, xb, szb, sga, sgb = _project(
            x_all, mod4, w_in_b[l], q_norm[l].reshape(1, HEAD_DIM), k_norm[l].reshape(1, HEAD_DIM),
            cos_t, slo_t, shi_t, attn_w=attn_w, kv_w=kv_w, pool_w=pool_w)
        o = _attention_dispatch(qT, k, vT, q_norm[l], k_norm[l])
        x_all = _merge(o, sza, xb, szb, sga, sgb, x_all, mod4, w_pool_b[l],
                       pool_scale[l].reshape(1, pool_w), w_a_b[l], w_b_b[l], w_o_b[l],
                       ln_g[l].reshape(1, d), ln_b[l].reshape(1, d),
                       ctx_len=ctx_len, alpha_res=alpha_res)
    return x_all[:, ctx_len:]
```

```python
import functools
import math

import jax
import jax.numpy as jnp
from jax import lax
from jax.experimental import pallas as pl
from jax.experimental.pallas import tpu as pltpu

F32 = jnp.float32
BF16 = jnp.bfloat16

HEAD_DIM = 128
N_KV_HEADS = 2
GROUP = 4
GRID_W = 64
POOL_WINDOWS = (2, 4, 8, 16)
POOL_GROUP = 128
POOL_HALO = 8
ROPE_THETA = 10000.0
EPS = 1e-6

TOKEN_TILE = 256
KV_SUB = 256
KV_SUBS_PER_STEP = 8
STEP_UNROLL = 4
VMEM_LIMIT_BYTES = 56 * 1024 * 1024

Q_SCALE = HEAD_DIM ** -0.5 * math.log2(math.e)
REDUCE_CHAINS = 4
BOUNDED_SCORE_LIMIT = 48.0


def _sigmoid(v):
    return 1.0 / (1.0 + jnp.exp(-v))


def _mod_kernel(c_ref, w_ref, b_ref, o_ref):
    cv = c_ref[...]
    s = cv * _sigmoid(cv)
    o_ref[0] = jnp.dot(s, w_ref[0], preferred_element_type=F32) + b_ref[0]


def _modulation(c_rows, w_mod, b_mod):
    depth, d, d3 = w_mod.shape
    rows = c_rows.shape[0]
    return pl.pallas_call(
        _mod_kernel,
        grid=(depth,),
        in_specs=[
            pl.BlockSpec((rows, d), lambda l: (0, 0)),
            pl.BlockSpec((1, d, d3), lambda l: (l, 0, 0)),
            pl.BlockSpec((1, 1, d3), lambda l: (l, 0, 0)),
        ],
        out_specs=pl.BlockSpec((1, rows, d3), lambda l: (l, 0, 0)),
        out_shape=jax.ShapeDtypeStruct((depth, rows, d3), F32),
        compiler_params=pltpu.CompilerParams(
            dimension_semantics=("arbitrary",), vmem_limit_bytes=VMEM_LIMIT_BYTES),
        name="modulation",
    )(c_rows, w_mod, b_mod.reshape(depth, 1, d3))


def _proj_kernel(cx_ref, x_ref, mod_ref, w_ref, qg_ref, kg_ref, cos_ref, slo_ref, shi_ref,
                 qT_ref, k_ref, vT_ref, sza_ref, xb_ref, szb_ref, sga_ref, sgb_ref,
                 *, d_model, attn_w, kv_w, pool_w):
    x = jnp.where(pl.program_id(1) == 0, cx_ref[0], x_ref[0])
    mod = mod_ref[0, 0]
    shift = mod[:, :d_model]
    scale = mod[:, d_model:2 * d_model]
    u = (x * (1.0 + scale) + shift).astype(BF16)

    cos = cos_ref[...]
    slo = slo_ref[...]
    shi = shi_ref[...]

    def norm_rope(h, g):
        ms = jnp.mean(h * h, axis=-1, keepdims=True)
        y = h * lax.rsqrt(ms + EPS) * g
        return y * cos + pltpu.roll(y, HEAD_DIM - 32, 1) * slo + pltpu.roll(y, 32, 1) * shi

    o = 0
    hq = jnp.dot(u, w_ref[:, o:o + attn_w], preferred_element_type=F32)
    o += attn_w
    qg = qg_ref[...]
    for h in range(attn_w // HEAD_DIM):
        qh = norm_rope(hq[:, h * HEAD_DIM:(h + 1) * HEAD_DIM], qg) * Q_SCALE
        qT_ref[0, h * HEAD_DIM:(h + 1) * HEAD_DIM, :] = qh.T.astype(BF16)

    hkv = jnp.dot(u, w_ref[:, o:o + 2 * kv_w], preferred_element_type=F32)
    o += 2 * kv_w
    kg = kg_ref[...]
    for g in range(kv_w // HEAD_DIM):
        kh = norm_rope(hkv[:, g * HEAD_DIM:(g + 1) * HEAD_DIM], kg)
        k_ref[0, :, g * HEAD_DIM:(g + 1) * HEAD_DIM] = kh.astype(BF16)
        vh = hkv[:, kv_w + g * HEAD_DIM:kv_w + (g + 1) * HEAD_DIM]
        vT_ref[0, g, 0] = vh.T.astype(BF16)

    za = jnp.dot(u, w_ref[:, o:o + attn_w], preferred_element_type=F32)
    o += attn_w
    sza_ref[0] = (za * _sigmoid(za)).astype(BF16)

    xb_ref[0] = jnp.dot(u, w_ref[:, o:o + pool_w], preferred_element_type=F32)
    o += pool_w

    zb = jnp.dot(u, w_ref[:, o:o + pool_w], preferred_element_type=F32)
    o += pool_w
    szb_ref[0] = (zb * _sigmoid(zb)).astype(BF16)

    ga = jnp.dot(u, w_ref[:, o:o + d_model], preferred_element_type=F32)
    o += d_model
    sga_ref[0] = _sigmoid(ga).astype(BF16)

    gb = jnp.dot(u, w_ref[:, o:o + d_model], preferred_element_type=F32)
    sgb_ref[0] = _sigmoid(gb).astype(BF16)


def _project(cx, x, mod4, w_in, qg, kg, cos_t, slo_t, shi_t, *, attn_w, kv_w, pool_w):
    b, seq_len, d = x.shape
    n_in = w_in.shape[1]
    tl = TOKEN_TILE
    t_all = tl + seq_len
    nt = t_all // tl
    n_kv = kv_w // HEAD_DIM
    row = lambda bi, ti: (bi, ti, 0)
    latent_row = lambda bi, ti: (bi, jnp.maximum(ti - 1, 0), 0)
    const2 = lambda bi, ti: (0, 0)
    out_shape = (
        jax.ShapeDtypeStruct((b, attn_w, t_all), BF16),
        jax.ShapeDtypeStruct((b, t_all, kv_w), BF16),
        jax.ShapeDtypeStruct((b, n_kv, nt, HEAD_DIM, tl), BF16),
        jax.ShapeDtypeStruct((b, t_all, attn_w), BF16),
        jax.ShapeDtypeStruct((b, t_all, pool_w), F32),
        jax.ShapeDtypeStruct((b, t_all, pool_w), BF16),
        jax.ShapeDtypeStruct((b, t_all, d), BF16),
        jax.ShapeDtypeStruct((b, t_all, d), BF16),
    )
    out_specs = (
        pl.BlockSpec((1, attn_w, tl), lambda bi, ti: (bi, 0, ti)),
        pl.BlockSpec((1, tl, kv_w), row),
        pl.BlockSpec((1, n_kv, 1, HEAD_DIM, tl), lambda bi, ti: (bi, 0, ti, 0, 0)),
        pl.BlockSpec((1, tl, attn_w), row),
        pl.BlockSpec((1, tl, pool_w), row),
        pl.BlockSpec((1, tl, pool_w), row),
        pl.BlockSpec((1, tl, d), row),
        pl.BlockSpec((1, tl, d), row),
    )
    in_specs = [
        pl.BlockSpec((1, tl, d), lambda bi, ti: (bi, 0, 0)),
        pl.BlockSpec((1, tl, d), latent_row),
        pl.BlockSpec((1, 1, 1, 3 * d), lambda bi, ti: (bi, jnp.minimum(ti, 1), 0, 0)),
        pl.BlockSpec((d, n_in), const2),
        pl.BlockSpec((1, HEAD_DIM), const2),
        pl.BlockSpec((1, HEAD_DIM), const2),
        pl.BlockSpec((tl, HEAD_DIM), lambda bi, ti: (ti, 0)),
        pl.BlockSpec((tl, HEAD_DIM), lambda bi, ti: (ti, 0)),
        pl.BlockSpec((tl, HEAD_DIM), lambda bi, ti: (ti, 0)),
    ]
    return pl.pallas_call(
        functools.partial(_proj_kernel, d_model=d, attn_w=attn_w, kv_w=kv_w, pool_w=pool_w),
        grid=(b, nt),
        in_specs=in_specs,
        out_specs=out_specs,
        out_shape=out_shape,
        compiler_params=pltpu.CompilerParams(
            dimension_semantics=("arbitrary", "arbitrary"), vmem_limit_bytes=VMEM_LIMIT_BYTES),
        name="projection",
    )(cx, x, mod4, w_in, qg, kg, cos_t, slo_t, shi_t)


def _col_reduce(x, op):
    nblk = x.shape[0] // 8
    accs = [x[8 * j:8 * j + 8] for j in range(min(REDUCE_CHAINS, nblk))]
    for i in range(len(accs), nblk):
        accs[i % REDUCE_CHAINS] = op(accs[i % REDUCE_CHAINS], x[8 * i:8 * i + 8])
    while len(accs) > 1:
        accs = [op(accs[2 * i], accs[2 * i + 1]) for i in range(len(accs) // 2)]
    return accs[0]


def _attn_kernel(qT_ref, k_ref, vT_ref, o_ref, s_ref, acc_ref, m_ref, l_ref,
                 *, n_latent_steps, bounded):
    t = pl.program_id(2)

    def keys(j0, nj):
        start = pl.multiple_of(j0 * KV_SUB, KV_SUB)
        return k_ref[0, pl.ds(start, nj * KV_SUB), :]

    def scores(kc, r):
        qT = qT_ref[0, r * HEAD_DIM:(r + 1) * HEAD_DIM, :]
        return jnp.dot(kc, qT, preferred_element_type=F32)

    def head_update(r, s, j0, nj, first):
        if bounded:
            p = jnp.exp2(s)
        else:
            cmax = jnp.max(_col_reduce(s, jnp.maximum), axis=0, keepdims=True)
            m_new = cmax if first else jnp.maximum(m_ref[r], cmax)
            p = jnp.exp2(s - m_new)
        lsum = jnp.sum(_col_reduce(p, jnp.add), axis=0, keepdims=True)
        pb = p.astype(BF16)
        pv = jnp.dot(vT_ref[0, 0, j0], pb[0:KV_SUB], preferred_element_type=F32)
        for jj in range(1, nj):
            pv += jnp.dot(vT_ref[0, 0, j0 + jj], pb[jj * KV_SUB:(jj + 1) * KV_SUB],
                          preferred_element_type=F32)
        if first:
            l_ref[r] = lsum
            acc_ref[r] = pv
        elif bounded:
            l_ref[r] += lsum
            acc_ref[r] += pv
        else:
            alpha = jnp.exp2(m_ref[r] - m_new)
            l_ref[r] = alpha * l_ref[r] + lsum
            acc_ref[r] = alpha * acc_ref[r] + pv
        if not bounded:
            m_ref[r] = m_new

    nj = KV_SUBS_PER_STEP
    half = nj // 2
    unroll = STEP_UNROLL if bounded and n_latent_steps % STEP_UNROLL == 0 else 1

    kc = keys(0, 1)
    s = scores(kc, 0)
    for r in range(GROUP):
        if r + 1 < GROUP:
            s_next = scores(kc, r + 1)
        else:
            s_ref[0:half * KV_SUB, :] = scores(keys(1, half), 0)
            s_ref[half * KV_SUB:, :] = scores(keys(1 + half, nj - half), 0)
        head_update(r, s, 0, 1, True)
        s = s_next

    def body(i, carry):
        s = s_ref[...]
        for u in range(unroll):
            j0 = 1 + (i * unroll + u) * nj
            j0_next = jnp.minimum(j0 + nj, 1 + (n_latent_steps - 1) * nj)
            kc = keys(j0, nj)
            for r in range(GROUP):
                s_next = scores(kc, r + 1) if r + 1 < GROUP else scores(keys(j0_next, nj), 0)
                head_update(r, s, j0, nj, False)
                s = s_next
        s_ref[...] = s
        return carry

    @pl.when(t > 0)
    def _latent_keys():
        lax.fori_loop(0, n_latent_steps // unroll, body, 0)

    for r in range(GROUP):
        o = acc_ref[r] * (1.0 / l_ref[r])
        o_ref[0, :, r * HEAD_DIM:(r + 1) * HEAD_DIM] = o.T.astype(BF16)


def _attention(qT, k, vT, *, bounded):
    b, attn_w, t_all = qT.shape
    n_kv, nt = vT.shape[1], vT.shape[2]
    tl = TOKEN_TILE
    gw = GROUP * HEAD_DIM
    n_latent_steps = (nt - 1) // KV_SUBS_PER_STEP
    return pl.pallas_call(
        functools.partial(_attn_kernel, n_latent_steps=n_latent_steps, bounded=bounded),
        grid=(b, n_kv, nt),
        in_specs=[
            pl.BlockSpec((1, gw, tl), lambda bi, g, ti: (bi, g, ti)),
            pl.BlockSpec((1, t_all, HEAD_DIM), lambda bi, g, ti: (bi, 0, g)),
            pl.BlockSpec((1, 1, nt, HEAD_DIM, KV_SUB), lambda bi, g, ti: (bi, g, 0, 0, 0)),
        ],
        out_specs=pl.BlockSpec((1, tl, gw), lambda bi, g, ti: (bi, ti, g)),
        out_shape=jax.ShapeDtypeStruct((b, t_all, attn_w), BF16),
        scratch_shapes=[
            pltpu.VMEM((KV_SUBS_PER_STEP * KV_SUB, tl), F32),
            pltpu.VMEM((GROUP, HEAD_DIM, tl), F32),
            pltpu.VMEM((GROUP, 1, tl), F32),
            pltpu.VMEM((GROUP, 1, tl), F32),
        ],
        compiler_params=pltpu.CompilerParams(
            dimension_semantics=("arbitrary", "arbitrary", "arbitrary"),
            vmem_limit_bytes=VMEM_LIMIT_BYTES),
        name="attention_bounded" if bounded else "attention_online",
    )(qT, k, vT)


def _attention_dispatch(qT, k, vT, q_gain, k_gain):
    score_bound = HEAD_DIM * Q_SCALE * jnp.max(jnp.abs(q_gain)) * jnp.max(jnp.abs(k_gain))
    return lax.cond(score_bound < BOUNDED_SCORE_LIMIT,
                    functools.partial(_attention, bounded=True),
                    functools.partial(_attention, bounded=False),
                    qT, k, vT)


def _merge_kernel(o_ref, sza_ref, xb_ref, xbp_ref, xbn_ref, szb_ref, sga_ref, sgb_ref,
                  cx_ref, x_ref, mod_ref, wpool_ref, pscale_ref, wa_ref, wb_ref, wo_ref,
                  lng_ref, lnb_ref, cx_out_ref, x_out_ref, ext_ref,
                  *, d_model, n_tiles, ctx_len, seq_len, alpha_res):
    t = pl.program_id(1)
    tl = TOKEN_TILE
    h = POOL_HALO

    prev_ok = t >= 2
    next_ok = jnp.logical_and(t >= 1, t <= n_tiles - 2)
    ext_ref[0:h, :] = jnp.where(prev_ok, xbp_ref[0], 0.0)
    ext_ref[h:h + tl, :] = xb_ref[0]
    ext_ref[h + tl:2 * h + tl, :] = jnp.where(next_ok, xbn_ref[0], 0.0)

    pos0 = jnp.where(t == 0, 0, (t - 1) * tl)
    seq = jnp.where(t == 0, ctx_len, seq_len)
    edge_row = lax.broadcasted_iota(jnp.int32, (h, POOL_GROUP), 0)

    def inv_count(pos, a, bb):
        cnt = jnp.minimum(pos + bb, seq) - jnp.maximum(pos - a, 0)
        return 1.0 / cnt.astype(F32)

    parts = []
    for gi, w in enumerate(POOL_WINDOWS):
        a = w // 2
        bb = w - a
        lanes = slice(gi * POOL_GROUP, (gi + 1) * POOL_GROUP)
        acc = ext_ref[h - a:h - a + tl, lanes]
        for j in range(-a + 1, bb):
            acc = acc + ext_ref[h + j:h + j + tl, lanes]
        mean = jnp.concatenate([
            acc[0:h] * inv_count(pos0 + edge_row, a, bb),
            acc[h:tl - h] * (1.0 / w),
            acc[tl - h:tl] * inv_count(pos0 + (tl - h) + edge_row, a, bb)], axis=0)
        m = mean - ext_ref[h:h + tl, lanes]
        parts.append(jnp.dot(m.astype(BF16), wpool_ref[gi], preferred_element_type=F32))
    p_pool = jnp.concatenate(parts, axis=-1) * pscale_ref[...]

    b_in = (p_pool * szb_ref[0].astype(F32)).astype(BF16)
    yb = jnp.dot(b_in, wb_ref[...], preferred_element_type=F32)
    a_in = (o_ref[0].astype(F32) * sza_ref[0].astype(F32)).astype(BF16)
    ya = jnp.dot(a_in, wa_ref[...], preferred_element_type=F32)
    y = (sga_ref[0].astype(F32) * ya + sgb_ref[0].astype(F32) * yb).astype(BF16)
    yo = jnp.dot(y, wo_ref[...], preferred_element_type=F32)

    gate = mod_ref[0, 0][:, 2 * d_model:]
    x_in = jnp.where(t == 0, cx_ref[0], x_ref[0])
    r = alpha_res * x_in + gate * yo
    mu = jnp.mean(r, axis=-1, keepdims=True)
    rc = r - mu
    var = jnp.mean(rc * rc, axis=-1, keepdims=True)
    res = rc * lax.rsqrt(var + EPS) * lng_ref[...] + lnb_ref[...]
    cx_out_ref[0] = res
    x_out_ref[0] = res


def _merge(o, sza, xb, szb, sga, sgb, cx, x, mod4, wpool, pscale, wa, wb, wo, lng, lnb,
           *, ctx_len, alpha_res):
    b, seq_len, d = x.shape
    attn_w = o.shape[2]
    pool_w = xb.shape[2]
    tl = TOKEN_TILE
    t_all = tl + seq_len
    nt = t_all // tl
    hb = tl // POOL_HALO
    n_hblocks = t_all // POOL_HALO
    row = lambda bi, ti: (bi, ti, 0)
    latent_row = lambda bi, ti: (bi, jnp.maximum(ti - 1, 0), 0)
    const2 = lambda bi, ti: (0, 0)
    in_specs = [
        pl.BlockSpec((1, tl, attn_w), row),
        pl.BlockSpec((1, tl, attn_w), row),
        pl.BlockSpec((1, tl, pool_w), row),
        pl.BlockSpec((1, POOL_HALO, pool_w), lambda bi, ti: (bi, jnp.maximum(ti * hb - 1, 0), 0)),
        pl.BlockSpec((1, POOL_HALO, pool_w),
                     lambda bi, ti: (bi, jnp.minimum((ti + 1) * hb, n_hblocks - 1), 0)),
        pl.BlockSpec((1, tl, pool_w), row),
        pl.BlockSpec((1, tl, d), row),
        pl.BlockSpec((1, tl, d), row),
        pl.BlockSpec((1, tl, d), lambda bi, ti: (bi, 0, 0)),
        pl.BlockSpec((1, tl, d), latent_row),
        pl.BlockSpec((1, 1, 1, 3 * d), lambda bi, ti: (bi, jnp.minimum(ti, 1), 0, 0)),
        pl.BlockSpec(wpool.shape, lambda bi, ti: (0, 0, 0)),
        pl.BlockSpec((1, pool_w), const2),
        pl.BlockSpec(wa.shape, const2),
        pl.BlockSpec(wb.shape, const2),
        pl.BlockSpec(wo.shape, const2),
        pl.BlockSpec((1, d), const2),
        pl.BlockSpec((1, d), const2),
    ]
    return pl.pallas_call(
        functools.partial(_merge_kernel, d_model=d, n_tiles=nt, ctx_len=ctx_len,
                          seq_len=seq_len, alpha_res=alpha_res),
        grid=(b, nt),
        in_specs=in_specs,
        out_specs=(
            pl.BlockSpec((1, tl, d), lambda bi, ti: (bi, jnp.minimum(ti, 1), 0)),
            pl.BlockSpec((1, tl, d), latent_row),
        ),
        out_shape=(
            jax.ShapeDtypeStruct((b, 2 * tl, d), F32),
            jax.ShapeDtypeStruct((b, seq_len, d), F32),
        ),
        scratch_shapes=[pltpu.VMEM((tl + 2 * POOL_HALO, pool_w), F32)],
        compiler_params=pltpu.CompilerParams(
            dimension_semantics=("arbitrary", "arbitrary"), vmem_limit_bytes=VMEM_LIMIT_BYTES),
        name="merge",
    )(o, sza, xb, xb, xb, szb, sga, sgb, cx, x, mod4, wpool, pscale, wa, wb, wo, lng, lnb)


def _rope_tables(seq_len, ctx_len):
    n_freq = HEAD_DIM // 4
    n_rows = seq_len // GRID_W
    inv_freq = 1.0 / (ROPE_THETA ** (jnp.arange(n_freq, dtype=F32) / n_freq))
    ar = jnp.arange(n_rows, dtype=jnp.int32).astype(F32)[:, None] * inv_freq[None, :]
    ac = jnp.arange(GRID_W, dtype=jnp.int32).astype(F32)[:, None] * inv_freq[None, :]

    def grid_table(row_part, col_part):
        rp = jnp.broadcast_to(jnp.concatenate(row_part, axis=-1)[:, None, :],
                              (n_rows, GRID_W, 2 * n_freq))
        cp = jnp.broadcast_to(jnp.concatenate(col_part, axis=-1)[None, :, :],
                              (n_rows, GRID_W, 2 * n_freq))
        return jnp.concatenate([rp, cp], axis=-1).reshape(seq_len, HEAD_DIM)

    cos_r, sin_r, cos_c, sin_c = jnp.cos(ar), jnp.sin(ar), jnp.cos(ac), jnp.sin(ac)
    zr, zc = jnp.zeros_like(sin_r), jnp.zeros_like(sin_c)
    cos = grid_table([cos_r, cos_r], [cos_c, cos_c])
    slo = grid_table([-sin_r, zr], [-sin_c, zc])
    shi = grid_table([zr, sin_r], [zc, sin_c])
    ones = jnp.ones((ctx_len, HEAD_DIM), F32)
    zeros = jnp.zeros((ctx_len, HEAD_DIM), F32)
    return (jnp.concatenate([ones, cos], axis=0),
            jnp.concatenate([zeros, slo], axis=0),
            jnp.concatenate([zeros, shi], axis=0))


def kernel(x, c, ctx, c_ctx, w_mod, b_mod, w_in, q_norm, k_norm, w_pool, pool_scale,
           w_br_a, w_br_b, w_out, ln_g, ln_b):
    b, seq_len, d = x.shape
    ctx_len = ctx.shape[1]
    depth = w_in.shape[0]
    attn_w = w_br_a.shape[1]
    pool_w = w_br_b.shape[1]
    kv_w = (w_in.shape[2] - 2 * attn_w - 2 * pool_w - 2 * d) // 2
    assert ctx_len == TOKEN_TILE and seq_len % (TOKEN_TILE * KV_SUBS_PER_STEP) == 0
    assert attn_w // HEAD_DIM == GROUP * N_KV_HEADS and kv_w == N_KV_HEADS * HEAD_DIM
    alpha_res = (2 * depth) ** 0.25

    assert seq_len % GRID_W == 0
    cx = ctx

    c_rows = jnp.zeros((8, d), F32).at[0].set(c_ctx).at[1:1 + b].set(c)
    mod = _modulation(c_rows, w_mod, b_mod)

    cos_t, slo_t, shi_t = _rope_tables(seq_len, ctx_len)

    w_in_b = w_in.astype(BF16)
    w_pool_b = w_pool.astype(BF16)
    w_a_b = w_br_a.astype(BF16)
    w_b_b = w_br_b.astype(BF16)
    w_o_b = w_out.astype(BF16)

    for l in range(depth):
        mod_ctx = jnp.broadcast_to(mod[l, 0][None], (b, 3 * d))
        mod4 = jnp.stack([mod_ctx, mod[l, 1:1 + b]], axis=1).reshape(b, 2, 1, 3 * d)
        qT, k, vT, sza, xb, szb, sga, sgb = _project(
            cx, x, mod4, w_in_b[l], q_norm[l].reshape(1, HEAD_DIM), k_norm[l].reshape(1, HEAD_DIM),
            cos_t, slo_t, shi_t, attn_w=attn_w, kv_w=kv_w, pool_w=pool_w)
        o = _attention_dispatch(qT, k, vT, q_norm[l], k_norm[l])
        cx, x = _merge(o, sza, xb, szb, sga, sgb, cx, x, mod4, w_pool_b[l],
                       pool_scale[l].reshape(1, pool_w), w_a_b[l], w_b_b[l], w_o_b[l],
                       ln_g[l].reshape(1, d), ln_b[l].reshape(1, d),
                       ctx_len=ctx_len, alpha_res=alpha_res)
    return x
```

```python
import functools
import math

import jax
import jax.numpy as jnp
from jax import lax
from jax.experimental import pallas as pl
from jax.experimental.pallas import tpu as pltpu

F32 = jnp.float32
BF16 = jnp.bfloat16

HEAD_DIM = 128
N_KV_HEADS = 2
GROUP = 4
GRID_W = 64
POOL_WINDOWS = (2, 4, 8, 16)
POOL_GROUP = 128
POOL_HALO = 8
ROPE_THETA = 10000.0
EPS = 1e-6

TOKEN_TILE = 256
KV_SUB = 256
KV_SUBS_PER_STEP = 4
STEP_UNROLL = 8
VMEM_LIMIT_BYTES = 56 * 1024 * 1024

Q_SCALE = HEAD_DIM ** -0.5 * math.log2(math.e)
REDUCE_CHAINS = 4
BOUNDED_SCORE_LIMIT = 48.0


def _sigmoid(v):
    return 1.0 / (1.0 + jnp.exp(-v))


def _mod_kernel(c_ref, w_ref, b_ref, o_ref):
    cv = c_ref[...]
    s = cv * _sigmoid(cv)
    o_ref[0] = jnp.dot(s, w_ref[0], preferred_element_type=F32) + b_ref[0]


def _modulation(c_rows, w_mod, b_mod):
    depth, d, d3 = w_mod.shape
    rows = c_rows.shape[0]
    return pl.pallas_call(
        _mod_kernel,
        grid=(depth,),
        in_specs=[
            pl.BlockSpec((rows, d), lambda l: (0, 0)),
            pl.BlockSpec((1, d, d3), lambda l: (l, 0, 0)),
            pl.BlockSpec((1, 1, d3), lambda l: (l, 0, 0)),
        ],
        out_specs=pl.BlockSpec((1, rows, d3), lambda l: (l, 0, 0)),
        out_shape=jax.ShapeDtypeStruct((depth, rows, d3), F32),
        compiler_params=pltpu.CompilerParams(
            dimension_semantics=("arbitrary",), vmem_limit_bytes=VMEM_LIMIT_BYTES),
        name="modulation",
    )(c_rows, w_mod, b_mod.reshape(depth, 1, d3))


def _proj_kernel(cx_ref, x_ref, mod_ref, w_ref, qg_ref, kg_ref, cos_ref, slo_ref, shi_ref,
                 qT_ref, k_ref, vT_ref, sza_ref, xb_ref, szb_ref, sga_ref, sgb_ref,
                 *, d_model, attn_w, kv_w, pool_w):
    x = jnp.where(pl.program_id(1) == 0, cx_ref[0], x_ref[0])
    mod = mod_ref[0, 0]
    shift = mod[:, :d_model]
    scale = mod[:, d_model:2 * d_model]
    u = (x * (1.0 + scale) + shift).astype(BF16)

    cos = cos_ref[...]
    slo = slo_ref[...]
    shi = shi_ref[...]

    def norm_rope(h, g):
        ms = jnp.mean(h * h, axis=-1, keepdims=True)
        y = h * lax.rsqrt(ms + EPS) * g
        return y * cos + pltpu.roll(y, HEAD_DIM - 32, 1) * slo + pltpu.roll(y, 32, 1) * shi

    o = 0
    hq = jnp.dot(u, w_ref[:, o:o + attn_w], preferred_element_type=F32)
    o += attn_w
    qg = qg_ref[...]
    for h in range(attn_w // HEAD_DIM):
        qh = norm_rope(hq[:, h * HEAD_DIM:(h + 1) * HEAD_DIM], qg) * Q_SCALE
        qT_ref[0, h * HEAD_DIM:(h + 1) * HEAD_DIM, :] = qh.T.astype(BF16)

    hkv = jnp.dot(u, w_ref[:, o:o + 2 * kv_w], preferred_element_type=F32)
    o += 2 * kv_w
    kg = kg_ref[...]
    for g in range(kv_w // HEAD_DIM):
        kh = norm_rope(hkv[:, g * HEAD_DIM:(g + 1) * HEAD_DIM], kg)
        k_ref[0, :, g * HEAD_DIM:(g + 1) * HEAD_DIM] = kh.astype(BF16)
        vh = hkv[:, kv_w + g * HEAD_DIM:kv_w + (g + 1) * HEAD_DIM]
        vT_ref[0, g, 0] = vh.T.astype(BF16)

    za = jnp.dot(u, w_ref[:, o:o + attn_w], preferred_element_type=F32)
    o += attn_w
    sza_ref[0] = (za * _sigmoid(za)).astype(BF16)

    xb_ref[0] = jnp.dot(u, w_ref[:, o:o + pool_w], preferred_element_type=F32)
    o += pool_w

    zb = jnp.dot(u, w_ref[:, o:o + pool_w], preferred_element_type=F32)
    o += pool_w
    szb_ref[0] = (zb * _sigmoid(zb)).astype(BF16)

    ga = jnp.dot(u, w_ref[:, o:o + d_model], preferred_element_type=F32)
    o += d_model
    sga_ref[0] = _sigmoid(ga).astype(BF16)

    gb = jnp.dot(u, w_ref[:, o:o + d_model], preferred_element_type=F32)
    sgb_ref[0] = _sigmoid(gb).astype(BF16)


def _project(cx, x, mod4, w_in, qg, kg, cos_t, slo_t, shi_t, *, attn_w, kv_w, pool_w):
    b, seq_len, d = x.shape
    n_in = w_in.shape[1]
    tl = TOKEN_TILE
    t_all = tl + seq_len
    nt = t_all // tl
    n_kv = kv_w // HEAD_DIM
    row = lambda bi, ti: (bi, ti, 0)
    latent_row = lambda bi, ti: (bi, jnp.maximum(ti - 1, 0), 0)
    const2 = lambda bi, ti: (0, 0)
    out_shape = (
        jax.ShapeDtypeStruct((b, attn_w, t_all), BF16),
        jax.ShapeDtypeStruct((b, t_all, kv_w), BF16),
        jax.ShapeDtypeStruct((b, n_kv, nt, HEAD_DIM, tl), BF16),
        jax.ShapeDtypeStruct((b, t_all, attn_w), BF16),
        jax.ShapeDtypeStruct((b, t_all, pool_w), F32),
        jax.ShapeDtypeStruct((b, t_all, pool_w), BF16),
        jax.ShapeDtypeStruct((b, t_all, d), BF16),
        jax.ShapeDtypeStruct((b, t_all, d), BF16),
    )
    out_specs = (
        pl.BlockSpec((1, attn_w, tl), lambda bi, ti: (bi, 0, ti)),
        pl.BlockSpec((1, tl, kv_w), row),
        pl.BlockSpec((1, n_kv, 1, HEAD_DIM, tl), lambda bi, ti: (bi, 0, ti, 0, 0)),
        pl.BlockSpec((1, tl, attn_w), row),
        pl.BlockSpec((1, tl, pool_w), row),
        pl.BlockSpec((1, tl, pool_w), row),
        pl.BlockSpec((1, tl, d), row),
        pl.BlockSpec((1, tl, d), row),
    )
    in_specs = [
        pl.BlockSpec((1, tl, d), lambda bi, ti: (bi, 0, 0)),
        pl.BlockSpec((1, tl, d), latent_row),
        pl.BlockSpec((1, 1, 1, 3 * d), lambda bi, ti: (bi, jnp.minimum(ti, 1), 0, 0)),
        pl.BlockSpec((d, n_in), const2),
        pl.BlockSpec((1, HEAD_DIM), const2),
        pl.BlockSpec((1, HEAD_DIM), const2),
        pl.BlockSpec((tl, HEAD_DIM), lambda bi, ti: (ti, 0)),
        pl.BlockSpec((tl, HEAD_DIM), lambda bi, ti: (ti, 0)),
        pl.BlockSpec((tl, HEAD_DIM), lambda bi, ti: (ti, 0)),
    ]
    return pl.pallas_call(
        functools.partial(_proj_kernel, d_model=d, attn_w=attn_w, kv_w=kv_w, pool_w=pool_w),
        grid=(b, nt),
        in_specs=in_specs,
        out_specs=out_specs,
        out_shape=out_shape,
        compiler_params=pltpu.CompilerParams(
            dimension_semantics=("arbitrary", "arbitrary"), vmem_limit_bytes=VMEM_LIMIT_BYTES),
        name="projection",
    )(cx, x, mod4, w_in, qg, kg, cos_t, slo_t, shi_t)


def _col_reduce(x, op):
    nblk = x.shape[0] // 8
    accs = [x[8 * j:8 * j + 8] for j in range(min(REDUCE_CHAINS, nblk))]
    for i in range(len(accs), nblk):
        accs[i % REDUCE_CHAINS] = op(accs[i % REDUCE_CHAINS], x[8 * i:8 * i + 8])
    while len(accs) > 1:
        accs = [op(accs[2 * i], accs[2 * i + 1]) for i in range(len(accs) // 2)]
    return accs[0]


def _attn_kernel(qT_ref, k_ref, vT_ref, o_ref, s_ref, acc_ref, m_ref, l_ref,
                 *, n_latent_steps, bounded):
    t = pl.program_id(2)

    def keys(j0, nj):
        start = pl.multiple_of(j0 * KV_SUB, KV_SUB)
        return k_ref[0, pl.ds(start, nj * KV_SUB), :]

    def scores(kc, r):
        qT = qT_ref[0, r * HEAD_DIM:(r + 1) * HEAD_DIM, :]
        return jnp.dot(kc, qT, preferred_element_type=F32)

    def head_update(r, s, j0, nj, first):
        if bounded:
            p = jnp.exp2(s)
        else:
            cmax = jnp.max(_col_reduce(s, jnp.maximum), axis=0, keepdims=True)
            m_new = cmax if first else jnp.maximum(m_ref[r], cmax)
            p = jnp.exp2(s - m_new)
        lsum = jnp.sum(_col_reduce(p, jnp.add), axis=0, keepdims=True)
        pb = p.astype(BF16)
        pv = jnp.dot(vT_ref[0, 0, j0], pb[0:KV_SUB], preferred_element_type=F32)
        for jj in range(1, nj):
            pv += jnp.dot(vT_ref[0, 0, j0 + jj], pb[jj * KV_SUB:(jj + 1) * KV_SUB],
                          preferred_element_type=F32)
        if first:
            l_ref[r] = lsum
            acc_ref[r] = pv
        elif bounded:
            l_ref[r] += lsum
            acc_ref[r] += pv
        else:
            alpha = jnp.exp2(m_ref[r] - m_new)
            l_ref[r] = alpha * l_ref[r] + lsum
            acc_ref[r] = alpha * acc_ref[r] + pv
        if not bounded:
            m_ref[r] = m_new

    nj = KV_SUBS_PER_STEP
    half = nj // 2
    unroll = STEP_UNROLL if bounded and n_latent_steps % STEP_UNROLL == 0 else 1

    kc = keys(0, 1)
    s = scores(kc, 0)
    for r in range(GROUP):
        if r + 1 < GROUP:
            s_next = scores(kc, r + 1)
        else:
            s_ref[0:half * KV_SUB, :] = scores(keys(1, half), 0)
            s_ref[half * KV_SUB:, :] = scores(keys(1 + half, nj - half), 0)
        head_update(r, s, 0, 1, True)
        s = s_next

    def body(i, carry):
        s = s_ref[...]
        for u in range(unroll):
            j0 = 1 + (i * unroll + u) * nj
            j0_next = jnp.minimum(j0 + nj, 1 + (n_latent_steps - 1) * nj)
            kc = keys(j0, nj)
            for r in range(GROUP):
                s_next = scores(kc, r + 1) if r + 1 < GROUP else scores(keys(j0_next, nj), 0)
                head_update(r, s, j0, nj, False)
                s = s_next
        s_ref[...] = s
        return carry

    @pl.when(t > 0)
    def _latent_keys():
        lax.fori_loop(0, n_latent_steps // unroll, body, 0)

    for r in range(GROUP):
        o = acc_ref[r] * (1.0 / l_ref[r])
        o_ref[0, :, r * HEAD_DIM:(r + 1) * HEAD_DIM] = o.T.astype(BF16)


def _attention(qT, k, vT, *, bounded):
    b, attn_w, t_all = qT.shape
    n_kv, nt = vT.shape[1], vT.shape[2]
    tl = TOKEN_TILE
    gw = GROUP * HEAD_DIM
    n_latent_steps = (nt - 1) // KV_SUBS_PER_STEP
    return pl.pallas_call(
        functools.partial(_attn_kernel, n_latent_steps=n_latent_steps, bounded=bounded),
        grid=(b, n_kv, nt),
        in_specs=[
            pl.BlockSpec((1, gw, tl), lambda bi, g, ti: (bi, g, ti)),
            pl.BlockSpec((1, t_all, HEAD_DIM), lambda bi, g, ti: (bi, 0, g)),
            pl.BlockSpec((1, 1, nt, HEAD_DIM, KV_SUB), lambda bi, g, ti: (bi, g, 0, 0, 0)),
        ],
        out_specs=pl.BlockSpec((1, tl, gw), lambda bi, g, ti: (bi, ti, g)),
        out_shape=jax.ShapeDtypeStruct((b, t_all, attn_w), BF16),
        scratch_shapes=[
            pltpu.VMEM((KV_SUBS_PER_STEP * KV_SUB, tl), F32),
            pltpu.VMEM((GROUP, HEAD_DIM, tl), F32),
            pltpu.VMEM((GROUP, 1, tl), F32),
            pltpu.VMEM((GROUP, 1, tl), F32),
        ],
        compiler_params=pltpu.CompilerParams(
            dimension_semantics=("arbitrary", "arbitrary", "arbitrary"),
            vmem_limit_bytes=VMEM_LIMIT_BYTES),
        name="attention_bounded" if bounded else "attention_online",
    )(qT, k, vT)


def _attention_dispatch(qT, k, vT, q_gain, k_gain):
    score_bound = HEAD_DIM * Q_SCALE * jnp.max(jnp.abs(q_gain)) * jnp.max(jnp.abs(k_gain))
    return lax.cond(score_bound < BOUNDED_SCORE_LIMIT,
                    functools.partial(_attention, bounded=True),
                    functools.partial(_attention, bounded=False),
                    qT, k, vT)


def _merge_kernel(o_ref, sza_ref, xb_ref, xbp_ref, xbn_ref, szb_ref, sga_ref, sgb_ref,
                  cx_ref, x_ref, mod_ref, wpool_ref, pscale_ref, wa_ref, wb_ref, wo_ref,
                  lng_ref, lnb_ref, cx_out_ref, x_out_ref, ext_ref,
                  *, d_model, n_tiles, ctx_len, seq_len, alpha_res):
    t = pl.program_id(1)
    tl = TOKEN_TILE
    h = POOL_HALO

    prev_ok = t >= 2
    next_ok = jnp.logical_and(t >= 1, t <= n_tiles - 2)
    ext_ref[0:h, :] = jnp.where(prev_ok, xbp_ref[0], 0.0)
    ext_ref[h:h + tl, :] = xb_ref[0]
    ext_ref[h + tl:2 * h + tl, :] = jnp.where(next_ok, xbn_ref[0], 0.0)

    pos0 = jnp.where(t == 0, 0, (t - 1) * tl)
    seq = jnp.where(t == 0, ctx_len, seq_len)
    edge_row = lax.broadcasted_iota(jnp.int32, (h, POOL_GROUP), 0)

    def inv_count(pos, a, bb):
        cnt = jnp.minimum(pos + bb, seq) - jnp.maximum(pos - a, 0)
        return 1.0 / cnt.astype(F32)

    parts = []
    for gi, w in enumerate(POOL_WINDOWS):
        a = w // 2
        bb = w - a
        lanes = slice(gi * POOL_GROUP, (gi + 1) * POOL_GROUP)
        acc = ext_ref[h - a:h - a + tl, lanes]
        for j in range(-a + 1, bb):
            acc = acc + ext_ref[h + j:h + j + tl, lanes]
        mean = jnp.concatenate([
            acc[0:h] * inv_count(pos0 + edge_row, a, bb),
            acc[h:tl - h] * (1.0 / w),
            acc[tl - h:tl] * inv_count(pos0 + (tl - h) + edge_row, a, bb)], axis=0)
        m = mean - ext_ref[h:h + tl, lanes]
        parts.append(jnp.dot(m.astype(BF16), wpool_ref[gi], preferred_element_type=F32))
    p_pool = jnp.concatenate(parts, axis=-1) * pscale_ref[...]

    b_in = (p_pool * szb_ref[0].astype(F32)).astype(BF16)
    yb = jnp.dot(b_in, wb_ref[...], preferred_element_type=F32)
    a_in = (o_ref[0].astype(F32) * sza_ref[0].astype(F32)).astype(BF16)
    ya = jnp.dot(a_in, wa_ref[...], preferred_element_type=F32)
    y = (sga_ref[0].astype(F32) * ya + sgb_ref[0].astype(F32) * yb).astype(BF16)
    yo = jnp.dot(y, wo_ref[...], preferred_element_type=F32)

    gate = mod_ref[0, 0][:, 2 * d_model:]
    x_in = jnp.where(t == 0, cx_ref[0], x_ref[0])
    r = alpha_res * x_in + gate * yo
    mu = jnp.mean(r, axis=-1, keepdims=True)
    rc = r - mu
    var = jnp.mean(rc * rc, axis=-1, keepdims=True)
    res = rc * lax.rsqrt(var + EPS) * lng_ref[...] + lnb_ref[...]
    cx_out_ref[0] = res
    x_out_ref[0] = res


def _merge(o, sza, xb, szb, sga, sgb, cx, x, mod4, wpool, pscale, wa, wb, wo, lng, lnb,
           *, ctx_len, alpha_res):
    b, seq_len, d = x.shape
    attn_w = o.shape[2]
    pool_w = xb.shape[2]
    tl = TOKEN_TILE
    t_all = tl + seq_len
    nt = t_all // tl
    hb = tl // POOL_HALO
    n_hblocks = t_all // POOL_HALO
    row = lambda bi, ti: (bi, ti, 0)
    latent_row = lambda bi, ti: (bi, jnp.maximum(ti - 1, 0), 0)
    const2 = lambda bi, ti: (0, 0)
    in_specs = [
        pl.BlockSpec((1, tl, attn_w), row),
        pl.BlockSpec((1, tl, attn_w), row),
        pl.BlockSpec((1, tl, pool_w), row),
        pl.BlockSpec((1, POOL_HALO, pool_w), lambda bi, ti: (bi, jnp.maximum(ti * hb - 1, 0), 0)),
        pl.BlockSpec((1, POOL_HALO, pool_w),
                     lambda bi, ti: (bi, jnp.minimum((ti + 1) * hb, n_hblocks - 1), 0)),
        pl.BlockSpec((1, tl, pool_w), row),
        pl.BlockSpec((1, tl, d), row),
        pl.BlockSpec((1, tl, d), row),
        pl.BlockSpec((1, tl, d), lambda bi, ti: (bi, 0, 0)),
        pl.BlockSpec((1, tl, d), latent_row),
        pl.BlockSpec((1, 1, 1, 3 * d), lambda bi, ti: (bi, jnp.minimum(ti, 1), 0, 0)),
        pl.BlockSpec(wpool.shape, lambda bi, ti: (0, 0, 0)),
        pl.BlockSpec((1, pool_w), const2),
        pl.BlockSpec(wa.shape, const2),
        pl.BlockSpec(wb.shape, const2),
        pl.BlockSpec(wo.shape, const2),
        pl.BlockSpec((1, d), const2),
        pl.BlockSpec((1, d), const2),
    ]
    return pl.pallas_call(
        functools.partial(_merge_kernel, d_model=d, n_tiles=nt, ctx_len=ctx_len,
                          seq_len=seq_len, alpha_res=alpha_res),
        grid=(b, nt),
        in_specs=in_specs,
        out_specs=(
            pl.BlockSpec((1, tl, d), lambda bi, ti: (bi, jnp.minimum(ti, 1), 0)),
            pl.BlockSpec((1, tl, d), latent_row),
        ),
        out_shape=(
            jax.ShapeDtypeStruct((b, 2 * tl, d), F32),
            jax.ShapeDtypeStruct((b, seq_len, d), F32),
        ),
        scratch_shapes=[pltpu.VMEM((tl + 2 * POOL_HALO, pool_w), F32)],
        compiler_params=pltpu.CompilerParams(
            dimension_semantics=("arbitrary", "arbitrary"), vmem_limit_bytes=VMEM_LIMIT_BYTES),
        name="merge",
    )(o, sza, xb, xb, xb, szb, sga, sgb, cx, x, mod4, wpool, pscale, wa, wb, wo, lng, lnb)


def _rope_tables(seq_len, ctx_len):
    n_freq = HEAD_DIM // 4
    n_rows = seq_len // GRID_W
    inv_freq = 1.0 / (ROPE_THETA ** (jnp.arange(n_freq, dtype=F32) / n_freq))
    ar = jnp.arange(n_rows, dtype=jnp.int32).astype(F32)[:, None] * inv_freq[None, :]
    ac = jnp.arange(GRID_W, dtype=jnp.int32).astype(F32)[:, None] * inv_freq[None, :]

    def grid_table(row_part, col_part):
        rp = jnp.broadcast_to(jnp.concatenate(row_part, axis=-1)[:, None, :],
                              (n_rows, GRID_W, 2 * n_freq))
        cp = jnp.broadcast_to(jnp.concatenate(col_part, axis=-1)[None, :, :],
                              (n_rows, GRID_W, 2 * n_freq))
        return jnp.concatenate([rp, cp], axis=-1).reshape(seq_len, HEAD_DIM)

    cos_r, sin_r, cos_c, sin_c = jnp.cos(ar), jnp.sin(ar), jnp.cos(ac), jnp.sin(ac)
    zr, zc = jnp.zeros_like(sin_r), jnp.zeros_like(sin_c)
    cos = grid_table([cos_r, cos_r], [cos_c, cos_c])
    slo = grid_table([-sin_r, zr], [-sin_c, zc])
    shi = grid_table([zr, sin_r], [zc, sin_c])
    ones = jnp.ones((ctx_len, HEAD_DIM), F32)
    zeros = jnp.zeros((ctx_len, HEAD_DIM), F32)
    return (jnp.concatenate([ones, cos], axis=0),
            jnp.concatenate([zeros, slo], axis=0),
            jnp.concatenate([zeros, shi], axis=0))


def kernel(x, c, ctx, c_ctx, w_mod, b_mod, w_in, q_norm, k_norm, w_pool, pool_scale,
           w_br_a, w_br_b, w_out, ln_g, ln_b):
    b, seq_len, d = x.shape
    ctx_len = ctx.shape[1]
    depth = w_in.shape[0]
    attn_w = w_br_a.shape[1]
    pool_w = w_br_b.shape[1]
    kv_w = (w_in.shape[2] - 2 * attn_w - 2 * pool_w - 2 * d) // 2
    assert ctx_len == TOKEN_TILE and seq_len % (TOKEN_TILE * KV_SUBS_PER_STEP) == 0
    assert attn_w // HEAD_DIM == GROUP * N_KV_HEADS and kv_w == N_KV_HEADS * HEAD_DIM
    alpha_res = (2 * depth) ** 0.25

    assert seq_len % GRID_W == 0
    cx = ctx

    c_rows = jnp.zeros((8, d), F32).at[0].set(c_ctx).at[1:1 + b].set(c)
    mod = _modulation(c_rows, w_mod, b_mod)

    cos_t, slo_t, shi_t = _rope_tables(seq_len, ctx_len)

    w_in_b = w_in.astype(BF16)
    w_pool_b = w_pool.astype(BF16)
    w_a_b = w_br_a.astype(BF16)
    w_b_b = w_br_b.astype(BF16)
    w_o_b = w_out.astype(BF16)

    for l in range(depth):
        mod_ctx = jnp.broadcast_to(mod[l, 0][None], (b, 3 * d))
        mod4 = jnp.stack([mod_ctx, mod[l, 1:1 + b]], axis=1).reshape(b, 2, 1, 3 * d)
        qT, k, vT, sza, xb, szb, sga, sgb = _project(
            cx, x, mod4, w_in_b[l], q_norm[l].reshape(1, HEAD_DIM), k_norm[l].reshape(1, HEAD_DIM),
            cos_t, slo_t, shi_t, attn_w=attn_w, kv_w=kv_w, pool_w=pool_w)
        o = _attention_dispatch(qT, k, vT, q_norm[l], k_norm[l])
        cx, x = _merge(o, sza, xb, szb, sga, sgb, cx, x, mod4, w_pool_b[l],
                       pool_scale[l].reshape(1, pool_w), w_a_b[l], w_b_b[l], w_o_b[l],
                       ln_g[l].reshape(1, d), ln_b[l].reshape(1, d),
                       ctx_len=ctx_len, alpha_res=alpha_res)
    return x
```

```python
import functools
import math

import jax
import jax.numpy as jnp
from jax import lax
from jax.experimental import pallas as pl
from jax.experimental.pallas import tpu as pltpu

F32 = jnp.float32
BF16 = jnp.bfloat16

HEAD_DIM = 128
N_KV_HEADS = 2
GROUP = 4
GRID_W = 64
POOL_WINDOWS = (2, 4, 8, 16)
POOL_GROUP = 128
POOL_HALO = 8
ROPE_THETA = 10000.0
EPS = 1e-6

TOKEN_TILE = 256
KV_SUB = 256
KV_SUBS_PER_STEP = 4
STEP_UNROLL = 8
VMEM_LIMIT_BYTES = 56 * 1024 * 1024

Q_SCALE = HEAD_DIM ** -0.5 * math.log2(math.e)
REDUCE_CHAINS = 4
BOUNDED_SCORE_LIMIT = 48.0


def _sigmoid(v):
    return 1.0 / (1.0 + jnp.exp(-v))


def _mod_kernel(c_ref, w_ref, b_ref, o_ref):
    cv = c_ref[...]
    s = cv * _sigmoid(cv)
    o_ref[0] = jnp.dot(s, w_ref[0], preferred_element_type=F32) + b_ref[0]


def _modulation(c_rows, w_mod, b_mod):
    depth, d, d3 = w_mod.shape
    rows = c_rows.shape[0]
    return pl.pallas_call(
        _mod_kernel,
        grid=(depth,),
        in_specs=[
            pl.BlockSpec((rows, d), lambda l: (0, 0)),
            pl.BlockSpec((1, d, d3), lambda l: (l, 0, 0)),
            pl.BlockSpec((1, 1, d3), lambda l: (l, 0, 0)),
        ],
        out_specs=pl.BlockSpec((1, rows, d3), lambda l: (l, 0, 0)),
        out_shape=jax.ShapeDtypeStruct((depth, rows, d3), F32),
        compiler_params=pltpu.CompilerParams(
            dimension_semantics=("arbitrary",), vmem_limit_bytes=VMEM_LIMIT_BYTES),
        name="modulation",
    )(c_rows, w_mod, b_mod.reshape(depth, 1, d3))


def _proj_kernel(cx_ref, x_ref, mod_ref, w_ref, qg_ref, kg_ref, cos_ref, slo_ref, shi_ref,
                 qT_ref, k_ref, vT_ref, sza_ref, xb_ref, szb_ref, sga_ref, sgb_ref,
                 *, d_model, attn_w, kv_w, pool_w):
    x = jnp.where(pl.program_id(1) == 0, cx_ref[0], x_ref[0])
    mod = mod_ref[0, 0]
    shift = mod[:, :d_model]
    scale = mod[:, d_model:2 * d_model]
    u = (x * (1.0 + scale) + shift).astype(BF16)

    cos = cos_ref[...]
    slo = slo_ref[...]
    shi = shi_ref[...]

    def norm_rope(h, g):
        ms = jnp.mean(h * h, axis=-1, keepdims=True)
        y = h * lax.rsqrt(ms + EPS) * g
        return y * cos + pltpu.roll(y, HEAD_DIM - 32, 1) * slo + pltpu.roll(y, 32, 1) * shi

    o = 0
    hq = jnp.dot(u, w_ref[:, o:o + attn_w], preferred_element_type=F32)
    o += attn_w
    qg = qg_ref[...]
    for h in range(attn_w // HEAD_DIM):
        qh = norm_rope(hq[:, h * HEAD_DIM:(h + 1) * HEAD_DIM], qg) * Q_SCALE
        qT_ref[0, h * HEAD_DIM:(h + 1) * HEAD_DIM, :] = qh.T.astype(BF16)

    hkv = jnp.dot(u, w_ref[:, o:o + 2 * kv_w], preferred_element_type=F32)
    o += 2 * kv_w
    kg = kg_ref[...]
    for g in range(kv_w // HEAD_DIM):
        kh = norm_rope(hkv[:, g * HEAD_DIM:(g + 1) * HEAD_DIM], kg)
        k_ref[0, :, g * HEAD_DIM:(g + 1) * HEAD_DIM] = kh.astype(BF16)
        vh = hkv[:, kv_w + g * HEAD_DIM:kv_w + (g + 1) * HEAD_DIM]
        vT_ref[0, g, 0] = vh.T.astype(BF16)

    za = jnp.dot(u, w_ref[:, o:o + attn_w], preferred_element_type=F32)
    o += attn_w
    sza_ref[0] = (za * _sigmoid(za)).astype(BF16)

    xb_ref[0] = jnp.dot(u, w_ref[:, o:o + pool_w], preferred_element_type=F32)
    o += pool_w

    zb = jnp.dot(u, w_ref[:, o:o + pool_w], preferred_element_type=F32)
    o += pool_w
    szb_ref[0] = (zb * _sigmoid(zb)).astype(BF16)

    ga = jnp.dot(u, w_ref[:, o:o + d_model], preferred_element_type=F32)
    o += d_model
    sga_ref[0] = _sigmoid(ga).astype(BF16)

    gb = jnp.dot(u, w_ref[:, o:o + d_model], preferred_element_type=F32)
    sgb_ref[0] = _sigmoid(gb).astype(BF16)


def _project(cx, x, mod4, w_in, qg, kg, cos_t, slo_t, shi_t, *, attn_w, kv_w, pool_w):
    b, seq_len, d = x.shape
    n_in = w_in.shape[1]
    tl = TOKEN_TILE
    t_all = tl + seq_len
    nt = t_all // tl
    n_kv = kv_w // HEAD_DIM
    row = lambda bi, ti: (bi, ti, 0)
    latent_row = lambda bi, ti: (bi, jnp.maximum(ti - 1, 0), 0)
    const2 = lambda bi, ti: (0, 0)
    out_shape = (
        jax.ShapeDtypeStruct((b, attn_w, t_all), BF16),
        jax.ShapeDtypeStruct((b, t_all, kv_w), BF16),
        jax.ShapeDtypeStruct((b, n_kv, nt, HEAD_DIM, tl), BF16),
        jax.ShapeDtypeStruct((b, t_all, attn_w), BF16),
        jax.ShapeDtypeStruct((b, t_all, pool_w), F32),
        jax.ShapeDtypeStruct((b, t_all, pool_w), BF16),
        jax.ShapeDtypeStruct((b, t_all, d), BF16),
        jax.ShapeDtypeStruct((b, t_all, d), BF16),
    )
    out_specs = (
        pl.BlockSpec((1, attn_w, tl), lambda bi, ti: (bi, 0, ti)),
        pl.BlockSpec((1, tl, kv_w), row),
        pl.BlockSpec((1, n_kv, 1, HEAD_DIM, tl), lambda bi, ti: (bi, 0, ti, 0, 0)),
        pl.BlockSpec((1, tl, attn_w), row),
        pl.BlockSpec((1, tl, pool_w), row),
        pl.BlockSpec((1, tl, pool_w), row),
        pl.BlockSpec((1, tl, d), row),
        pl.BlockSpec((1, tl, d), row),
    )
    in_specs = [
        pl.BlockSpec((1, tl, d), lambda bi, ti: (bi, 0, 0)),
        pl.BlockSpec((1, tl, d), latent_row),
        pl.BlockSpec((1, 1, 1, 3 * d), lambda bi, ti: (bi, jnp.minimum(ti, 1), 0, 0)),
        pl.BlockSpec((d, n_in), const2),
        pl.BlockSpec((1, HEAD_DIM), const2),
        pl.BlockSpec((1, HEAD_DIM), const2),
        pl.BlockSpec((tl, HEAD_DIM), lambda bi, ti: (ti, 0)),
        pl.BlockSpec((tl, HEAD_DIM), lambda bi, ti: (ti, 0)),
        pl.BlockSpec((tl, HEAD_DIM), lambda bi, ti: (ti, 0)),
    ]
    return pl.pallas_call(
        functools.partial(_proj_kernel, d_model=d, attn_w=attn_w, kv_w=kv_w, pool_w=pool_w),
        grid=(b, nt),
        in_specs=in_specs,
        out_specs=out_specs,
        out_shape=out_shape,
        compiler_params=pltpu.CompilerParams(
            dimension_semantics=("arbitrary", "arbitrary"), vmem_limit_bytes=VMEM_LIMIT_BYTES),
        name="projection",
    )(cx, x, mod4, w_in, qg, kg, cos_t, slo_t, shi_t)


def _col_reduce(x, op):
    nblk = x.shape[0] // 8
    accs = [x[8 * j:8 * j + 8] for j in range(min(REDUCE_CHAINS, nblk))]
    for i in range(len(accs), nblk):
        accs[i % REDUCE_CHAINS] = op(accs[i % REDUCE_CHAINS], x[8 * i:8 * i + 8])
    while len(accs) > 1:
        accs = [op(accs[2 * i], accs[2 * i + 1]) for i in range(len(accs) // 2)]
    return accs[0]


def _attn_kernel(qT_ref, k_ref, vT_ref, o_ref, s_ref, acc_ref, m_ref, l_ref,
                 *, n_latent_steps, bounded):
    t = pl.program_id(2)

    def keys(j0, nj):
        start = pl.multiple_of(j0 * KV_SUB, KV_SUB)
        return k_ref[0, pl.ds(start, nj * KV_SUB), :]

    def scores(kc, r):
        qT = qT_ref[0, r * HEAD_DIM:(r + 1) * HEAD_DIM, :]
        return jnp.dot(kc, qT, preferred_element_type=F32)

    def head_update(r, s, j0, nj, first):
        if bounded:
            p = jnp.exp2(s)
        else:
            cmax = jnp.max(_col_reduce(s, jnp.maximum), axis=0, keepdims=True)
            m_new = cmax if first else jnp.maximum(m_ref[r], cmax)
            p = jnp.exp2(s - m_new)
        lsum = jnp.sum(_col_reduce(p, jnp.add), axis=0, keepdims=True)
        pb = p.astype(BF16)
        pv = jnp.dot(vT_ref[0, 0, j0], pb[0:KV_SUB], preferred_element_type=F32)
        for jj in range(1, nj):
            pv += jnp.dot(vT_ref[0, 0, j0 + jj], pb[jj * KV_SUB:(jj + 1) * KV_SUB],
                          preferred_element_type=F32)
        if first:
            l_ref[r] = lsum
            acc_ref[r] = pv
        elif bounded:
            l_ref[r] += lsum
            acc_ref[r] += pv
        else:
            alpha = jnp.exp2(m_ref[r] - m_new)
            l_ref[r] = alpha * l_ref[r] + lsum
            acc_ref[r] = alpha * acc_ref[r] + pv
        if not bounded:
            m_ref[r] = m_new

    nj = KV_SUBS_PER_STEP
    half = nj // 2
    unroll = STEP_UNROLL if bounded and n_latent_steps % STEP_UNROLL == 0 else 1

    kc = keys(0, 1)
    s = scores(kc, 0)
    for r in range(GROUP):
        if r + 1 < GROUP:
            s_next = scores(kc, r + 1)
        else:
            s_ref[0:half * KV_SUB, :] = scores(keys(1, half), 0)
            s_ref[half * KV_SUB:, :] = scores(keys(1 + half, nj - half), 0)
        head_update(r, s, 0, 1, True)
        s = s_next

    def body(i, carry):
        s = s_ref[...]
        for u in range(unroll):
            j0 = 1 + (i * unroll + u) * nj
            j0_next = jnp.minimum(j0 + nj, 1 + (n_latent_steps - 1) * nj)
            kc = keys(j0, nj)
            for r in range(GROUP):
                s_next = scores(kc, r + 1) if r + 1 < GROUP else scores(keys(j0_next, nj), 0)
                head_update(r, s, j0, nj, False)
                s = s_next
        s_ref[...] = s
        return carry

    @pl.when(t > 0)
    def _latent_keys():
        lax.fori_loop(0, n_latent_steps // unroll, body, 0)

    for r in range(GROUP):
        o = acc_ref[r] * (1.0 / l_ref[r])
        o_ref[0, :, r * HEAD_DIM:(r + 1) * HEAD_DIM] = o.T.astype(BF16)


def _attention(qT, k, vT, *, bounded):
    b, attn_w, t_all = qT.shape
    n_kv, nt = vT.shape[1], vT.shape[2]
    tl = TOKEN_TILE
    gw = GROUP * HEAD_DIM
    n_latent_steps = (nt - 1) // KV_SUBS_PER_STEP
    return pl.pallas_call(
        functools.partial(_attn_kernel, n_latent_steps=n_latent_steps, bounded=bounded),
        grid=(b, n_kv, nt),
        in_specs=[
            pl.BlockSpec((1, gw, tl), lambda bi, g, ti: (bi, g, ti)),
            pl.BlockSpec((1, t_all, HEAD_DIM), lambda bi, g, ti: (bi, 0, g)),
            pl.BlockSpec((1, 1, nt, HEAD_DIM, KV_SUB), lambda bi, g, ti: (bi, g, 0, 0, 0)),
        ],
        out_specs=pl.BlockSpec((1, tl, gw), lambda bi, g, ti: (bi, ti, g)),
        out_shape=jax.ShapeDtypeStruct((b, t_all, attn_w), BF16),
        scratch_shapes=[
            pltpu.VMEM((KV_SUBS_PER_STEP * KV_SUB, tl), F32),
            pltpu.VMEM((GROUP, HEAD_DIM, tl), F32),
            pltpu.VMEM((GROUP, 1, tl), F32),
            pltpu.VMEM((GROUP, 1, tl), F32),
        ],
        compiler_params=pltpu.CompilerParams(
            dimension_semantics=("arbitrary", "arbitrary", "arbitrary"),
            vmem_limit_bytes=VMEM_LIMIT_BYTES),
        name="attention_bounded" if bounded else "attention_online",
    )(qT, k, vT)


def _attention_dispatch(qT, k, vT, q_gain, k_gain):
    score_bound = HEAD_DIM * Q_SCALE * jnp.max(jnp.abs(q_gain)) * jnp.max(jnp.abs(k_gain))
    return lax.cond(score_bound < BOUNDED_SCORE_LIMIT,
                    functools.partial(_attention, bounded=True),
                    functools.partial(_attention, bounded=False),
                    qT, k, vT)


def _merge_kernel(oa_ref, ob_ref, szaa_ref, szab_ref, xba_ref, xbb_ref, xbp_ref, xbn_ref,
                  szba_ref, szbb_ref, sgaa_ref, sgab_ref, sgba_ref, sgbb_ref,
                  cx_ref, x_ref, mod_ref, wpool_ref, pscale_ref, wa_ref, wb_ref, wo_ref,
                  lng_ref, lnb_ref, cx_out_ref, x_out_ref, ext_ref,
                  *, d_model, n_tiles, ctx_len, seq_len, alpha_res):
    j = pl.program_id(1)
    tl = TOKEN_TILE
    h = POOL_HALO
    first = j == 0

    prev_ok = j >= 2
    next_ok = jnp.logical_and(j >= 1, 2 * j <= n_tiles - 2)
    ext_ref[0:h, :] = jnp.where(prev_ok, xbp_ref[0], 0.0)
    ext_ref[h:h + tl, :] = jnp.where(first, 0.0, xba_ref[0])
    ext_ref[h + tl:h + 2 * tl, :] = xbb_ref[0]
    ext_ref[h + 2 * tl:2 * h + 2 * tl, :] = jnp.where(next_ok, xbn_ref[0], 0.0)

    pos_a = jnp.maximum(2 * j - 2, 0) * tl
    pos_b = jnp.where(first, 0, (2 * j - 1) * tl)
    seq_b = jnp.where(first, ctx_len, seq_len)
    edge_row = lax.broadcasted_iota(jnp.int32, (h, POOL_GROUP), 0)

    def inv_count(pos, seq, a, bb):
        cnt = jnp.minimum(pos + bb, seq) - jnp.maximum(pos - a, 0)
        return 1.0 / cnt.astype(F32)

    pooled = []
    for gi, w in enumerate(POOL_WINDOWS):
        a = w // 2
        bb = w - a
        lanes = slice(gi * POOL_GROUP, (gi + 1) * POOL_GROUP)
        acc = ext_ref[h - a:h - a + 2 * tl, lanes]
        for jj in range(-a + 1, bb):
            acc = acc + ext_ref[h + jj:h + jj + 2 * tl, lanes]
        mean = jnp.concatenate([
            acc[0:h] * inv_count(pos_a + edge_row, seq_len, a, bb),
            acc[h:tl - h] * (1.0 / w),
            acc[tl - h:tl] * inv_count(pos_a + (tl - h) + edge_row, seq_len, a, bb),
            acc[tl:tl + h] * inv_count(pos_b + edge_row, seq_b, a, bb),
            acc[tl + h:2 * tl - h] * (1.0 / w),
            acc[2 * tl - h:2 * tl] * inv_count(pos_b + (tl - h) + edge_row, seq_b, a, bb)], axis=0)
        pooled.append(mean - ext_ref[h:h + 2 * tl, lanes])

    halves = ((oa_ref, szaa_ref, szba_ref, sgaa_ref, sgba_ref),
              (ob_ref, szab_ref, szbb_ref, sgab_ref, sgbb_ref))
    ya = []
    for o_ref, sza_ref, _, _, _ in halves:
        a_in = (o_ref[0].astype(F32) * sza_ref[0].astype(F32)).astype(BF16)
        ya.append(jnp.dot(a_in, wa_ref[...], preferred_element_type=F32))
    yb = []
    for hi, (_, _, szb_ref, _, _) in enumerate(halves):
        rows = slice(hi * tl, (hi + 1) * tl)
        parts = [jnp.dot(pooled[gi][rows].astype(BF16), wpool_ref[gi],
                         preferred_element_type=F32) for gi in range(len(POOL_WINDOWS))]
        p_pool = jnp.concatenate(parts, axis=-1) * pscale_ref[...]
        b_in = (p_pool * szb_ref[0].astype(F32)).astype(BF16)
        yb.append(jnp.dot(b_in, wb_ref[...], preferred_element_type=F32))
    yo = []
    for hi, (_, _, _, sga_ref, sgb_ref) in enumerate(halves):
        y = (sga_ref[0].astype(F32) * ya[hi] + sgb_ref[0].astype(F32) * yb[hi]).astype(BF16)
        yo.append(jnp.dot(y, wo_ref[...], preferred_element_type=F32))

    gate_ctx = mod_ref[0, 0][:, 2 * d_model:]
    gate_lat = mod_ref[0, 1][:, 2 * d_model:]
    x_in = (x_ref[0, 0:tl, :], jnp.where(first, cx_ref[0], x_ref[0, tl:2 * tl, :]))
    gates = (gate_lat, jnp.where(first, gate_ctx, gate_lat))
    for hi in range(2):
        r = alpha_res * x_in[hi] + gates[hi] * yo[hi]
        mu = jnp.mean(r, axis=-1, keepdims=True)
        rc = r - mu
        var = jnp.mean(rc * rc, axis=-1, keepdims=True)
        res = rc * lax.rsqrt(var + EPS) * lng_ref[...] + lnb_ref[...]
        x_out_ref[0, hi * tl:(hi + 1) * tl, :] = res
        if hi == 1:
            cx_out_ref[0] = res


def _merge(o, sza, xb, szb, sga, sgb, cx, x, mod4, wpool, pscale, wa, wb, wo, lng, lnb,
           *, ctx_len, alpha_res):
    b, seq_len, d = x.shape
    attn_w = o.shape[2]
    pool_w = xb.shape[2]
    tl = TOKEN_TILE
    nt = (tl + seq_len) // tl
    assert nt % 2 == 1
    n_steps = (nt + 1) // 2
    hb = tl // POOL_HALO
    n_hblocks = nt * hb
    tile_a = lambda bi, j: (bi, jnp.maximum(2 * j - 1, 0), 0)
    tile_b = lambda bi, j: (bi, 2 * j, 0)
    const2 = lambda bi, j: (0, 0)

    def pair(width):
        return [pl.BlockSpec((1, tl, width), tile_a), pl.BlockSpec((1, tl, width), tile_b)]

    in_specs = (
        pair(attn_w) + pair(attn_w) + pair(pool_w) + [
            pl.BlockSpec((1, POOL_HALO, pool_w),
                         lambda bi, j: (bi, jnp.maximum((2 * j - 1) * hb - 1, 0), 0)),
            pl.BlockSpec((1, POOL_HALO, pool_w),
                         lambda bi, j: (bi, jnp.minimum((2 * j + 1) * hb, n_hblocks - 1), 0)),
        ] + pair(pool_w) + pair(d) + pair(d) + [
            pl.BlockSpec((1, tl, d), lambda bi, j: (bi, 0, 0)),
            pl.BlockSpec((1, 2 * tl, d), lambda bi, j: (bi, jnp.maximum(j - 1, 0), 0)),
            pl.BlockSpec((1, 2, 1, 3 * d), lambda bi, j: (bi, 0, 0, 0)),
            pl.BlockSpec(wpool.shape, lambda bi, j: (0, 0, 0)),
            pl.BlockSpec((1, pool_w), const2),
            pl.BlockSpec(wa.shape, const2),
            pl.BlockSpec(wb.shape, const2),
            pl.BlockSpec(wo.shape, const2),
            pl.BlockSpec((1, d), const2),
            pl.BlockSpec((1, d), const2),
        ])
    return pl.pallas_call(
        functools.partial(_merge_kernel, d_model=d, n_tiles=nt, ctx_len=ctx_len,
                          seq_len=seq_len, alpha_res=alpha_res),
        grid=(b, n_steps),
        in_specs=in_specs,
        out_specs=(
            pl.BlockSpec((1, tl, d), lambda bi, j: (bi, jnp.minimum(j, 1), 0)),
            pl.BlockSpec((1, 2 * tl, d), lambda bi, j: (bi, jnp.maximum(j - 1, 0), 0)),
        ),
        out_shape=(
            jax.ShapeDtypeStruct((b, 2 * tl, d), F32),
            jax.ShapeDtypeStruct((b, seq_len, d), F32),
        ),
        scratch_shapes=[pltpu.VMEM((2 * tl + 2 * POOL_HALO, pool_w), F32)],
        compiler_params=pltpu.CompilerParams(
            dimension_semantics=("arbitrary", "arbitrary"), vmem_limit_bytes=VMEM_LIMIT_BYTES),
        name="merge",
    )(o, o, sza, sza, xb, xb, xb, xb, szb, szb, sga, sga, sgb, sgb, cx, x, mod4,
      wpool, pscale, wa, wb, wo, lng, lnb)


def _rope_tables(seq_len, ctx_len):
    n_freq = HEAD_DIM // 4
    n_rows = seq_len // GRID_W
    inv_freq = 1.0 / (ROPE_THETA ** (jnp.arange(n_freq, dtype=F32) / n_freq))
    ar = jnp.arange(n_rows, dtype=jnp.int32).astype(F32)[:, None] * inv_freq[None, :]
    ac = jnp.arange(GRID_W, dtype=jnp.int32).astype(F32)[:, None] * inv_freq[None, :]

    def grid_table(row_part, col_part):
        rp = jnp.broadcast_to(jnp.concatenate(row_part, axis=-1)[:, None, :],
                              (n_rows, GRID_W, 2 * n_freq))
        cp = jnp.broadcast_to(jnp.concatenate(col_part, axis=-1)[None, :, :],
                              (n_rows, GRID_W, 2 * n_freq))
        return jnp.concatenate([rp, cp], axis=-1).reshape(seq_len, HEAD_DIM)

    cos_r, sin_r, cos_c, sin_c = jnp.cos(ar), jnp.sin(ar), jnp.cos(ac), jnp.sin(ac)
    zr, zc = jnp.zeros_like(sin_r), jnp.zeros_like(sin_c)
    cos = grid_table([cos_r, cos_r], [cos_c, cos_c])
    slo = grid_table([-sin_r, zr], [-sin_c, zc])
    shi = grid_table([zr, sin_r], [zc, sin_c])
    ones = jnp.ones((ctx_len, HEAD_DIM), F32)
    zeros = jnp.zeros((ctx_len, HEAD_DIM), F32)
    return (jnp.concatenate([ones, cos], axis=0),
            jnp.concatenate([zeros, slo], axis=0),
            jnp.concatenate([zeros, shi], axis=0))


def kernel(x, c, ctx, c_ctx, w_mod, b_mod, w_in, q_norm, k_norm, w_pool, pool_scale,
           w_br_a, w_br_b, w_out, ln_g, ln_b):
    b, seq_len, d = x.shape
    ctx_len = ctx.shape[1]
    depth = w_in.shape[0]
    attn_w = w_br_a.shape[1]
    pool_w = w_br_b.shape[1]
    kv_w = (w_in.shape[2] - 2 * attn_w - 2 * pool_w - 2 * d) // 2
    assert ctx_len == TOKEN_TILE and seq_len % (TOKEN_TILE * KV_SUBS_PER_STEP) == 0
    assert attn_w // HEAD_DIM == GROUP * N_KV_HEADS and kv_w == N_KV_HEADS * HEAD_DIM
    alpha_res = (2 * depth) ** 0.25

    assert seq_len % GRID_W == 0
    cx = ctx

    c_rows = jnp.zeros((8, d), F32).at[0].set(c_ctx).at[1:1 + b].set(c)
    mod = _modulation(c_rows, w_mod, b_mod)

    cos_t, slo_t, shi_t = _rope_tables(seq_len, ctx_len)

    w_in_b = w_in.astype(BF16)
    w_pool_b = w_pool.astype(BF16)
    w_a_b = w_br_a.astype(BF16)
    w_b_b = w_br_b.astype(BF16)
    w_o_b = w_out.astype(BF16)

    for l in range(depth):
        mod_ctx = jnp.broadcast_to(mod[l, 0][None], (b, 3 * d))
        mod4 = jnp.stack([mod_ctx, mod[l, 1:1 + b]], axis=1).reshape(b, 2, 1, 3 * d)
        qT, k, vT, sza, xb, szb, sga, sgb = _project(
            cx, x, mod4, w_in_b[l], q_norm[l].reshape(1, HEAD_DIM), k_norm[l].reshape(1, HEAD_DIM),
            cos_t, slo_t, shi_t, attn_w=attn_w, kv_w=kv_w, pool_w=pool_w)
        o = _attention_dispatch(qT, k, vT, q_norm[l], k_norm[l])
        cx, x = _merge(o, sza, xb, szb, sga, sgb, cx, x, mod4, w_pool_b[l],
                       pool_scale[l].reshape(1, pool_w), w_a_b[l], w_b_b[l], w_o_b[l],
                       ln_g[l].reshape(1, d), ln_b[l].reshape(1, d),
                       ctx_len=ctx_len, alpha_res=alpha_res)
    return x
```

```python
import functools
import math

import jax
import jax.numpy as jnp
from jax import lax
from jax.experimental import pallas as pl
from jax.experimental.pallas import tpu as pltpu

F32 = jnp.float32
BF16 = jnp.bfloat16

SUBLANES = 8

HEAD_DIM = 128
N_KV_HEADS = 2
GROUP = 4
GRID_W = 64
ROT_HALF = HEAD_DIM // 4
POOL_WINDOWS = (2, 4, 8, 16)
POOL_GROUP = 128
POOL_HALO = max(POOL_WINDOWS) // 2
ROPE_THETA = 10000.0
EPS = 1e-6

TOKEN_TILE = 256
KV_SUB = 256
KV_SUBS_PER_STEP = 4
STEP_UNROLL = 16
VMEM_LIMIT_BYTES = 56 * 1024 * 1024

Q_SCALE = HEAD_DIM ** -0.5 * math.log2(math.e)
REDUCE_CHAINS = 4
BOUNDED_SCORE_LIMIT = 48.0


def _sigmoid(v):
    return 1.0 / (1.0 + jnp.exp(-v))


def _mod_kernel(c_ref, w_ref, b_ref, o_ref):
    cv = c_ref[...]
    s = cv * _sigmoid(cv)
    o_ref[0] = jnp.dot(s, w_ref[0], preferred_element_type=F32) + b_ref[0]


def _modulation(c_rows, w_mod, b_mod):
    depth, d, d3 = w_mod.shape
    rows = c_rows.shape[0]
    return pl.pallas_call(
        _mod_kernel,
        grid=(depth,),
        in_specs=[
            pl.BlockSpec((rows, d), lambda l: (0, 0)),
            pl.BlockSpec((1, d, d3), lambda l: (l, 0, 0)),
            pl.BlockSpec((1, 1, d3), lambda l: (l, 0, 0)),
        ],
        out_specs=pl.BlockSpec((1, rows, d3), lambda l: (l, 0, 0)),
        out_shape=jax.ShapeDtypeStruct((depth, rows, d3), F32),
        compiler_params=pltpu.CompilerParams(
            dimension_semantics=("arbitrary",), vmem_limit_bytes=VMEM_LIMIT_BYTES),
        name="modulation",
    )(c_rows, w_mod, b_mod.reshape(depth, 1, d3))


def _proj_kernel(cx_ref, x_ref, mod_ref, w_ref, qg_ref, kg_ref, cos_ref, slo_ref, shi_ref,
                 qT_ref, k_ref, vT_ref, sza_ref, xb_ref, szb_ref, sga_ref, sgb_ref,
                 *, d_model, attn_w, kv_w, pool_w):
    x = jnp.where(pl.program_id(1) == 0, cx_ref[0], x_ref[0])
    mod = mod_ref[0, 0]
    shift = mod[:, :d_model]
    scale = mod[:, d_model:2 * d_model]
    u = (x * (1.0 + scale) + shift).astype(BF16)

    cos = cos_ref[...]
    slo = slo_ref[...]
    shi = shi_ref[...]

    def norm_rope(h, g):
        ms = jnp.mean(h * h, axis=-1, keepdims=True)
        y = h * lax.rsqrt(ms + EPS) * g
        return (y * cos + pltpu.roll(y, HEAD_DIM - ROT_HALF, 1) * slo
                + pltpu.roll(y, ROT_HALF, 1) * shi)

    o = 0
    hq = jnp.dot(u, w_ref[:, o:o + attn_w], preferred_element_type=F32)
    o += attn_w
    qg = qg_ref[...]
    for h in range(attn_w // HEAD_DIM):
        qh = norm_rope(hq[:, h * HEAD_DIM:(h + 1) * HEAD_DIM], qg) * Q_SCALE
        qT_ref[0, h * HEAD_DIM:(h + 1) * HEAD_DIM, :] = qh.T.astype(BF16)

    hkv = jnp.dot(u, w_ref[:, o:o + 2 * kv_w], preferred_element_type=F32)
    o += 2 * kv_w
    kg = kg_ref[...]
    for g in range(kv_w // HEAD_DIM):
        kh = norm_rope(hkv[:, g * HEAD_DIM:(g + 1) * HEAD_DIM], kg)
        k_ref[0, :, g * HEAD_DIM:(g + 1) * HEAD_DIM] = kh.astype(BF16)
        vh = hkv[:, kv_w + g * HEAD_DIM:kv_w + (g + 1) * HEAD_DIM]
        vT_ref[0, g, 0] = vh.T.astype(BF16)

    za = jnp.dot(u, w_ref[:, o:o + attn_w], preferred_element_type=F32)
    o += attn_w
    sza_ref[0] = (za * _sigmoid(za)).astype(BF16)

    xb_ref[0] = jnp.dot(u, w_ref[:, o:o + pool_w], preferred_element_type=F32)
    o += pool_w

    zb = jnp.dot(u, w_ref[:, o:o + pool_w], preferred_element_type=F32)
    o += pool_w
    szb_ref[0] = (zb * _sigmoid(zb)).astype(BF16)

    ga = jnp.dot(u, w_ref[:, o:o + d_model], preferred_element_type=F32)
    o += d_model
    sga_ref[0] = _sigmoid(ga).astype(BF16)

    gb = jnp.dot(u, w_ref[:, o:o + d_model], preferred_element_type=F32)
    sgb_ref[0] = _sigmoid(gb).astype(BF16)


def _project(cx, x, mod4, w_in, qg, kg, cos_t, slo_t, shi_t, *, attn_w, kv_w, pool_w):
    b, seq_len, d = x.shape
    n_in = w_in.shape[1]
    tl = TOKEN_TILE
    t_all = tl + seq_len
    nt = t_all // tl
    n_kv = kv_w // HEAD_DIM
    row = lambda bi, ti: (bi, ti, 0)
    latent_row = lambda bi, ti: (bi, jnp.maximum(ti - 1, 0), 0)
    const2 = lambda bi, ti: (0, 0)
    out_shape = (
        jax.ShapeDtypeStruct((b, attn_w, t_all), BF16),
        jax.ShapeDtypeStruct((b, t_all, kv_w), BF16),
        jax.ShapeDtypeStruct((b, n_kv, nt, HEAD_DIM, tl), BF16),
        jax.ShapeDtypeStruct((b, t_all, attn_w), BF16),
        jax.ShapeDtypeStruct((b, t_all, pool_w), F32),
        jax.ShapeDtypeStruct((b, t_all, pool_w), BF16),
        jax.ShapeDtypeStruct((b, t_all, d), BF16),
        jax.ShapeDtypeStruct((b, t_all, d), BF16),
    )
    out_specs = (
        pl.BlockSpec((1, attn_w, tl), lambda bi, ti: (bi, 0, ti)),
        pl.BlockSpec((1, tl, kv_w), row),
        pl.BlockSpec((1, n_kv, 1, HEAD_DIM, tl), lambda bi, ti: (bi, 0, ti, 0, 0)),
        pl.BlockSpec((1, tl, attn_w), row),
        pl.BlockSpec((1, tl, pool_w), row),
        pl.BlockSpec((1, tl, pool_w), row),
        pl.BlockSpec((1, tl, d), row),
        pl.BlockSpec((1, tl, d), row),
    )
    in_specs = [
        pl.BlockSpec((1, tl, d), lambda bi, ti: (bi, 0, 0)),
        pl.BlockSpec((1, tl, d), latent_row),
        pl.BlockSpec((1, 1, 1, 3 * d), lambda bi, ti: (bi, jnp.minimum(ti, 1), 0, 0)),
        pl.BlockSpec((d, n_in), const2),
        pl.BlockSpec((1, HEAD_DIM), const2),
        pl.BlockSpec((1, HEAD_DIM), const2),
        pl.BlockSpec((tl, HEAD_DIM), lambda bi, ti: (ti, 0)),
        pl.BlockSpec((tl, HEAD_DIM), lambda bi, ti: (ti, 0)),
        pl.BlockSpec((tl, HEAD_DIM), lambda bi, ti: (ti, 0)),
    ]
    return pl.pallas_call(
        functools.partial(_proj_kernel, d_model=d, attn_w=attn_w, kv_w=kv_w, pool_w=pool_w),
        grid=(b, nt),
        in_specs=in_specs,
        out_specs=out_specs,
        out_shape=out_shape,
        compiler_params=pltpu.CompilerParams(
            dimension_semantics=("arbitrary", "arbitrary"), vmem_limit_bytes=VMEM_LIMIT_BYTES),
        name="projection",
    )(cx, x, mod4, w_in, qg, kg, cos_t, slo_t, shi_t)


def _col_reduce(x, op):
    sl = SUBLANES
    nblk = x.shape[0] // sl
    accs = [x[sl * j:sl * j + sl] for j in range(min(REDUCE_CHAINS, nblk))]
    for i in range(len(accs), nblk):
        accs[i % REDUCE_CHAINS] = op(accs[i % REDUCE_CHAINS], x[sl * i:sl * i + sl])
    while len(accs) > 1:
        accs = [op(accs[2 * i], accs[2 * i + 1]) for i in range(len(accs) // 2)]
    return accs[0]


def _attn_kernel(qT_ref, k_ref, vT_ref, o_ref, s_ref, acc_ref, m_ref, l_ref,
                 *, n_latent_steps, bounded):
    t = pl.program_id(2)

    def keys(j0, nj):
        start = pl.multiple_of(j0 * KV_SUB, KV_SUB)
        return k_ref[0, pl.ds(start, nj * KV_SUB), :]

    def scores(kc, r):
        qT = qT_ref[0, r * HEAD_DIM:(r + 1) * HEAD_DIM, :]
        return jnp.dot(kc, qT, preferred_element_type=F32)

    def head_update(r, s, j0, nj, first):
        if bounded:
            p = jnp.exp2(s)
        else:
            cmax = jnp.max(_col_reduce(s, jnp.maximum), axis=0, keepdims=True)
            m_new = cmax if first else jnp.maximum(m_ref[r], cmax)
            p = jnp.exp2(s - m_new)
        lsum = jnp.sum(_col_reduce(p, jnp.add), axis=0, keepdims=True)
        pb = p.astype(BF16)
        pv = jnp.dot(vT_ref[0, 0, j0], pb[0:KV_SUB], preferred_element_type=F32)
        for jj in range(1, nj):
            pv += jnp.dot(vT_ref[0, 0, j0 + jj], pb[jj * KV_SUB:(jj + 1) * KV_SUB],
                          preferred_element_type=F32)
        if first:
            l_ref[r] = lsum
            acc_ref[r] = pv
        elif bounded:
            l_ref[r] += lsum
            acc_ref[r] += pv
        else:
            alpha = jnp.exp2(m_ref[r] - m_new)
            l_ref[r] = alpha * l_ref[r] + lsum
            acc_ref[r] = alpha * acc_ref[r] + pv
        if not bounded:
            m_ref[r] = m_new

    nj = KV_SUBS_PER_STEP
    half = nj // 2
    unroll = STEP_UNROLL if bounded and n_latent_steps % STEP_UNROLL == 0 else 1

    kc = keys(0, 1)
    s = scores(kc, 0)
    for r in range(GROUP):
        if r + 1 < GROUP:
            s_next = scores(kc, r + 1)
        else:
            s_ref[0:half * KV_SUB, :] = scores(keys(1, half), 0)
            s_ref[half * KV_SUB:, :] = scores(keys(1 + half, nj - half), 0)
        head_update(r, s, 0, 1, True)
        s = s_next

    def body(i, carry):
        s = s_ref[...]
        for u in range(unroll):
            j0 = 1 + (i * unroll + u) * nj
            j0_next = jnp.minimum(j0 + nj, 1 + (n_latent_steps - 1) * nj)
            kc = keys(j0, nj)
            for r in range(GROUP):
                s_next = scores(kc, r + 1) if r + 1 < GROUP else scores(keys(j0_next, nj), 0)
                head_update(r, s, j0, nj, False)
                s = s_next
        s_ref[...] = s
        return carry

    @pl.when(t > 0)
    def _latent_keys():
        lax.fori_loop(0, n_latent_steps // unroll, body, 0)

    for r in range(GROUP):
        o = acc_ref[r] * (1.0 / l_ref[r])
        o_ref[0, :, r * HEAD_DIM:(r + 1) * HEAD_DIM] = o.T.astype(BF16)


def _attention(qT, k, vT, *, bounded):
    b, attn_w, t_all = qT.shape
    n_kv, nt = vT.shape[1], vT.shape[2]
    tl = TOKEN_TILE
    gw = GROUP * HEAD_DIM
    n_latent_steps = (nt - 1) // KV_SUBS_PER_STEP
    return pl.pallas_call(
        functools.partial(_attn_kernel, n_latent_steps=n_latent_steps, bounded=bounded),
        grid=(b, n_kv, nt),
        in_specs=[
            pl.BlockSpec((1, gw, tl), lambda bi, g, ti: (bi, g, ti)),
            pl.BlockSpec((1, t_all, HEAD_DIM), lambda bi, g, ti: (bi, 0, g)),
            pl.BlockSpec((1, 1, nt, HEAD_DIM, KV_SUB), lambda bi, g, ti: (bi, g, 0, 0, 0)),
        ],
        out_specs=pl.BlockSpec((1, tl, gw), lambda bi, g, ti: (bi, ti, g)),
        out_shape=jax.ShapeDtypeStruct((b, t_all, attn_w), BF16),
        scratch_shapes=[
            pltpu.VMEM((KV_SUBS_PER_STEP * KV_SUB, tl), F32),
            pltpu.VMEM((GROUP, HEAD_DIM, tl), F32),
            pltpu.VMEM((GROUP, 1, tl), F32),
            pltpu.VMEM((GROUP, 1, tl), F32),
        ],
        compiler_params=pltpu.CompilerParams(
            dimension_semantics=("arbitrary", "arbitrary", "arbitrary"),
            vmem_limit_bytes=VMEM_LIMIT_BYTES),
        name="attention_bounded" if bounded else "attention_online",
    )(qT, k, vT)


def _attention_dispatch(qT, k, vT, q_gain, k_gain):
    score_bound = HEAD_DIM * Q_SCALE * jnp.max(jnp.abs(q_gain)) * jnp.max(jnp.abs(k_gain))
    return lax.cond(score_bound < BOUNDED_SCORE_LIMIT,
                    functools.partial(_attention, bounded=True),
                    functools.partial(_attention, bounded=False),
                    qT, k, vT)


def _merge_kernel(oa_ref, ob_ref, szaa_ref, szab_ref, xba_ref, xbb_ref, xbp_ref, xbn_ref,
                  szba_ref, szbb_ref, sgaa_ref, sgab_ref, sgba_ref, sgbb_ref,
                  cx_ref, x_ref, mod_ref, wpool_ref, pscale_ref, wa_ref, wb_ref, wo_ref,
                  lng_ref, lnb_ref, cx_out_ref, x_out_ref, ext_ref,
                  *, d_model, n_tiles, ctx_len, seq_len, alpha_res):
    j = pl.program_id(1)
    tl = TOKEN_TILE
    h = POOL_HALO
    first = j == 0

    prev_ok = j >= 2
    next_ok = jnp.logical_and(j >= 1, 2 * j <= n_tiles - 2)
    ext_ref[0:h, :] = jnp.where(prev_ok, xbp_ref[0], 0.0)
    ext_ref[h:h + tl, :] = jnp.where(first, 0.0, xba_ref[0])
    ext_ref[h + tl:h + 2 * tl, :] = xbb_ref[0]
    ext_ref[h + 2 * tl:2 * h + 2 * tl, :] = jnp.where(next_ok, xbn_ref[0], 0.0)

    pos_a = jnp.maximum(2 * j - 2, 0) * tl
    pos_b = jnp.where(first, 0, (2 * j - 1) * tl)
    seq_b = jnp.where(first, ctx_len, seq_len)
    edge_row = lax.broadcasted_iota(jnp.int32, (h, POOL_GROUP), 0)

    def inv_count(pos, seq, a, bb):
        cnt = jnp.minimum(pos + bb, seq) - jnp.maximum(pos - a, 0)
        return 1.0 / cnt.astype(F32)

    pooled = []
    for gi, w in enumerate(POOL_WINDOWS):
        a = w // 2
        bb = w - a
        lanes = slice(gi * POOL_GROUP, (gi + 1) * POOL_GROUP)
        acc = ext_ref[h - a:h - a + 2 * tl, lanes]
        for jj in range(-a + 1, bb):
            acc = acc + ext_ref[h + jj:h + jj + 2 * tl, lanes]
        mean = jnp.concatenate([
            acc[0:h] * inv_count(pos_a + edge_row, seq_len, a, bb),
            acc[h:tl - h] * (1.0 / w),
            acc[tl - h:tl] * inv_count(pos_a + (tl - h) + edge_row, seq_len, a, bb),
            acc[tl:tl + h] * inv_count(pos_b + edge_row, seq_b, a, bb),
            acc[tl + h:2 * tl - h] * (1.0 / w),
            acc[2 * tl - h:2 * tl] * inv_count(pos_b + (tl - h) + edge_row, seq_b, a, bb)], axis=0)
        pooled.append(mean - ext_ref[h:h + 2 * tl, lanes])

    halves = ((oa_ref, szaa_ref, szba_ref, sgaa_ref, sgba_ref),
              (ob_ref, szab_ref, szbb_ref, sgab_ref, sgbb_ref))
    ya = []
    for o_ref, sza_ref, _, _, _ in halves:
        a_in = (o_ref[0].astype(F32) * sza_ref[0].astype(F32)).astype(BF16)
        ya.append(jnp.dot(a_in, wa_ref[...], preferred_element_type=F32))
    yb = []
    for hi, (_, _, szb_ref, _, _) in enumerate(halves):
        rows = slice(hi * tl, (hi + 1) * tl)
        parts = [jnp.dot(pooled[gi][rows].astype(BF16), wpool_ref[gi],
                         preferred_element_type=F32) for gi in range(len(POOL_WINDOWS))]
        p_pool = jnp.concatenate(parts, axis=-1) * pscale_ref[...]
        b_in = (p_pool * szb_ref[0].astype(F32)).astype(BF16)
        yb.append(jnp.dot(b_in, wb_ref[...], preferred_element_type=F32))
    yo = []
    for hi, (_, _, _, sga_ref, sgb_ref) in enumerate(halves):
        y = (sga_ref[0].astype(F32) * ya[hi] + sgb_ref[0].astype(F32) * yb[hi]).astype(BF16)
        yo.append(jnp.dot(y, wo_ref[...], preferred_element_type=F32))

    gate_ctx = mod_ref[0, 0][:, 2 * d_model:]
    gate_lat = mod_ref[0, 1][:, 2 * d_model:]
    x_in = (x_ref[0, 0:tl, :], jnp.where(first, cx_ref[0], x_ref[0, tl:2 * tl, :]))
    gates = (gate_lat, jnp.where(first, gate_ctx, gate_lat))
    for hi in range(2):
        r = alpha_res * x_in[hi] + gates[hi] * yo[hi]
        mu = jnp.mean(r, axis=-1, keepdims=True)
        rc = r - mu
        var = jnp.mean(rc * rc, axis=-1, keepdims=True)
        res = rc * lax.rsqrt(var + EPS) * lng_ref[...] + lnb_ref[...]
        x_out_ref[0, hi * tl:(hi + 1) * tl, :] = res
        if hi == 1:
            cx_out_ref[0] = res


def _merge(o, sza, xb, szb, sga, sgb, cx, x, mod4, wpool, pscale, wa, wb, wo, lng, lnb,
           *, ctx_len, alpha_res):
    b, seq_len, d = x.shape
    attn_w = o.shape[2]
    pool_w = xb.shape[2]
    tl = TOKEN_TILE
    nt = (tl + seq_len) // tl
    assert nt % 2 == 1
    n_steps = (nt + 1) // 2
    hb = tl // POOL_HALO
    n_hblocks = nt * hb
    tile_a = lambda bi, j: (bi, jnp.maximum(2 * j - 1, 0), 0)
    tile_b = lambda bi, j: (bi, 2 * j, 0)
    const2 = lambda bi, j: (0, 0)

    def pair(width):
        return [pl.BlockSpec((1, tl, width), tile_a), pl.BlockSpec((1, tl, width), tile_b)]

    in_specs = (
        pair(attn_w) + pair(attn_w) + pair(pool_w) + [
            pl.BlockSpec((1, POOL_HALO, pool_w),
                         lambda bi, j: (bi, jnp.maximum((2 * j - 1) * hb - 1, 0), 0)),
            pl.BlockSpec((1, POOL_HALO, pool_w),
                         lambda bi, j: (bi, jnp.minimum((2 * j + 1) * hb, n_hblocks - 1), 0)),
        ] + pair(pool_w) + pair(d) + pair(d) + [
            pl.BlockSpec((1, tl, d), lambda bi, j: (bi, 0, 0)),
            pl.BlockSpec((1, 2 * tl, d), lambda bi, j: (bi, jnp.maximum(j - 1, 0), 0)),
            pl.BlockSpec((1, 2, 1, 3 * d), lambda bi, j: (bi, 0, 0, 0)),
            pl.BlockSpec(wpool.shape, lambda bi, j: (0, 0, 0)),
            pl.BlockSpec((1, pool_w), const2),
            pl.BlockSpec(wa.shape, const2),
            pl.BlockSpec(wb.shape, const2),
            pl.BlockSpec(wo.shape, const2),
            pl.BlockSpec((1, d), const2),
            pl.BlockSpec((1, d), const2),
        ])
    return pl.pallas_call(
        functools.partial(_merge_kernel, d_model=d, n_tiles=nt, ctx_len=ctx_len,
                          seq_len=seq_len, alpha_res=alpha_res),
        grid=(b, n_steps),
        in_specs=in_specs,
        out_specs=(
            pl.BlockSpec((1, tl, d), lambda bi, j: (bi, jnp.minimum(j, 1), 0)),
            pl.BlockSpec((1, 2 * tl, d), lambda bi, j: (bi, jnp.maximum(j - 1, 0), 0)),
        ),
        out_shape=(
            jax.ShapeDtypeStruct((b, 2 * tl, d), F32),
            jax.ShapeDtypeStruct((b, seq_len, d), F32),
        ),
        scratch_shapes=[pltpu.VMEM((2 * tl + 2 * POOL_HALO, pool_w), F32)],
        compiler_params=pltpu.CompilerParams(
            dimension_semantics=("arbitrary", "arbitrary"), vmem_limit_bytes=VMEM_LIMIT_BYTES),
        name="merge",
    )(o, o, sza, sza, xb, xb, xb, xb, szb, szb, sga, sga, sgb, sgb, cx, x, mod4,
      wpool, pscale, wa, wb, wo, lng, lnb)


def _rope_tables(seq_len, ctx_len):
    n_freq = ROT_HALF
    n_rows = seq_len // GRID_W
    inv_freq = 1.0 / (ROPE_THETA ** (jnp.arange(n_freq, dtype=F32) / n_freq))
    ar = jnp.arange(n_rows, dtype=jnp.int32).astype(F32)[:, None] * inv_freq[None, :]
    ac = jnp.arange(GRID_W, dtype=jnp.int32).astype(F32)[:, None] * inv_freq[None, :]

    def grid_table(row_part, col_part):
        rp = jnp.broadcast_to(jnp.concatenate(row_part, axis=-1)[:, None, :],
                              (n_rows, GRID_W, 2 * n_freq))
        cp = jnp.broadcast_to(jnp.concatenate(col_part, axis=-1)[None, :, :],
                              (n_rows, GRID_W, 2 * n_freq))
        return jnp.concatenate([rp, cp], axis=-1).reshape(seq_len, HEAD_DIM)

    cos_r, sin_r, cos_c, sin_c = jnp.cos(ar), jnp.sin(ar), jnp.cos(ac), jnp.sin(ac)
    zr, zc = jnp.zeros_like(sin_r), jnp.zeros_like(sin_c)
    cos = grid_table([cos_r, cos_r], [cos_c, cos_c])
    slo = grid_table([-sin_r, zr], [-sin_c, zc])
    shi = grid_table([zr, sin_r], [zc, sin_c])
    ones = jnp.ones((ctx_len, HEAD_DIM), F32)
    zeros = jnp.zeros((ctx_len, HEAD_DIM), F32)
    return (jnp.concatenate([ones, cos], axis=0),
            jnp.concatenate([zeros, slo], axis=0),
            jnp.concatenate([zeros, shi], axis=0))


def kernel(x, c, ctx, c_ctx, w_mod, b_mod, w_in, q_norm, k_norm, w_pool, pool_scale,
           w_br_a, w_br_b, w_out, ln_g, ln_b):
    b, seq_len, d = x.shape
    ctx_len = ctx.shape[1]
    depth = w_in.shape[0]
    attn_w = w_br_a.shape[1]
    pool_w = w_br_b.shape[1]
    kv_w = (w_in.shape[2] - 2 * attn_w - 2 * pool_w - 2 * d) // 2
    assert ctx_len == TOKEN_TILE and seq_len % (TOKEN_TILE * KV_SUBS_PER_STEP) == 0
    assert attn_w // HEAD_DIM == GROUP * N_KV_HEADS and kv_w == N_KV_HEADS * HEAD_DIM
    assert seq_len % GRID_W == 0 and 1 + b <= SUBLANES
    alpha_res = (2 * depth) ** 0.25
    cx = ctx

    c_rows = jnp.zeros((SUBLANES, d), F32).at[0].set(c_ctx).at[1:1 + b].set(c)
    mod = _modulation(c_rows, w_mod, b_mod)

    cos_t, slo_t, shi_t = _rope_tables(seq_len, ctx_len)

    w_in_b = w_in.astype(BF16)
    w_pool_b = w_pool.astype(BF16)
    w_a_b = w_br_a.astype(BF16)
    w_b_b = w_br_b.astype(BF16)
    w_o_b = w_out.astype(BF16)

    for l in range(depth):
        mod_ctx = jnp.broadcast_to(mod[l, 0][None], (b, 3 * d))
        mod4 = jnp.stack([mod_ctx, mod[l, 1:1 + b]], axis=1).reshape(b, 2, 1, 3 * d)
        qT, k, vT, sza, xb, szb, sga, sgb = _project(
            cx, x, mod4, w_in_b[l], q_norm[l].reshape(1, HEAD_DIM), k_norm[l].reshape(1, HEAD_DIM),
            cos_t, slo_t, shi_t, attn_w=attn_w, kv_w=kv_w, pool_w=pool_w)
        o = _attention_dispatch(qT, k, vT, q_norm[l], k_norm[l])
        cx, x = _merge(o, sza, xb, szb, sga, sgb, cx, x, mod4, w_pool_b[l],
                       pool_scale[l].reshape(1, pool_w), w_a_b[l], w_b_b[l], w_o_b[l],
                       ln_g[l].reshape(1, d), ln_b[l].reshape(1, d),
                       ctx_len=ctx_len, alpha_res=alpha_res)
    return x
```

```python
import functools
import math

import jax
import jax.numpy as jnp
from jax import lax
from jax.experimental import pallas as pl
from jax.experimental.pallas import tpu as pltpu

F32 = jnp.float32
BF16 = jnp.bfloat16

SUBLANES = 8

HEAD_DIM = 128
N_KV_HEADS = 2
GROUP = 4
GRID_W = 64
ROT_HALF = HEAD_DIM // 4
POOL_WINDOWS = (2, 4, 8, 16)
POOL_GROUP = 128
POOL_HALO = max(POOL_WINDOWS) // 2
ROPE_THETA = 10000.0
EPS = 1e-6

TOKEN_TILE = 256
KV_SUB = 256
KV_SUBS_PER_STEP = 4
VMEM_LIMIT_BYTES = 56 * 1024 * 1024

Q_SCALE = HEAD_DIM ** -0.5 * math.log2(math.e)
REDUCE_CHAINS = 4
BOUNDED_SCORE_LIMIT = 48.0


def _sigmoid(v):
    return 1.0 / (1.0 + jnp.exp(-v))


def _mod_kernel(c_ref, w_ref, b_ref, o_ref):
    cv = c_ref[...]
    s = cv * _sigmoid(cv)
    o_ref[0] = jnp.dot(s, w_ref[0], preferred_element_type=F32) + b_ref[0]


def _modulation(c_rows, w_mod, b_mod):
    depth, d, d3 = w_mod.shape
    rows = c_rows.shape[0]
    return pl.pallas_call(
        _mod_kernel,
        grid=(depth,),
        in_specs=[
            pl.BlockSpec((rows, d), lambda l: (0, 0)),
            pl.BlockSpec((1, d, d3), lambda l: (l, 0, 0)),
            pl.BlockSpec((1, 1, d3), lambda l: (l, 0, 0)),
        ],
        out_specs=pl.BlockSpec((1, rows, d3), lambda l: (l, 0, 0)),
        out_shape=jax.ShapeDtypeStruct((depth, rows, d3), F32),
        compiler_params=pltpu.CompilerParams(
            dimension_semantics=("arbitrary",), vmem_limit_bytes=VMEM_LIMIT_BYTES),
        name="modulation",
    )(c_rows, w_mod, b_mod.reshape(depth, 1, d3))


def _proj_kernel(cx_ref, x_ref, mod_ref, w_ref, qg_ref, kg_ref, cos_ref, slo_ref, shi_ref,
                 qT_ref, k_ref, vT_ref, sza_ref, xb_ref, szb_ref, sga_ref, sgb_ref,
                 *, d_model, attn_w, kv_w, pool_w):
    x = jnp.where(pl.program_id(1) == 0, cx_ref[0], x_ref[0])
    mod = mod_ref[0, 0]
    shift = mod[:, :d_model]
    scale = mod[:, d_model:2 * d_model]
    u = (x * (1.0 + scale) + shift).astype(BF16)

    cos = cos_ref[...]
    slo = slo_ref[...]
    shi = shi_ref[...]

    def norm_rope(h, g):
        ms = jnp.mean(h * h, axis=-1, keepdims=True)
        y = h * lax.rsqrt(ms + EPS) * g
        return (y * cos + pltpu.roll(y, HEAD_DIM - ROT_HALF, 1) * slo
                + pltpu.roll(y, ROT_HALF, 1) * shi)

    o = 0
    hq = jnp.dot(u, w_ref[:, o:o + attn_w], preferred_element_type=F32)
    o += attn_w
    qg = qg_ref[...]
    for h in range(attn_w // HEAD_DIM):
        qh = norm_rope(hq[:, h * HEAD_DIM:(h + 1) * HEAD_DIM], qg) * Q_SCALE
        qT_ref[0, h * HEAD_DIM:(h + 1) * HEAD_DIM, :] = qh.T.astype(BF16)

    hkv = jnp.dot(u, w_ref[:, o:o + 2 * kv_w], preferred_element_type=F32)
    o += 2 * kv_w
    kg = kg_ref[...]
    for g in range(kv_w // HEAD_DIM):
        kh = norm_rope(hkv[:, g * HEAD_DIM:(g + 1) * HEAD_DIM], kg)
        k_ref[0, :, g * HEAD_DIM:(g + 1) * HEAD_DIM] = kh.astype(BF16)
        vh = hkv[:, kv_w + g * HEAD_DIM:kv_w + (g + 1) * HEAD_DIM]
        vT_ref[0, g, 0] = vh.T.astype(BF16)

    za = jnp.dot(u, w_ref[:, o:o + attn_w], preferred_element_type=F32)
    o += attn_w
    sza_ref[0] = (za * _sigmoid(za)).astype(BF16)

    xb_ref[0] = jnp.dot(u, w_ref[:, o:o + pool_w], preferred_element_type=F32)
    o += pool_w

    zb = jnp.dot(u, w_ref[:, o:o + pool_w], preferred_element_type=F32)
    o += pool_w
    szb_ref[0] = (zb * _sigmoid(zb)).astype(BF16)

    ga = jnp.dot(u, w_ref[:, o:o + d_model], preferred_element_type=F32)
    o += d_model
    sga_ref[0] = _sigmoid(ga).astype(BF16)

    gb = jnp.dot(u, w_ref[:, o:o + d_model], preferred_element_type=F32)
    sgb_ref[0] = _sigmoid(gb).astype(BF16)


def _project(cx, x, mod4, w_in, qg, kg, cos_t, slo_t, shi_t, *, attn_w, kv_w, pool_w):
    b, seq_len, d = x.shape
    n_in = w_in.shape[1]
    tl = TOKEN_TILE
    t_all = tl + seq_len
    nt = t_all // tl
    n_kv = kv_w // HEAD_DIM
    row = lambda bi, ti: (bi, ti, 0)
    latent_row = lambda bi, ti: (bi, jnp.maximum(ti - 1, 0), 0)
    const2 = lambda bi, ti: (0, 0)
    out_shape = (
        jax.ShapeDtypeStruct((b, attn_w, t_all), BF16),
        jax.ShapeDtypeStruct((b, t_all, kv_w), BF16),
        jax.ShapeDtypeStruct((b, n_kv, nt, HEAD_DIM, tl), BF16),
        jax.ShapeDtypeStruct((b, t_all, attn_w), BF16),
        jax.ShapeDtypeStruct((b, t_all, pool_w), F32),
        jax.ShapeDtypeStruct((b, t_all, pool_w), BF16),
        jax.ShapeDtypeStruct((b, t_all, d), BF16),
        jax.ShapeDtypeStruct((b, t_all, d), BF16),
    )
    out_specs = (
        pl.BlockSpec((1, attn_w, tl), lambda bi, ti: (bi, 0, ti)),
        pl.BlockSpec((1, tl, kv_w), row),
        pl.BlockSpec((1, n_kv, 1, HEAD_DIM, tl), lambda bi, ti: (bi, 0, ti, 0, 0)),
        pl.BlockSpec((1, tl, attn_w), row),
        pl.BlockSpec((1, tl, pool_w), row),
        pl.BlockSpec((1, tl, pool_w), row),
        pl.BlockSpec((1, tl, d), row),
        pl.BlockSpec((1, tl, d), row),
    )
    in_specs = [
        pl.BlockSpec((1, tl, d), lambda bi, ti: (bi, 0, 0)),
        pl.BlockSpec((1, tl, d), latent_row),
        pl.BlockSpec((1, 1, 1, 3 * d), lambda bi, ti: (bi, jnp.minimum(ti, 1), 0, 0)),
        pl.BlockSpec((d, n_in), const2),
        pl.BlockSpec((1, HEAD_DIM), const2),
        pl.BlockSpec((1, HEAD_DIM), const2),
        pl.BlockSpec((tl, HEAD_DIM), lambda bi, ti: (ti, 0)),
        pl.BlockSpec((tl, HEAD_DIM), lambda bi, ti: (ti, 0)),
        pl.BlockSpec((tl, HEAD_DIM), lambda bi, ti: (ti, 0)),
    ]
    return pl.pallas_call(
        functools.partial(_proj_kernel, d_model=d, attn_w=attn_w, kv_w=kv_w, pool_w=pool_w),
        grid=(b, nt),
        in_specs=in_specs,
        out_specs=out_specs,
        out_shape=out_shape,
        compiler_params=pltpu.CompilerParams(
            dimension_semantics=("arbitrary", "arbitrary"), vmem_limit_bytes=VMEM_LIMIT_BYTES),
        name="projection",
    )(cx, x, mod4, w_in, qg, kg, cos_t, slo_t, shi_t)


def _col_reduce(x, op):
    sl = SUBLANES
    nblk = x.shape[0] // sl
    accs = [x[sl * j:sl * j + sl] for j in range(min(REDUCE_CHAINS, nblk))]
    for i in range(len(accs), nblk):
        accs[i % REDUCE_CHAINS] = op(accs[i % REDUCE_CHAINS], x[sl * i:sl * i + sl])
    while len(accs) > 1:
        accs = [op(accs[2 * i], accs[2 * i + 1]) for i in range(len(accs) // 2)]
    return accs[0]


def _attn_kernel(qT_ref, k_ref, vT_ref, o_ref, acc_ref, m_ref, l_ref,
                 *, n_latent_steps, bounded):
    t = pl.program_id(2)

    def keys(j0, nj):
        start = j0 * KV_SUB
        if not isinstance(start, int):
            start = pl.multiple_of(start, KV_SUB)
        return k_ref[0, pl.ds(start, nj * KV_SUB), :]

    def scores(kc, r):
        qT = qT_ref[0, r * HEAD_DIM:(r + 1) * HEAD_DIM, :]
        return jnp.dot(kc, qT, preferred_element_type=F32)

    def head_update(r, s, j0, nj, first):
        if bounded:
            p = jnp.exp2(s)
        else:
            cmax = jnp.max(_col_reduce(s, jnp.maximum), axis=0, keepdims=True)
            m_new = cmax if first else jnp.maximum(m_ref[r], cmax)
            p = jnp.exp2(s - m_new)
        lsum = jnp.sum(_col_reduce(p, jnp.add), axis=0, keepdims=True)
        pb = p.astype(BF16)
        pv = jnp.dot(vT_ref[0, 0, j0], pb[0:KV_SUB], preferred_element_type=F32)
        for jj in range(1, nj):
            pv += jnp.dot(vT_ref[0, 0, j0 + jj], pb[jj * KV_SUB:(jj + 1) * KV_SUB],
                          preferred_element_type=F32)
        if first:
            l_ref[r] = lsum
            acc_ref[r] = pv
        elif bounded:
            l_ref[r] += lsum
            acc_ref[r] += pv
        else:
            alpha = jnp.exp2(m_ref[r] - m_new)
            l_ref[r] = alpha * l_ref[r] + lsum
            acc_ref[r] = alpha * acc_ref[r] + pv
        if not bounded:
            m_ref[r] = m_new

    def run_chunks(chunks, first):
        s = scores(keys(*chunks[0]), 0)
        for ci, (j0, n) in enumerate(chunks):
            kc = keys(j0, n)
            for r in range(GROUP):
                if r + 1 < GROUP:
                    s_next = scores(kc, r + 1)
                elif ci + 1 < len(chunks):
                    s_next = scores(keys(*chunks[ci + 1]), 0)
                else:
                    s_next = None
                head_update(r, s, j0, n, first and ci == 0)
                s = s_next

    nj = KV_SUBS_PER_STEP
    if bounded:
        @pl.when(t == 0)
        def _context_queries():
            run_chunks([(0, 1)], True)

        @pl.when(t > 0)
        def _latent_queries():
            run_chunks([(0, 1 + nj)] + [(1 + u * nj, nj) for u in range(1, n_latent_steps)], True)
    else:
        run_chunks([(0, 1)], True)

        @pl.when(t > 0)
        def _latent_keys():
            def body(i, carry):
                run_chunks([(1 + i * nj, nj)], False)
                return carry

            lax.fori_loop(0, n_latent_steps, body, 0)

    for r in range(GROUP):
        o = acc_ref[r] * (1.0 / l_ref[r])
        o_ref[0, :, r * HEAD_DIM:(r + 1) * HEAD_DIM] = o.T.astype(BF16)


def _attention(qT, k, vT, *, bounded):
    b, attn_w, t_all = qT.shape
    n_kv, nt = vT.shape[1], vT.shape[2]
    tl = TOKEN_TILE
    gw = GROUP * HEAD_DIM
    n_latent_steps = (nt - 1) // KV_SUBS_PER_STEP
    return pl.pallas_call(
        functools.partial(_attn_kernel, n_latent_steps=n_latent_steps, bounded=bounded),
        grid=(b, n_kv, nt),
        in_specs=[
            pl.BlockSpec((1, gw, tl), lambda bi, g, ti: (bi, g, ti)),
            pl.BlockSpec((1, t_all, HEAD_DIM), lambda bi, g, ti: (bi, 0, g)),
            pl.BlockSpec((1, 1, nt, HEAD_DIM, KV_SUB), lambda bi, g, ti: (bi, g, 0, 0, 0)),
        ],
        out_specs=pl.BlockSpec((1, tl, gw), lambda bi, g, ti: (bi, ti, g)),
        out_shape=jax.ShapeDtypeStruct((b, t_all, attn_w), BF16),
        scratch_shapes=[
            pltpu.VMEM((GROUP, HEAD_DIM, tl), F32),
            pltpu.VMEM((GROUP, 1, tl), F32),
            pltpu.VMEM((GROUP, 1, tl), F32),
        ],
        compiler_params=pltpu.CompilerParams(
            dimension_semantics=("arbitrary", "arbitrary", "arbitrary"),
            vmem_limit_bytes=VMEM_LIMIT_BYTES),
        name="attention_bounded" if bounded else "attention_online",
    )(qT, k, vT)


def _attention_dispatch(qT, k, vT, q_gain, k_gain):
    score_bound = HEAD_DIM * Q_SCALE * jnp.max(jnp.abs(q_gain)) * jnp.max(jnp.abs(k_gain))
    return lax.cond(score_bound < BOUNDED_SCORE_LIMIT,
                    functools.partial(_attention, bounded=True),
                    functools.partial(_attention, bounded=False),
                    qT, k, vT)


def _merge_kernel(oa_ref, ob_ref, szaa_ref, szab_ref, xba_ref, xbb_ref, xbp_ref, xbn_ref,
                  szba_ref, szbb_ref, sgaa_ref, sgab_ref, sgba_ref, sgbb_ref,
                  cx_ref, x_ref, mod_ref, wpool_ref, pscale_ref, wa_ref, wb_ref, wo_ref,
                  lng_ref, lnb_ref, cx_out_ref, x_out_ref, ext_ref,
                  *, d_model, n_tiles, ctx_len, seq_len, alpha_res):
    j = pl.program_id(1)
    tl = TOKEN_TILE
    h = POOL_HALO
    first = j == 0

    prev_ok = j >= 2
    next_ok = jnp.logical_and(j >= 1, 2 * j <= n_tiles - 2)
    ext_ref[0:h, :] = jnp.where(prev_ok, xbp_ref[0], 0.0)
    ext_ref[h:h + tl, :] = jnp.where(first, 0.0, xba_ref[0])
    ext_ref[h + tl:h + 2 * tl, :] = xbb_ref[0]
    ext_ref[h + 2 * tl:2 * h + 2 * tl, :] = jnp.where(next_ok, xbn_ref[0], 0.0)

    pos_a = jnp.maximum(2 * j - 2, 0) * tl
    pos_b = jnp.where(first, 0, (2 * j - 1) * tl)
    seq_b = jnp.where(first, ctx_len, seq_len)
    edge_row = lax.broadcasted_iota(jnp.int32, (h, POOL_GROUP), 0)

    def inv_count(pos, seq, a, bb):
        cnt = jnp.minimum(pos + bb, seq) - jnp.maximum(pos - a, 0)
        return 1.0 / cnt.astype(F32)

    pooled = []
    for gi, w in enumerate(POOL_WINDOWS):
        a = w // 2
        bb = w - a
        lanes = slice(gi * POOL_GROUP, (gi + 1) * POOL_GROUP)
        acc = ext_ref[h - a:h - a + 2 * tl, lanes]
        for jj in range(-a + 1, bb):
            acc = acc + ext_ref[h + jj:h + jj + 2 * tl, lanes]
        mean = jnp.concatenate([
            acc[0:h] * inv_count(pos_a + edge_row, seq_len, a, bb),
            acc[h:tl - h] * (1.0 / w),
            acc[tl - h:tl] * inv_count(pos_a + (tl - h) + edge_row, seq_len, a, bb),
            acc[tl:tl + h] * inv_count(pos_b + edge_row, seq_b, a, bb),
            acc[tl + h:2 * tl - h] * (1.0 / w),
            acc[2 * tl - h:2 * tl] * inv_count(pos_b + (tl - h) + edge_row, seq_b, a, bb)], axis=0)
        pooled.append(mean - ext_ref[h:h + 2 * tl, lanes])

    halves = ((oa_ref, szaa_ref, szba_ref, sgaa_ref, sgba_ref),
              (ob_ref, szab_ref, szbb_ref, sgab_ref, sgbb_ref))
    ya = []
    for o_ref, sza_ref, _, _, _ in halves:
        a_in = (o_ref[0].astype(F32) * sza_ref[0].astype(F32)).astype(BF16)
        ya.append(jnp.dot(a_in, wa_ref[...], preferred_element_type=F32))
    yb = []
    for hi, (_, _, szb_ref, _, _) in enumerate(halves):
        rows = slice(hi * tl, (hi + 1) * tl)
        parts = [jnp.dot(pooled[gi][rows].astype(BF16), wpool_ref[gi],
                         preferred_element_type=F32) for gi in range(len(POOL_WINDOWS))]
        p_pool = jnp.concatenate(parts, axis=-1) * pscale_ref[...]
        b_in = (p_pool * szb_ref[0].astype(F32)).astype(BF16)
        yb.append(jnp.dot(b_in, wb_ref[...], preferred_element_type=F32))
    yo = []
    for hi, (_, _, _, sga_ref, sgb_ref) in enumerate(halves):
        y = (sga_ref[0].astype(F32) * ya[hi] + sgb_ref[0].astype(F32) * yb[hi]).astype(BF16)
        yo.append(jnp.dot(y, wo_ref[...], preferred_element_type=F32))

    gate_ctx = mod_ref[0, 0][:, 2 * d_model:]
    gate_lat = mod_ref[0, 1][:, 2 * d_model:]
    x_in = (x_ref[0, 0:tl, :], jnp.where(first, cx_ref[0], x_ref[0, tl:2 * tl, :]))
    gates = (gate_lat, jnp.where(first, gate_ctx, gate_lat))
    for hi in range(2):
        r = alpha_res * x_in[hi] + gates[hi] * yo[hi]
        mu = jnp.mean(r, axis=-1, keepdims=True)
        rc = r - mu
        var = jnp.mean(rc * rc, axis=-1, keepdims=True)
        res = rc * lax.rsqrt(var + EPS) * lng_ref[...] + lnb_ref[...]
        x_out_ref[0, hi * tl:(hi + 1) * tl, :] = res
        if hi == 1:
            cx_out_ref[0] = res


def _merge(o, sza, xb, szb, sga, sgb, cx, x, mod4, wpool, pscale, wa, wb, wo, lng, lnb,
           *, ctx_len, alpha_res):
    b, seq_len, d = x.shape
    attn_w = o.shape[2]
    pool_w = xb.shape[2]
    tl = TOKEN_TILE
    nt = (tl + seq_len) // tl
    assert nt % 2 == 1
    n_steps = (nt + 1) // 2
    hb = tl // POOL_HALO
    n_hblocks = nt * hb
    tile_a = lambda bi, j: (bi, jnp.maximum(2 * j - 1, 0), 0)
    tile_b = lambda bi, j: (bi, 2 * j, 0)
    const2 = lambda bi, j: (0, 0)

    def pair(width):
        return [pl.BlockSpec((1, tl, width), tile_a), pl.BlockSpec((1, tl, width), tile_b)]

    in_specs = (
        pair(attn_w) + pair(attn_w) + pair(pool_w) + [
            pl.BlockSpec((1, POOL_HALO, pool_w),
                         lambda bi, j: (bi, jnp.maximum((2 * j - 1) * hb - 1, 0), 0)),
            pl.BlockSpec((1, POOL_HALO, pool_w),
                         lambda bi, j: (bi, jnp.minimum((2 * j + 1) * hb, n_hblocks - 1), 0)),
        ] + pair(pool_w) + pair(d) + pair(d) + [
            pl.BlockSpec((1, tl, d), lambda bi, j: (bi, 0, 0)),
            pl.BlockSpec((1, 2 * tl, d), lambda bi, j: (bi, jnp.maximum(j - 1, 0), 0)),
            pl.BlockSpec((1, 2, 1, 3 * d), lambda bi, j: (bi, 0, 0, 0)),
            pl.BlockSpec(wpool.shape, lambda bi, j: (0, 0, 0)),
            pl.BlockSpec((1, pool_w), const2),
            pl.BlockSpec(wa.shape, const2),
            pl.BlockSpec(wb.shape, const2),
            pl.BlockSpec(wo.shape, const2),
            pl.BlockSpec((1, d), const2),
            pl.BlockSpec((1, d), const2),
        ])
    return pl.pallas_call(
        functools.partial(_merge_kernel, d_model=d, n_tiles=nt, ctx_len=ctx_len,
                          seq_len=seq_len, alpha_res=alpha_res),
        grid=(b, n_steps),
        in_specs=in_specs,
        out_specs=(
            pl.BlockSpec((1, tl, d), lambda bi, j: (bi, jnp.minimum(j, 1), 0)),
            pl.BlockSpec((1, 2 * tl, d), lambda bi, j: (bi, jnp.maximum(j - 1, 0), 0)),
        ),
        out_shape=(
            jax.ShapeDtypeStruct((b, 2 * tl, d), F32),
            jax.ShapeDtypeStruct((b, seq_len, d), F32),
        ),
        scratch_shapes=[pltpu.VMEM((2 * tl + 2 * POOL_HALO, pool_w), F32)],
        compiler_params=pltpu.CompilerParams(
            dimension_semantics=("arbitrary", "arbitrary"), vmem_limit_bytes=VMEM_LIMIT_BYTES),
        name="merge",
    )(o, o, sza, sza, xb, xb, xb, xb, szb, szb, sga, sga, sgb, sgb, cx, x, mod4,
      wpool, pscale, wa, wb, wo, lng, lnb)


def _rope_tables(seq_len, ctx_len):
    n_freq = ROT_HALF
    n_rows = seq_len // GRID_W
    inv_freq = 1.0 / (ROPE_THETA ** (jnp.arange(n_freq, dtype=F32) / n_freq))
    ar = jnp.arange(n_rows, dtype=jnp.int32).astype(F32)[:, None] * inv_freq[None, :]
    ac = jnp.arange(GRID_W, dtype=jnp.int32).astype(F32)[:, None] * inv_freq[None, :]

    def grid_table(row_part, col_part):
        rp = jnp.broadcast_to(jnp.concatenate(row_part, axis=-1)[:, None, :],
                              (n_rows, GRID_W, 2 * n_freq))
        cp = jnp.broadcast_to(jnp.concatenate(col_part, axis=-1)[None, :, :],
                              (n_rows, GRID_W, 2 * n_freq))
        return jnp.concatenate([rp, cp], axis=-1).reshape(seq_len, HEAD_DIM)

    cos_r, sin_r, cos_c, sin_c = jnp.cos(ar), jnp.sin(ar), jnp.cos(ac), jnp.sin(ac)
    zr, zc = jnp.zeros_like(sin_r), jnp.zeros_like(sin_c)
    cos = grid_table([cos_r, cos_r], [cos_c, cos_c])
    slo = grid_table([-sin_r, zr], [-sin_c, zc])
    shi = grid_table([zr, sin_r], [zc, sin_c])
    ones = jnp.ones((ctx_len, HEAD_DIM), F32)
    zeros = jnp.zeros((ctx_len, HEAD_DIM), F32)
    return (jnp.concatenate([ones, cos], axis=0),
            jnp.concatenate([zeros, slo], axis=0),
            jnp.concatenate([zeros, shi], axis=0))


def kernel(x, c, ctx, c_ctx, w_mod, b_mod, w_in, q_norm, k_norm, w_pool, pool_scale,
           w_br_a, w_br_b, w_out, ln_g, ln_b):
    b, seq_len, d = x.shape
    ctx_len = ctx.shape[1]
    depth = w_in.shape[0]
    attn_w = w_br_a.shape[1]
    pool_w = w_br_b.shape[1]
    kv_w = (w_in.shape[2] - 2 * attn_w - 2 * pool_w - 2 * d) // 2
    assert ctx_len == TOKEN_TILE and seq_len % (TOKEN_TILE * KV_SUBS_PER_STEP) == 0
    assert attn_w // HEAD_DIM == GROUP * N_KV_HEADS and kv_w == N_KV_HEADS * HEAD_DIM
    assert seq_len % GRID_W == 0 and 1 + b <= SUBLANES
    alpha_res = (2 * depth) ** 0.25
    cx = ctx

    c_rows = jnp.zeros((SUBLANES, d), F32).at[0].set(c_ctx).at[1:1 + b].set(c)
    mod = _modulation(c_rows, w_mod, b_mod)

    cos_t, slo_t, shi_t = _rope_tables(seq_len, ctx_len)

    w_in_b = w_in.astype(BF16)
    w_pool_b = w_pool.astype(BF16)
    w_a_b = w_br_a.astype(BF16)
    w_b_b = w_br_b.astype(BF16)
    w_o_b = w_out.astype(BF16)

    for l in range(depth):
        mod_ctx = jnp.broadcast_to(mod[l, 0][None], (b, 3 * d))
        mod4 = jnp.stack([mod_ctx, mod[l, 1:1 + b]], axis=1).reshape(b, 2, 1, 3 * d)
        qT, k, vT, sza, xb, szb, sga, sgb = _project(
            cx, x, mod4, w_in_b[l], q_norm[l].reshape(1, HEAD_DIM), k_norm[l].reshape(1, HEAD_DIM),
            cos_t, slo_t, shi_t, attn_w=attn_w, kv_w=kv_w, pool_w=pool_w)
        o = _attention_dispatch(qT, k, vT, q_norm[l], k_norm[l])
        cx, x = _merge(o, sza, xb, szb, sga, sgb, cx, x, mod4, w_pool_b[l],
                       pool_scale[l].reshape(1, pool_w), w_a_b[l], w_b_b[l], w_o_b[l],
                       ln_g[l].reshape(1, d), ln_b[l].reshape(1, d),
                       ctx_len=ctx_len, alpha_res=alpha_res)
    return x
```

```python
import functools
import math

import jax
import jax.numpy as jnp
from jax import lax
from jax.experimental import pallas as pl
from jax.experimental.pallas import tpu as pltpu

F32 = jnp.float32
BF16 = jnp.bfloat16

SUBLANES = 8

HEAD_DIM = 128
N_KV_HEADS = 2
GROUP = 4
GRID_W = 64
ROT_HALF = HEAD_DIM // 4
POOL_WINDOWS = (2, 4, 8, 16)
POOL_GROUP = 128
POOL_HALO = max(POOL_WINDOWS) // 2
ROPE_THETA = 10000.0
EPS = 1e-6

TOKEN_TILE = 256
KV_SUB = 256
KV_SUBS_PER_STEP = 4
VMEM_LIMIT_BYTES = 56 * 1024 * 1024

Q_SCALE = HEAD_DIM ** -0.5 * math.log2(math.e)
REDUCE_CHAINS = 4
BOUNDED_SCORE_LIMIT = 48.0


def _sigmoid(v):
    return 1.0 / (1.0 + jnp.exp(-v))


def _mod_kernel(c_ref, w_ref, b_ref, o_ref):
    cv = c_ref[...]
    s = cv * _sigmoid(cv)
    o_ref[0] = jnp.dot(s, w_ref[0], preferred_element_type=F32) + b_ref[0]


def _modulation(c_rows, w_mod, b_mod):
    depth, d, d3 = w_mod.shape
    rows = c_rows.shape[0]
    return pl.pallas_call(
        _mod_kernel,
        grid=(depth,),
        in_specs=[
            pl.BlockSpec((rows, d), lambda l: (0, 0)),
            pl.BlockSpec((1, d, d3), lambda l: (l, 0, 0)),
            pl.BlockSpec((1, 1, d3), lambda l: (l, 0, 0)),
        ],
        out_specs=pl.BlockSpec((1, rows, d3), lambda l: (l, 0, 0)),
        out_shape=jax.ShapeDtypeStruct((depth, rows, d3), F32),
        compiler_params=pltpu.CompilerParams(
            dimension_semantics=("arbitrary",), vmem_limit_bytes=VMEM_LIMIT_BYTES),
        name="modulation",
    )(c_rows, w_mod, b_mod.reshape(depth, 1, d3))


def _proj_kernel(cx_ref, x_ref, mod_ref, w_ref, qg_ref, kg_ref, cos_ref, slo_ref, shi_ref,
                 qT_ref, k_ref, vT_ref, sza_ref, xb_ref, szb_ref, sga_ref, sgb_ref,
                 *, d_model, attn_w, kv_w, pool_w):
    x = jnp.where(pl.program_id(1) == 0, cx_ref[0], x_ref[0])
    mod = mod_ref[0, 0]
    shift = mod[:, :d_model]
    scale = mod[:, d_model:2 * d_model]
    u = (x * (1.0 + scale) + shift).astype(BF16)

    cos = cos_ref[...]
    slo = slo_ref[...]
    shi = shi_ref[...]

    def norm_rope(h, g):
        ms = jnp.mean(h * h, axis=-1, keepdims=True)
        y = h * lax.rsqrt(ms + EPS) * g
        return (y * cos + pltpu.roll(y, HEAD_DIM - ROT_HALF, 1) * slo
                + pltpu.roll(y, ROT_HALF, 1) * shi)

    o = 0
    hq = jnp.dot(u, w_ref[:, o:o + attn_w], preferred_element_type=F32)
    o += attn_w
    qg = qg_ref[...]
    for h in range(attn_w // HEAD_DIM):
        qh = norm_rope(hq[:, h * HEAD_DIM:(h + 1) * HEAD_DIM], qg) * Q_SCALE
        qT_ref[0, h * HEAD_DIM:(h + 1) * HEAD_DIM, :] = qh.T.astype(BF16)

    hkv = jnp.dot(u, w_ref[:, o:o + 2 * kv_w], preferred_element_type=F32)
    o += 2 * kv_w
    kg = kg_ref[...]
    for g in range(kv_w // HEAD_DIM):
        kh = norm_rope(hkv[:, g * HEAD_DIM:(g + 1) * HEAD_DIM], kg)
        k_ref[0, :, g * HEAD_DIM:(g + 1) * HEAD_DIM] = kh.astype(BF16)
        vh = hkv[:, kv_w + g * HEAD_DIM:kv_w + (g + 1) * HEAD_DIM]
        vT_ref[0, g, 0] = vh.T.astype(BF16)

    za = jnp.dot(u, w_ref[:, o:o + attn_w], preferred_element_type=F32)
    o += attn_w
    sza_ref[0] = (za * _sigmoid(za)).astype(BF16)

    xb_ref[0] = jnp.dot(u, w_ref[:, o:o + pool_w], preferred_element_type=F32)
    o += pool_w

    zb = jnp.dot(u, w_ref[:, o:o + pool_w], preferred_element_type=F32)
    o += pool_w
    szb_ref[0] = (zb * _sigmoid(zb)).astype(BF16)

    ga = jnp.dot(u, w_ref[:, o:o + d_model], preferred_element_type=F32)
    o += d_model
    sga_ref[0] = _sigmoid(ga).astype(BF16)

    gb = jnp.dot(u, w_ref[:, o:o + d_model], preferred_element_type=F32)
    sgb_ref[0] = _sigmoid(gb).astype(BF16)


def _project(cx, x, mod, w_in, qg, kg, cos_t, slo_t, shi_t, *, layer, attn_w, kv_w, pool_w):
    b, seq_len, d = x.shape
    n_in = w_in.shape[2]
    tl = TOKEN_TILE
    t_all = tl + seq_len
    nt = t_all // tl
    n_kv = kv_w // HEAD_DIM
    row = lambda bi, ti: (bi, ti, 0)
    latent_row = lambda bi, ti: (bi, jnp.maximum(ti - 1, 0), 0)
    out_shape = (
        jax.ShapeDtypeStruct((b, attn_w, t_all), BF16),
        jax.ShapeDtypeStruct((b, t_all, kv_w), BF16),
        jax.ShapeDtypeStruct((b, n_kv, nt, HEAD_DIM, tl), BF16),
        jax.ShapeDtypeStruct((b, t_all, attn_w), BF16),
        jax.ShapeDtypeStruct((b, t_all, pool_w), F32),
        jax.ShapeDtypeStruct((b, t_all, pool_w), BF16),
        jax.ShapeDtypeStruct((b, t_all, d), BF16),
        jax.ShapeDtypeStruct((b, t_all, d), BF16),
    )
    out_specs = (
        pl.BlockSpec((1, attn_w, tl), lambda bi, ti: (bi, 0, ti)),
        pl.BlockSpec((1, tl, kv_w), row),
        pl.BlockSpec((1, n_kv, 1, HEAD_DIM, tl), lambda bi, ti: (bi, 0, ti, 0, 0)),
        pl.BlockSpec((1, tl, attn_w), row),
        pl.BlockSpec((1, tl, pool_w), row),
        pl.BlockSpec((1, tl, pool_w), row),
        pl.BlockSpec((1, tl, d), row),
        pl.BlockSpec((1, tl, d), row),
    )
    in_specs = [
        pl.BlockSpec((1, tl, d), lambda bi, ti: (bi, 0, 0)),
        pl.BlockSpec((1, tl, d), latent_row),
        pl.BlockSpec((None, 1, 1, 1, 3 * d),
                     lambda bi, ti: (layer, bi, jnp.minimum(ti, 1), 0, 0)),
        pl.BlockSpec((None, d, n_in), lambda bi, ti: (layer, 0, 0)),
        pl.BlockSpec((None, 1, HEAD_DIM), lambda bi, ti: (layer, 0, 0)),
        pl.BlockSpec((None, 1, HEAD_DIM), lambda bi, ti: (layer, 0, 0)),
        pl.BlockSpec((tl, HEAD_DIM), lambda bi, ti: (ti, 0)),
        pl.BlockSpec((tl, HEAD_DIM), lambda bi, ti: (ti, 0)),
        pl.BlockSpec((tl, HEAD_DIM), lambda bi, ti: (ti, 0)),
    ]
    return pl.pallas_call(
        functools.partial(_proj_kernel, d_model=d, attn_w=attn_w, kv_w=kv_w, pool_w=pool_w),
        grid=(b, nt),
        in_specs=in_specs,
        out_specs=out_specs,
        out_shape=out_shape,
        compiler_params=pltpu.CompilerParams(
            dimension_semantics=("arbitrary", "arbitrary"), vmem_limit_bytes=VMEM_LIMIT_BYTES),
        name="projection",
    )(cx, x, mod, w_in, qg, kg, cos_t, slo_t, shi_t)


def _col_reduce(x, op):
    sl = SUBLANES
    nblk = x.shape[0] // sl
    accs = [x[sl * j:sl * j + sl] for j in range(min(REDUCE_CHAINS, nblk))]
    for i in range(len(accs), nblk):
        accs[i % REDUCE_CHAINS] = op(accs[i % REDUCE_CHAINS], x[sl * i:sl * i + sl])
    while len(accs) > 1:
        accs = [op(accs[2 * i], accs[2 * i + 1]) for i in range(len(accs) // 2)]
    return accs[0]


def _attn_kernel(qT_ref, k_ref, vT_ref, o_ref, acc_ref, m_ref, l_ref,
                 *, n_latent_steps, bounded):
    t = pl.program_id(2)

    def keys(j0, nj):
        start = j0 * KV_SUB
        if not isinstance(start, int):
            start = pl.multiple_of(start, KV_SUB)
        return k_ref[0, pl.ds(start, nj * KV_SUB), :]

    def scores(kc, r):
        qT = qT_ref[0, r * HEAD_DIM:(r + 1) * HEAD_DIM, :]
        return jnp.dot(kc, qT, preferred_element_type=F32)

    def head_update(r, s, j0, nj, first):
        if bounded:
            p = jnp.exp2(s)
        else:
            cmax = jnp.max(_col_reduce(s, jnp.maximum), axis=0, keepdims=True)
            m_new = cmax if first else jnp.maximum(m_ref[r], cmax)
            p = jnp.exp2(s - m_new)
        lsum = jnp.sum(_col_reduce(p, jnp.add), axis=0, keepdims=True)
        pb = p.astype(BF16)
        pv = jnp.dot(vT_ref[0, 0, j0], pb[0:KV_SUB], preferred_element_type=F32)
        for jj in range(1, nj):
            pv += jnp.dot(vT_ref[0, 0, j0 + jj], pb[jj * KV_SUB:(jj + 1) * KV_SUB],
                          preferred_element_type=F32)
        if first:
            l_ref[r] = lsum
            acc_ref[r] = pv
        elif bounded:
            l_ref[r] += lsum
            acc_ref[r] += pv
        else:
            alpha = jnp.exp2(m_ref[r] - m_new)
            l_ref[r] = alpha * l_ref[r] + lsum
            acc_ref[r] = alpha * acc_ref[r] + pv
        if not bounded:
            m_ref[r] = m_new

    def run_chunks(chunks, first):
        s = scores(keys(*chunks[0]), 0)
        for ci, (j0, n) in enumerate(chunks):
            kc = keys(j0, n)
            for r in range(GROUP):
                if r + 1 < GROUP:
                    s_next = scores(kc, r + 1)
                elif ci + 1 < len(chunks):
                    s_next = scores(keys(*chunks[ci + 1]), 0)
                else:
                    s_next = None
                head_update(r, s, j0, n, first and ci == 0)
                s = s_next

    nj = KV_SUBS_PER_STEP
    if bounded:
        @pl.when(t == 0)
        def _context_queries():
            run_chunks([(0, 1)], True)

        @pl.when(t > 0)
        def _latent_queries():
            run_chunks([(0, 1 + nj)] + [(1 + u * nj, nj) for u in range(1, n_latent_steps)], True)
    else:
        run_chunks([(0, 1)], True)

        @pl.when(t > 0)
        def _latent_keys():
            def body(i, carry):
                run_chunks([(1 + i * nj, nj)], False)
                return carry

            lax.fori_loop(0, n_latent_steps, body, 0)

    for r in range(GROUP):
        o = acc_ref[r] * (1.0 / l_ref[r])
        o_ref[0, :, r * HEAD_DIM:(r + 1) * HEAD_DIM] = o.T.astype(BF16)


def _attention(qT, k, vT, *, bounded):
    b, attn_w, t_all = qT.shape
    n_kv, nt = vT.shape[1], vT.shape[2]
    tl = TOKEN_TILE
    gw = GROUP * HEAD_DIM
    n_latent_steps = (nt - 1) // KV_SUBS_PER_STEP
    return pl.pallas_call(
        functools.partial(_attn_kernel, n_latent_steps=n_latent_steps, bounded=bounded),
        grid=(b, n_kv, nt),
        in_specs=[
            pl.BlockSpec((1, gw, tl), lambda bi, g, ti: (bi, g, ti)),
            pl.BlockSpec((1, t_all, HEAD_DIM), lambda bi, g, ti: (bi, 0, g)),
            pl.BlockSpec((1, 1, nt, HEAD_DIM, KV_SUB), lambda bi, g, ti: (bi, g, 0, 0, 0)),
        ],
        out_specs=pl.BlockSpec((1, tl, gw), lambda bi, g, ti: (bi, ti, g)),
        out_shape=jax.ShapeDtypeStruct((b, t_all, attn_w), BF16),
        scratch_shapes=[
            pltpu.VMEM((GROUP, HEAD_DIM, tl), F32),
            pltpu.VMEM((GROUP, 1, tl), F32),
            pltpu.VMEM((GROUP, 1, tl), F32),
        ],
        compiler_params=pltpu.CompilerParams(
            dimension_semantics=("arbitrary", "arbitrary", "arbitrary"),
            vmem_limit_bytes=VMEM_LIMIT_BYTES),
        name="attention_bounded" if bounded else "attention_online",
    )(qT, k, vT)


def _score_bounds(q_gain, k_gain):
    return (HEAD_DIM * Q_SCALE * jnp.max(jnp.abs(q_gain), axis=-1)
            * jnp.max(jnp.abs(k_gain), axis=-1))


def _attention_dispatch(qT, k, vT, score_bound):
    return lax.cond(score_bound < BOUNDED_SCORE_LIMIT,
                    functools.partial(_attention, bounded=True),
                    functools.partial(_attention, bounded=False),
                    qT, k, vT)


def _merge_kernel(oa_ref, ob_ref, szaa_ref, szab_ref, xba_ref, xbb_ref, xbp_ref, xbn_ref,
                  szba_ref, szbb_ref, sgaa_ref, sgab_ref, sgba_ref, sgbb_ref,
                  cx_ref, x_ref, mod_ref, wpool_ref, pscale_ref, wa_ref, wb_ref, wo_ref,
                  lng_ref, lnb_ref, cx_out_ref, x_out_ref, ext_ref,
                  *, d_model, n_tiles, ctx_len, seq_len, alpha_res):
    j = pl.program_id(1)
    tl = TOKEN_TILE
    h = POOL_HALO
    first = j == 0

    prev_ok = j >= 2
    next_ok = jnp.logical_and(j >= 1, 2 * j <= n_tiles - 2)
    ext_ref[0:h, :] = jnp.where(prev_ok, xbp_ref[0], 0.0)
    ext_ref[h:h + tl, :] = jnp.where(first, 0.0, xba_ref[0])
    ext_ref[h + tl:h + 2 * tl, :] = xbb_ref[0]
    ext_ref[h + 2 * tl:2 * h + 2 * tl, :] = jnp.where(next_ok, xbn_ref[0], 0.0)

    pos_a = jnp.maximum(2 * j - 2, 0) * tl
    pos_b = jnp.where(first, 0, (2 * j - 1) * tl)
    seq_b = jnp.where(first, ctx_len, seq_len)
    edge_row = lax.broadcasted_iota(jnp.int32, (h, POOL_GROUP), 0)

    def inv_count(pos, seq, a, bb):
        cnt = jnp.minimum(pos + bb, seq) - jnp.maximum(pos - a, 0)
        return 1.0 / cnt.astype(F32)

    def pooled(hi, gi, w):
        a = w // 2
        bb = w - a
        base = h + hi * tl
        lanes = slice(gi * POOL_GROUP, (gi + 1) * POOL_GROUP)
        pos, seq = (pos_a, seq_len) if hi == 0 else (pos_b, seq_b)
        acc = ext_ref[base - a:base - a + tl, lanes]
        for jj in range(-a + 1, bb):
            acc = acc + ext_ref[base + jj:base + jj + tl, lanes]
        mean = jnp.concatenate([
            acc[0:h] * inv_count(pos + edge_row, seq, a, bb),
            acc[h:tl - h] * (1.0 / w),
            acc[tl - h:tl] * inv_count(pos + (tl - h) + edge_row, seq, a, bb)], axis=0)
        return mean - ext_ref[base:base + tl, lanes]

    halves = ((oa_ref, szaa_ref, szba_ref, sgaa_ref, sgba_ref),
              (ob_ref, szab_ref, szbb_ref, sgab_ref, sgbb_ref))
    ya, yb = [], []
    for hi, (o_ref, sza_ref, szb_ref, _, _) in enumerate(halves):
        a_in = (o_ref[0].astype(F32) * sza_ref[0].astype(F32)).astype(BF16)
        ya.append(jnp.dot(a_in, wa_ref[...], preferred_element_type=F32))
        parts = [jnp.dot(pooled(hi, gi, w).astype(BF16), wpool_ref[gi],
                         preferred_element_type=F32) for gi, w in enumerate(POOL_WINDOWS)]
        p_pool = jnp.concatenate(parts, axis=-1) * pscale_ref[...]
        b_in = (p_pool * szb_ref[0].astype(F32)).astype(BF16)
        yb.append(jnp.dot(b_in, wb_ref[...], preferred_element_type=F32))
    yo = []
    for hi, (_, _, _, sga_ref, sgb_ref) in enumerate(halves):
        y = (sga_ref[0].astype(F32) * ya[hi] + sgb_ref[0].astype(F32) * yb[hi]).astype(BF16)
        yo.append(jnp.dot(y, wo_ref[...], preferred_element_type=F32))

    gate_ctx = mod_ref[0, 0][:, 2 * d_model:]
    gate_lat = mod_ref[0, 1][:, 2 * d_model:]
    x_in = (x_ref[0, 0:tl, :], jnp.where(first, cx_ref[0], x_ref[0, tl:2 * tl, :]))
    gates = (gate_lat, jnp.where(first, gate_ctx, gate_lat))
    for hi in range(2):
        r = alpha_res * x_in[hi] + gates[hi] * yo[hi]
        mu = jnp.mean(r, axis=-1, keepdims=True)
        rc = r - mu
        var = jnp.mean(rc * rc, axis=-1, keepdims=True)
        res = rc * lax.rsqrt(var + EPS) * lng_ref[...] + lnb_ref[...]
        x_out_ref[0, hi * tl:(hi + 1) * tl, :] = res
        if hi == 1:
            cx_out_ref[0] = res


def _merge(o, sza, xb, szb, sga, sgb, cx, x, mod, wpool, pscale, wa, wb, wo, lng, lnb,
           *, layer, ctx_len, alpha_res):
    b, seq_len, d = x.shape
    attn_w = o.shape[2]
    pool_w = xb.shape[2]
    tl = TOKEN_TILE
    nt = (tl + seq_len) // tl
    assert nt % 2 == 1
    n_steps = (nt + 1) // 2
    hb = tl // POOL_HALO
    n_hblocks = nt * hb
    tile_a = lambda bi, j: (bi, jnp.maximum(2 * j - 1, 0), 0)
    tile_b = lambda bi, j: (bi, 2 * j, 0)
    per_layer = lambda bi, j: (layer, 0, 0)

    def pair(width):
        return [pl.BlockSpec((1, tl, width), tile_a), pl.BlockSpec((1, tl, width), tile_b)]

    in_specs = (
        pair(attn_w) + pair(attn_w) + pair(pool_w) + [
            pl.BlockSpec((1, POOL_HALO, pool_w),
                         lambda bi, j: (bi, jnp.maximum((2 * j - 1) * hb - 1, 0), 0)),
            pl.BlockSpec((1, POOL_HALO, pool_w),
                         lambda bi, j: (bi, jnp.minimum((2 * j + 1) * hb, n_hblocks - 1), 0)),
        ] + pair(pool_w) + pair(d) + pair(d) + [
            pl.BlockSpec((1, tl, d), lambda bi, j: (bi, 0, 0)),
            pl.BlockSpec((1, 2 * tl, d), lambda bi, j: (bi, jnp.maximum(j - 1, 0), 0)),
            pl.BlockSpec((None, 1, 2, 1, 3 * d), lambda bi, j: (layer, bi, 0, 0, 0)),
            pl.BlockSpec((None,) + wpool.shape[1:], lambda bi, j: (layer, 0, 0, 0)),
            pl.BlockSpec((None, 1, pool_w), per_layer),
            pl.BlockSpec((None,) + wa.shape[1:], per_layer),
            pl.BlockSpec((None,) + wb.shape[1:], per_layer),
            pl.BlockSpec((None,) + wo.shape[1:], per_layer),
            pl.BlockSpec((None, 1, d), per_layer),
            pl.BlockSpec((None, 1, d), per_layer),
        ])
    return pl.pallas_call(
        functools.partial(_merge_kernel, d_model=d, n_tiles=nt, ctx_len=ctx_len,
                          seq_len=seq_len, alpha_res=alpha_res),
        grid=(b, n_steps),
        in_specs=in_specs,
        out_specs=(
            pl.BlockSpec((1, tl, d), lambda bi, j: (bi, jnp.minimum(j, 1), 0)),
            pl.BlockSpec((1, 2 * tl, d), lambda bi, j: (bi, jnp.maximum(j - 1, 0), 0)),
        ),
        out_shape=(
            jax.ShapeDtypeStruct((b, 2 * tl, d), F32),
            jax.ShapeDtypeStruct((b, seq_len, d), F32),
        ),
        scratch_shapes=[pltpu.VMEM((2 * tl + 2 * POOL_HALO, pool_w), F32)],
        compiler_params=pltpu.CompilerParams(
            dimension_semantics=("arbitrary", "arbitrary"), vmem_limit_bytes=VMEM_LIMIT_BYTES),
        name="merge",
    )(o, o, sza, sza, xb, xb, xb, xb, szb, szb, sga, sga, sgb, sgb, cx, x, mod,
      wpool, pscale, wa, wb, wo, lng, lnb)


def _rope_tables(seq_len, ctx_len):
    n_freq = ROT_HALF
    n_rows = seq_len // GRID_W
    inv_freq = 1.0 / (ROPE_THETA ** (jnp.arange(n_freq, dtype=F32) / n_freq))
    ar = jnp.arange(n_rows, dtype=jnp.int32).astype(F32)[:, None] * inv_freq[None, :]
    ac = jnp.arange(GRID_W, dtype=jnp.int32).astype(F32)[:, None] * inv_freq[None, :]

    def grid_table(row_part, col_part):
        rp = jnp.broadcast_to(jnp.concatenate(row_part, axis=-1)[:, None, :],
                              (n_rows, GRID_W, 2 * n_freq))
        cp = jnp.broadcast_to(jnp.concatenate(col_part, axis=-1)[None, :, :],
                              (n_rows, GRID_W, 2 * n_freq))
        return jnp.concatenate([rp, cp], axis=-1).reshape(seq_len, HEAD_DIM)

    cos_r, sin_r, cos_c, sin_c = jnp.cos(ar), jnp.sin(ar), jnp.cos(ac), jnp.sin(ac)
    zr, zc = jnp.zeros_like(sin_r), jnp.zeros_like(sin_c)
    cos = grid_table([cos_r, cos_r], [cos_c, cos_c])
    slo = grid_table([-sin_r, zr], [-sin_c, zc])
    shi = grid_table([zr, sin_r], [zc, sin_c])
    ones = jnp.ones((ctx_len, HEAD_DIM), F32)
    zeros = jnp.zeros((ctx_len, HEAD_DIM), F32)
    return (jnp.concatenate([ones, cos], axis=0),
            jnp.concatenate([zeros, slo], axis=0),
            jnp.concatenate([zeros, shi], axis=0))


def kernel(x, c, ctx, c_ctx, w_mod, b_mod, w_in, q_norm, k_norm, w_pool, pool_scale,
           w_br_a, w_br_b, w_out, ln_g, ln_b):
    b, seq_len, d = x.shape
    ctx_len = ctx.shape[1]
    depth = w_in.shape[0]
    attn_w = w_br_a.shape[1]
    pool_w = w_br_b.shape[1]
    kv_w = (w_in.shape[2] - 2 * attn_w - 2 * pool_w - 2 * d) // 2
    assert ctx_len == TOKEN_TILE and seq_len % (TOKEN_TILE * KV_SUBS_PER_STEP) == 0
    assert attn_w // HEAD_DIM == GROUP * N_KV_HEADS and kv_w == N_KV_HEADS * HEAD_DIM
    assert seq_len % GRID_W == 0 and 1 + b <= SUBLANES
    alpha_res = (2 * depth) ** 0.25
    cx = ctx

    c_rows = jnp.zeros((SUBLANES, d), F32).at[0].set(c_ctx).at[1:1 + b].set(c)
    mod = _modulation(c_rows, w_mod, b_mod)

    cos_t, slo_t, shi_t = _rope_tables(seq_len, ctx_len)

    mod_ctx = jnp.broadcast_to(mod[:, 0:1], (depth, b, 3 * d))
    mod = jnp.stack([mod_ctx, mod[:, 1:1 + b]], axis=2).reshape(depth, b, 2, 1, 3 * d)

    w_in_b = w_in.astype(BF16)
    w_pool_b = w_pool.astype(BF16)
    w_a_b = w_br_a.astype(BF16)
    w_b_b = w_br_b.astype(BF16)
    w_o_b = w_out.astype(BF16)
    qg = q_norm.reshape(depth, 1, HEAD_DIM)
    kg = k_norm.reshape(depth, 1, HEAD_DIM)
    pscale = pool_scale.reshape(depth, 1, pool_w)
    lng = ln_g.reshape(depth, 1, d)
    lnb = ln_b.reshape(depth, 1, d)
    score_bounds = _score_bounds(q_norm, k_norm)

    for l in range(depth):
        qT, k, vT, sza, xb, szb, sga, sgb = _project(
            cx, x, mod, w_in_b, qg, kg, cos_t, slo_t, shi_t,
            layer=l, attn_w=attn_w, kv_w=kv_w, pool_w=pool_w)
        o = _attention_dispatch(qT, k, vT, score_bounds[l])
        cx, x = _merge(o, sza, xb, szb, sga, sgb, cx, x, mod, w_pool_b, pscale,
                       w_a_b, w_b_b, w_o_b, lng, lnb,
                       layer=l, ctx_len=ctx_len, alpha_res=alpha_res)
    return x
```

```python
import functools
import math

import jax
import jax.numpy as jnp
from jax import lax
from jax.experimental import pallas as pl
from jax.experimental.pallas import tpu as pltpu

F32 = jnp.float32
BF16 = jnp.bfloat16
FP8 = jnp.float8_e4m3fn

SUBLANES = 8

HEAD_DIM = 128
N_KV_HEADS = 2
GROUP = 4
GRID_W = 64
ROT_HALF = HEAD_DIM // 4
POOL_WINDOWS = (2, 4, 8, 16)
POOL_GROUP = 128
POOL_HALO = max(POOL_WINDOWS) // 2
ROPE_THETA = 10000.0
EPS = 1e-6

TOKEN_TILE = 256
KV_SUB = 256
KV_SUBS_PER_STEP = 4
VMEM_LIMIT_BYTES = 56 * 1024 * 1024

Q_SCALE = HEAD_DIM ** -0.5 * math.log2(math.e)
REDUCE_CHAINS = 4
BOUNDED_SCORE_LIMIT = 24.0


def _sigmoid(v):
    return 1.0 / (1.0 + jnp.exp(-v))


def _mod_kernel(c_ref, w_ref, b_ref, o_ref):
    cv = c_ref[...]
    s = cv * _sigmoid(cv)
    o_ref[0] = jnp.dot(s, w_ref[0], preferred_element_type=F32) + b_ref[0]


def _modulation(c_rows, w_mod, b_mod):
    depth, d, d3 = w_mod.shape
    rows = c_rows.shape[0]
    return pl.pallas_call(
        _mod_kernel,
        grid=(depth,),
        in_specs=[
            pl.BlockSpec((rows, d), lambda l: (0, 0)),
            pl.BlockSpec((1, d, d3), lambda l: (l, 0, 0)),
            pl.BlockSpec((1, 1, d3), lambda l: (l, 0, 0)),
        ],
        out_specs=pl.BlockSpec((1, rows, d3), lambda l: (l, 0, 0)),
        out_shape=jax.ShapeDtypeStruct((depth, rows, d3), F32),
        compiler_params=pltpu.CompilerParams(
            dimension_semantics=("arbitrary",), vmem_limit_bytes=VMEM_LIMIT_BYTES),
        name="modulation",
    )(c_rows, w_mod, b_mod.reshape(depth, 1, d3))


def _proj_kernel(cx_ref, x_ref, mod_ref, w_ref, qg_ref, kg_ref, cos_ref, slo_ref, shi_ref,
                 qT_ref, q8T_ref, k_ref, k8_ref, vT_ref, sza_ref, xb_ref, szb_ref, sga_ref, sgb_ref,
                 *, d_model, attn_w, kv_w, pool_w):
    x = jnp.where(pl.program_id(1) == 0, cx_ref[0], x_ref[0])
    mod = mod_ref[0, 0]
    shift = mod[:, :d_model]
    scale = mod[:, d_model:2 * d_model]
    u = (x * (1.0 + scale) + shift).astype(BF16)

    cos = cos_ref[...]
    slo = slo_ref[...]
    shi = shi_ref[...]

    def norm_rope(h, g):
        ms = jnp.mean(h * h, axis=-1, keepdims=True)
        y = h * lax.rsqrt(ms + EPS) * g
        return (y * cos + pltpu.roll(y, HEAD_DIM - ROT_HALF, 1) * slo
                + pltpu.roll(y, ROT_HALF, 1) * shi)

    o = 0
    hq = jnp.dot(u, w_ref[:, o:o + attn_w], preferred_element_type=F32)
    o += attn_w
    qg = qg_ref[...]
    for h in range(attn_w // HEAD_DIM):
        qh = norm_rope(hq[:, h * HEAD_DIM:(h + 1) * HEAD_DIM], qg) * Q_SCALE
        qhT = qh.T
        qT_ref[0, h * HEAD_DIM:(h + 1) * HEAD_DIM, :] = qhT.astype(BF16)
        q8T_ref[0, h * HEAD_DIM:(h + 1) * HEAD_DIM, :] = qhT.astype(FP8)

    hkv = jnp.dot(u, w_ref[:, o:o + 2 * kv_w], preferred_element_type=F32)
    o += 2 * kv_w
    kg = kg_ref[...]
    for g in range(kv_w // HEAD_DIM):
        kh = norm_rope(hkv[:, g * HEAD_DIM:(g + 1) * HEAD_DIM], kg)
        k_ref[0, :, g * HEAD_DIM:(g + 1) * HEAD_DIM] = kh.astype(BF16)
        k8_ref[0, :, g * HEAD_DIM:(g + 1) * HEAD_DIM] = kh.astype(FP8)
        vh = hkv[:, kv_w + g * HEAD_DIM:kv_w + (g + 1) * HEAD_DIM]
        vT_ref[0, g, 0] = vh.T.astype(BF16)

    za = jnp.dot(u, w_ref[:, o:o + attn_w], preferred_element_type=F32)
    o += attn_w
    sza_ref[0] = (za * _sigmoid(za)).astype(BF16)

    xb_ref[0] = jnp.dot(u, w_ref[:, o:o + pool_w], preferred_element_type=F32)
    o += pool_w

    zb = jnp.dot(u, w_ref[:, o:o + pool_w], preferred_element_type=F32)
    o += pool_w
    szb_ref[0] = (zb * _sigmoid(zb)).astype(BF16)

    ga = jnp.dot(u, w_ref[:, o:o + d_model], preferred_element_type=F32)
    o += d_model
    sga_ref[0] = _sigmoid(ga).astype(BF16)

    gb = jnp.dot(u, w_ref[:, o:o + d_model], preferred_element_type=F32)
    sgb_ref[0] = _sigmoid(gb).astype(BF16)


def _project(cx, x, mod, w_in, qg, kg, cos_t, slo_t, shi_t, *, layer, attn_w, kv_w, pool_w):
    b, seq_len, d = x.shape
    n_in = w_in.shape[2]
    tl = TOKEN_TILE
    t_all = tl + seq_len
    nt = t_all // tl
    n_kv = kv_w // HEAD_DIM
    row = lambda bi, ti: (bi, ti, 0)
    latent_row = lambda bi, ti: (bi, jnp.maximum(ti - 1, 0), 0)
    out_shape = (
        jax.ShapeDtypeStruct((b, attn_w, t_all), BF16),
        jax.ShapeDtypeStruct((b, attn_w, t_all), FP8),
        jax.ShapeDtypeStruct((b, t_all, kv_w), BF16),
        jax.ShapeDtypeStruct((b, t_all, kv_w), FP8),
        jax.ShapeDtypeStruct((b, n_kv, nt, HEAD_DIM, tl), BF16),
        jax.ShapeDtypeStruct((b, t_all, attn_w), BF16),
        jax.ShapeDtypeStruct((b, t_all, pool_w), F32),
        jax.ShapeDtypeStruct((b, t_all, pool_w), BF16),
        jax.ShapeDtypeStruct((b, t_all, d), BF16),
        jax.ShapeDtypeStruct((b, t_all, d), BF16),
    )
    out_specs = (
        pl.BlockSpec((1, attn_w, tl), lambda bi, ti: (bi, 0, ti)),
        pl.BlockSpec((1, attn_w, tl), lambda bi, ti: (bi, 0, ti)),
        pl.BlockSpec((1, tl, kv_w), row),
        pl.BlockSpec((1, tl, kv_w), row),
        pl.BlockSpec((1, n_kv, 1, HEAD_DIM, tl), lambda bi, ti: (bi, 0, ti, 0, 0)),
        pl.BlockSpec((1, tl, attn_w), row),
        pl.BlockSpec((1, tl, pool_w), row),
        pl.BlockSpec((1, tl, pool_w), row),
        pl.BlockSpec((1, tl, d), row),
        pl.BlockSpec((1, tl, d), row),
    )
    in_specs = [
        pl.BlockSpec((1, tl, d), lambda bi, ti: (bi, 0, 0)),
        pl.BlockSpec((1, tl, d), latent_row),
        pl.BlockSpec((None, 1, 1, 1, 3 * d),
                     lambda bi, ti: (layer, bi, jnp.minimum(ti, 1), 0, 0)),
        pl.BlockSpec((None, d, n_in), lambda bi, ti: (layer, 0, 0)),
        pl.BlockSpec((None, 1, HEAD_DIM), lambda bi, ti: (layer, 0, 0)),
        pl.BlockSpec((None, 1, HEAD_DIM), lambda bi, ti: (layer, 0, 0)),
        pl.BlockSpec((tl, HEAD_DIM), lambda bi, ti: (ti, 0)),
        pl.BlockSpec((tl, HEAD_DIM), lambda bi, ti: (ti, 0)),
        pl.BlockSpec((tl, HEAD_DIM), lambda bi, ti: (ti, 0)),
    ]
    return pl.pallas_call(
        functools.partial(_proj_kernel, d_model=d, attn_w=attn_w, kv_w=kv_w, pool_w=pool_w),
        grid=(b, nt),
        in_specs=in_specs,
        out_specs=out_specs,
        out_shape=out_shape,
        compiler_params=pltpu.CompilerParams(
            dimension_semantics=("arbitrary", "arbitrary"), vmem_limit_bytes=VMEM_LIMIT_BYTES),
        name="projection",
    )(cx, x, mod, w_in, qg, kg, cos_t, slo_t, shi_t)


def _col_reduce(x, op):
    sl = SUBLANES
    nblk = x.shape[0] // sl
    accs = [x[sl * j:sl * j + sl] for j in range(min(REDUCE_CHAINS, nblk))]
    for i in range(len(accs), nblk):
        accs[i % REDUCE_CHAINS] = op(accs[i % REDUCE_CHAINS], x[sl * i:sl * i + sl])
    while len(accs) > 1:
        accs = [op(accs[2 * i], accs[2 * i + 1]) for i in range(len(accs) // 2)]
    return accs[0]


def _attn_kernel(qT_ref, k_ref, vT_ref, o_ref, acc_ref, m_ref, l_ref,
                 *, n_latent_steps, bounded):
    t = pl.program_id(2)

    def keys(j0, nj):
        start = j0 * KV_SUB
        if not isinstance(start, int):
            start = pl.multiple_of(start, KV_SUB)
        return k_ref[0, pl.ds(start, nj * KV_SUB), :]

    def scores(kc, r):
        qT = qT_ref[0, r * HEAD_DIM:(r + 1) * HEAD_DIM, :]
        return jnp.dot(kc, qT, preferred_element_type=F32)

    def head_update(r, s, j0, nj, first):
        if bounded:
            p = jnp.exp2(s)
        else:
            cmax = jnp.max(_col_reduce(s, jnp.maximum), axis=0, keepdims=True)
            m_new = cmax if first else jnp.maximum(m_ref[r], cmax)
            p = jnp.exp2(s - m_new)
        lsum = jnp.sum(_col_reduce(p, jnp.add), axis=0, keepdims=True)
        pb = p.astype(BF16)
        pv = jnp.dot(vT_ref[0, 0, j0], pb[0:KV_SUB], preferred_element_type=F32)
        for jj in range(1, nj):
            pv += jnp.dot(vT_ref[0, 0, j0 + jj], pb[jj * KV_SUB:(jj + 1) * KV_SUB],
                          preferred_element_type=F32)
        if first:
            l_ref[r] = lsum
            acc_ref[r] = pv
        elif bounded:
            l_ref[r] += lsum
            acc_ref[r] += pv
        else:
            alpha = jnp.exp2(m_ref[r] - m_new)
            l_ref[r] = alpha * l_ref[r] + lsum
            acc_ref[r] = alpha * acc_ref[r] + pv
        if not bounded:
            m_ref[r] = m_new

    def run_chunks(chunks, first):
        s = scores(keys(*chunks[0]), 0)
        for ci, (j0, n) in enumerate(chunks):
            kc = keys(j0, n)
            for r in range(GROUP):
                if r + 1 < GROUP:
                    s_next = scores(kc, r + 1)
                elif ci + 1 < len(chunks):
                    s_next = scores(keys(*chunks[ci + 1]), 0)
                else:
                    s_next = None
                head_update(r, s, j0, n, first and ci == 0)
                s = s_next

    nj = KV_SUBS_PER_STEP
    if bounded:
        @pl.when(t == 0)
        def _context_queries():
            run_chunks([(0, 1)], True)

        @pl.when(t > 0)
        def _latent_queries():
            run_chunks([(0, 1 + nj)] + [(1 + u * nj, nj) for u in range(1, n_latent_steps)], True)
    else:
        run_chunks([(0, 1)], True)

        @pl.when(t > 0)
        def _latent_keys():
            def body(i, carry):
                run_chunks([(1 + i * nj, nj)], False)
                return carry

            lax.fori_loop(0, n_latent_steps, body, 0)

    for r in range(GROUP):
        o = acc_ref[r] * (1.0 / l_ref[r])
        o_ref[0, :, r * HEAD_DIM:(r + 1) * HEAD_DIM] = o.T.astype(BF16)


def _attention(qT, k, vT, *, bounded):
    b, attn_w, t_all = qT.shape
    n_kv, nt = vT.shape[1], vT.shape[2]
    tl = TOKEN_TILE
    gw = GROUP * HEAD_DIM
    n_latent_steps = (nt - 1) // KV_SUBS_PER_STEP
    return pl.pallas_call(
        functools.partial(_attn_kernel, n_latent_steps=n_latent_steps, bounded=bounded),
        grid=(b, n_kv, nt),
        in_specs=[
            pl.BlockSpec((1, gw, tl), lambda bi, g, ti: (bi, g, ti)),
            pl.BlockSpec((1, t_all, HEAD_DIM), lambda bi, g, ti: (bi, 0, g)),
            pl.BlockSpec((1, 1, nt, HEAD_DIM, KV_SUB), lambda bi, g, ti: (bi, g, 0, 0, 0)),
        ],
        out_specs=pl.BlockSpec((1, tl, gw), lambda bi, g, ti: (bi, ti, g)),
        out_shape=jax.ShapeDtypeStruct((b, t_all, attn_w), BF16),
        scratch_shapes=[
            pltpu.VMEM((GROUP, HEAD_DIM, tl), F32),
            pltpu.VMEM((GROUP, 1, tl), F32),
            pltpu.VMEM((GROUP, 1, tl), F32),
        ],
        compiler_params=pltpu.CompilerParams(
            dimension_semantics=("arbitrary", "arbitrary", "arbitrary"),
            vmem_limit_bytes=VMEM_LIMIT_BYTES),
        name="attention_bounded" if bounded else "attention_online",
    )(qT, k, vT)


def _score_bounds(q_gain, k_gain):
    return (HEAD_DIM * Q_SCALE * jnp.max(jnp.abs(q_gain), axis=-1)
            * jnp.max(jnp.abs(k_gain), axis=-1))


def _attention_dispatch(qT, q8T, k, k8, vT, score_bound):
    return lax.cond(score_bound < BOUNDED_SCORE_LIMIT,
                    lambda: _attention(q8T, k8, vT, bounded=True),
                    lambda: _attention(qT, k, vT, bounded=False))


def _merge_kernel(oa_ref, ob_ref, szaa_ref, szab_ref, xba_ref, xbb_ref, xbp_ref, xbn_ref,
                  szba_ref, szbb_ref, sgaa_ref, sgab_ref, sgba_ref, sgbb_ref,
                  cx_ref, x_ref, mod_ref, wpool_ref, pscale_ref, wa_ref, wb_ref, wo_ref,
                  lng_ref, lnb_ref, cx_out_ref, x_out_ref, ext_ref,
                  *, d_model, n_tiles, ctx_len, seq_len, alpha_res):
    j = pl.program_id(1)
    tl = TOKEN_TILE
    h = POOL_HALO
    first = j == 0

    prev_ok = j >= 2
    next_ok = jnp.logical_and(j >= 1, 2 * j <= n_tiles - 2)
    ext_ref[0:h, :] = jnp.where(prev_ok, xbp_ref[0], 0.0)
    ext_ref[h:h + tl, :] = jnp.where(first, 0.0, xba_ref[0])
    ext_ref[h + tl:h + 2 * tl, :] = xbb_ref[0]
    ext_ref[h + 2 * tl:2 * h + 2 * tl, :] = jnp.where(next_ok, xbn_ref[0], 0.0)

    pos_a = jnp.maximum(2 * j - 2, 0) * tl
    pos_b = jnp.where(first, 0, (2 * j - 1) * tl)
    seq_b = jnp.where(first, ctx_len, seq_len)
    edge_row = lax.broadcasted_iota(jnp.int32, (h, POOL_GROUP), 0)

    def inv_count(pos, seq, a, bb):
        cnt = jnp.minimum(pos + bb, seq) - jnp.maximum(pos - a, 0)
        return 1.0 / cnt.astype(F32)

    def pooled(hi, gi, w):
        a = w // 2
        bb = w - a
        base = h + hi * tl
        lanes = slice(gi * POOL_GROUP, (gi + 1) * POOL_GROUP)
        pos, seq = (pos_a, seq_len) if hi == 0 else (pos_b, seq_b)
        acc = ext_ref[base - a:base - a + tl, lanes]
        for jj in range(-a + 1, bb):
            acc = acc + ext_ref[base + jj:base + jj + tl, lanes]
        mean = jnp.concatenate([
            acc[0:h] * inv_count(pos + edge_row, seq, a, bb),
            acc[h:tl - h] * (1.0 / w),
            acc[tl - h:tl] * inv_count(pos + (tl - h) + edge_row, seq, a, bb)], axis=0)
        return mean - ext_ref[base:base + tl, lanes]

    halves = ((oa_ref, szaa_ref, szba_ref, sgaa_ref, sgba_ref),
              (ob_ref, szab_ref, szbb_ref, sgab_ref, sgbb_ref))
    ya, yb = [], []
    for hi, (o_ref, sza_ref, szb_ref, _, _) in enumerate(halves):
        a_in = (o_ref[0].astype(F32) * sza_ref[0].astype(F32)).astype(BF16)
        ya.append(jnp.dot(a_in, wa_ref[...], preferred_element_type=F32))
        parts = [jnp.dot(pooled(hi, gi, w).astype(BF16), wpool_ref[gi],
                         preferred_element_type=F32) for gi, w in enumerate(POOL_WINDOWS)]
        p_pool = jnp.concatenate(parts, axis=-1) * pscale_ref[...]
        b_in = (p_pool * szb_ref[0].astype(F32)).astype(BF16)
        yb.append(jnp.dot(b_in, wb_ref[...], preferred_element_type=F32))
    yo = []
    for hi, (_, _, _, sga_ref, sgb_ref) in enumerate(halves):
        y = (sga_ref[0].astype(F32) * ya[hi] + sgb_ref[0].astype(F32) * yb[hi]).astype(BF16)
        yo.append(jnp.dot(y, wo_ref[...], preferred_element_type=F32))

    gate_ctx = mod_ref[0, 0][:, 2 * d_model:]
    gate_lat = mod_ref[0, 1][:, 2 * d_model:]
    x_in = (x_ref[0, 0:tl, :], jnp.where(first, cx_ref[0], x_ref[0, tl:2 * tl, :]))
    gates = (gate_lat, jnp.where(first, gate_ctx, gate_lat))
    for hi in range(2):
        r = alpha_res * x_in[hi] + gates[hi] * yo[hi]
        mu = jnp.mean(r, axis=-1, keepdims=True)
        rc = r - mu
        var = jnp.mean(rc * rc, axis=-1, keepdims=True)
        res = rc * lax.rsqrt(var + EPS) * lng_ref[...] + lnb_ref[...]
        x_out_ref[0, hi * tl:(hi + 1) * tl, :] = res
        if hi == 1:
            cx_out_ref[0] = res


def _merge(o, sza, xb, szb, sga, sgb, cx, x, mod, wpool, pscale, wa, wb, wo, lng, lnb,
           *, layer, ctx_len, alpha_res):
    b, seq_len, d = x.shape
    attn_w = o.shape[2]
    pool_w = xb.shape[2]
    tl = TOKEN_TILE
    nt = (tl + seq_len) // tl
    assert nt % 2 == 1
    n_steps = (nt + 1) // 2
    hb = tl // POOL_HALO
    n_hblocks = nt * hb
    tile_a = lambda bi, j: (bi, jnp.maximum(2 * j - 1, 0), 0)
    tile_b = lambda bi, j: (bi, 2 * j, 0)
    per_layer = lambda bi, j: (layer, 0, 0)

    def pair(width):
        return [pl.BlockSpec((1, tl, width), tile_a), pl.BlockSpec((1, tl, width), tile_b)]

    in_specs = (
        pair(attn_w) + pair(attn_w) + pair(pool_w) + [
            pl.BlockSpec((1, POOL_HALO, pool_w),
                         lambda bi, j: (bi, jnp.maximum((2 * j - 1) * hb - 1, 0), 0)),
            pl.BlockSpec((1, POOL_HALO, pool_w),
                         lambda bi, j: (bi, jnp.minimum((2 * j + 1) * hb, n_hblocks - 1), 0)),
        ] + pair(pool_w) + pair(d) + pair(d) + [
            pl.BlockSpec((1, tl, d), lambda bi, j: (bi, 0, 0)),
            pl.BlockSpec((1, 2 * tl, d), lambda bi, j: (bi, jnp.maximum(j - 1, 0), 0)),
            pl.BlockSpec((None, 1, 2, 1, 3 * d), lambda bi, j: (layer, bi, 0, 0, 0)),
            pl.BlockSpec((None,) + wpool.shape[1:], lambda bi, j: (layer, 0, 0, 0)),
            pl.BlockSpec((None, 1, pool_w), per_layer),
            pl.BlockSpec((None,) + wa.shape[1:], per_layer),
            pl.BlockSpec((None,) + wb.shape[1:], per_layer),
            pl.BlockSpec((None,) + wo.shape[1:], per_layer),
            pl.BlockSpec((None, 1, d), per_layer),
            pl.BlockSpec((None, 1, d), per_layer),
        ])
    return pl.pallas_call(
        functools.partial(_merge_kernel, d_model=d, n_tiles=nt, ctx_len=ctx_len,
                          seq_len=seq_len, alpha_res=alpha_res),
        grid=(b, n_steps),
        in_specs=in_specs,
        out_specs=(
            pl.BlockSpec((1, tl, d), lambda bi, j: (bi, jnp.minimum(j, 1), 0)),
            pl.BlockSpec((1, 2 * tl, d), lambda bi, j: (bi, jnp.maximum(j - 1, 0), 0)),
        ),
        out_shape=(
            jax.ShapeDtypeStruct((b, 2 * tl, d), F32),
            jax.ShapeDtypeStruct((b, seq_len, d), F32),
        ),
        scratch_shapes=[pltpu.VMEM((2 * tl + 2 * POOL_HALO, pool_w), F32)],
        compiler_params=pltpu.CompilerParams(
            dimension_semantics=("arbitrary", "arbitrary"), vmem_limit_bytes=VMEM_LIMIT_BYTES),
        name="merge",
    )(o, o, sza, sza, xb, xb, xb, xb, szb, szb, sga, sga, sgb, sgb, cx, x, mod,
      wpool, pscale, wa, wb, wo, lng, lnb)


def _rope_tables(seq_len, ctx_len):
    n_freq = ROT_HALF
    n_rows = seq_len // GRID_W
    inv_freq = 1.0 / (ROPE_THETA ** (jnp.arange(n_freq, dtype=F32) / n_freq))
    ar = jnp.arange(n_rows, dtype=jnp.int32).astype(F32)[:, None] * inv_freq[None, :]
    ac = jnp.arange(GRID_W, dtype=jnp.int32).astype(F32)[:, None] * inv_freq[None, :]

    def grid_table(row_part, col_part):
        rp = jnp.broadcast_to(jnp.concatenate(row_part, axis=-1)[:, None, :],
                              (n_rows, GRID_W, 2 * n_freq))
        cp = jnp.broadcast_to(jnp.concatenate(col_part, axis=-1)[None, :, :],
                              (n_rows, GRID_W, 2 * n_freq))
        return jnp.concatenate([rp, cp], axis=-1).reshape(seq_len, HEAD_DIM)

    cos_r, sin_r, cos_c, sin_c = jnp.cos(ar), jnp.sin(ar), jnp.cos(ac), jnp.sin(ac)
    zr, zc = jnp.zeros_like(sin_r), jnp.zeros_like(sin_c)
    cos = grid_table([cos_r, cos_r], [cos_c, cos_c])
    slo = grid_table([-sin_r, zr], [-sin_c, zc])
    shi = grid_table([zr, sin_r], [zc, sin_c])
    ones = jnp.ones((ctx_len, HEAD_DIM), F32)
    zeros = jnp.zeros((ctx_len, HEAD_DIM), F32)
    return (jnp.concatenate([ones, cos], axis=0),
            jnp.concatenate([zeros, slo], axis=0),
            jnp.concatenate([zeros, shi], axis=0))


def kernel(x, c, ctx, c_ctx, w_mod, b_mod, w_in, q_norm, k_norm, w_pool, pool_scale,
           w_br_a, w_br_b, w_out, ln_g, ln_b):
    b, seq_len, d = x.shape
    ctx_len = ctx.shape[1]
    depth = w_in.shape[0]
    attn_w = w_br_a.shape[1]
    pool_w = w_br_b.shape[1]
    kv_w = (w_in.shape[2] - 2 * attn_w - 2 * pool_w - 2 * d) // 2
    assert ctx_len == TOKEN_TILE and seq_len % (TOKEN_TILE * KV_SUBS_PER_STEP) == 0
    assert attn_w // HEAD_DIM == GROUP * N_KV_HEADS and kv_w == N_KV_HEADS * HEAD_DIM
    assert seq_len % GRID_W == 0 and 1 + b <= SUBLANES
    alpha_res = (2 * depth) ** 0.25
    cx = ctx

    c_rows = jnp.zeros((SUBLANES, d), F32).at[0].set(c_ctx).at[1:1 + b].set(c)
    mod = _modulation(c_rows, w_mod, b_mod)

    cos_t, slo_t, shi_t = _rope_tables(seq_len, ctx_len)

    mod_ctx = jnp.broadcast_to(mod[:, 0:1], (depth, b, 3 * d))
    mod = jnp.stack([mod_ctx, mod[:, 1:1 + b]], axis=2).reshape(depth, b, 2, 1, 3 * d)

    w_in_b = w_in.astype(BF16)
    w_pool_b = w_pool.astype(BF16)
    w_a_b = w_br_a.astype(BF16)
    w_b_b = w_br_b.astype(BF16)
    w_o_b = w_out.astype(BF16)
    qg = q_norm.reshape(depth, 1, HEAD_DIM)
    kg = k_norm.reshape(depth, 1, HEAD_DIM)
    pscale = pool_scale.reshape(depth, 1, pool_w)
    lng = ln_g.reshape(depth, 1, d)
    lnb = ln_b.reshape(depth, 1, d)
    score_bounds = _score_bounds(q_norm, k_norm)

    for l in range(depth):
        qT, q8T, k, k8, vT, sza, xb, szb, sga, sgb = _project(
            cx, x, mod, w_in_b, qg, kg, cos_t, slo_t, shi_t,
            layer=l, attn_w=attn_w, kv_w=kv_w, pool_w=pool_w)
        o = _attention_dispatch(qT, q8T, k, k8, vT, score_bounds[l])
        cx, x = _merge(o, sza, xb, szb, sga, sgb, cx, x, mod, w_pool_b, pscale,
                       w_a_b, w_b_b, w_o_b, lng, lnb,
                       layer=l, ctx_len=ctx_len, alpha_res=alpha_res)
    return x
```

```python
import functools
import math

import jax
import jax.numpy as jnp
from jax import lax
from jax.experimental import pallas as pl
from jax.experimental.pallas import tpu as pltpu

F32 = jnp.float32
BF16 = jnp.bfloat16
FP8 = jnp.float8_e4m3fn

SUBLANES = 8

HEAD_DIM = 128
N_KV_HEADS = 2
GROUP = 4
GRID_W = 64
ROT_HALF = HEAD_DIM // 4
POOL_WINDOWS = (2, 4, 8, 16)
POOL_GROUP = 128
POOL_HALO = max(POOL_WINDOWS) // 2
ROPE_THETA = 10000.0
EPS = 1e-6

TOKEN_TILE = 256
KV_SUB = 256
KV_SUBS_PER_STEP = 4
VMEM_LIMIT_BYTES = 56 * 1024 * 1024

Q_SCALE = HEAD_DIM ** -0.5 * math.log2(math.e)
REDUCE_CHAINS = 4
BOUNDED_SCORE_LIMIT = 24.0
FP8_GAIN_LIMIT = 4.0


def _sigmoid(v):
    return 1.0 / (1.0 + jnp.exp(-v))


def _mod_kernel(c_ref, w_ref, b_ref, o_ref):
    cv = c_ref[...]
    s = cv * _sigmoid(cv)
    o_ref[0] = jnp.dot(s, w_ref[0], preferred_element_type=F32) + b_ref[0]


def _modulation(c_rows, w_mod, b_mod):
    depth, d, d3 = w_mod.shape
    rows = c_rows.shape[0]
    return pl.pallas_call(
        _mod_kernel,
        grid=(depth,),
        in_specs=[
            pl.BlockSpec((rows, d), lambda l: (0, 0)),
            pl.BlockSpec((1, d, d3), lambda l: (l, 0, 0)),
            pl.BlockSpec((1, 1, d3), lambda l: (l, 0, 0)),
        ],
        out_specs=pl.BlockSpec((1, rows, d3), lambda l: (l, 0, 0)),
        out_shape=jax.ShapeDtypeStruct((depth, rows, d3), F32),
        compiler_params=pltpu.CompilerParams(
            dimension_semantics=("arbitrary",), vmem_limit_bytes=VMEM_LIMIT_BYTES),
        name="modulation",
    )(c_rows, w_mod, b_mod.reshape(depth, 1, d3))


def _proj_kernel(cx_ref, x_ref, mod_ref, w_ref, qg_ref, kg_ref, cos_ref, slo_ref, shi_ref,
                 qT_ref, q8T_ref, k_ref, k8_ref, vT_ref, sza_ref, xb_ref, szb_ref, sga_ref, sgb_ref,
                 *, d_model, attn_w, kv_w, pool_w):
    x = jnp.where(pl.program_id(1) == 0, cx_ref[0], x_ref[0])
    mod = mod_ref[0, 0]
    shift = mod[:, :d_model]
    scale = mod[:, d_model:2 * d_model]
    u = (x * (1.0 + scale) + shift).astype(BF16)

    cos = cos_ref[...]
    slo = slo_ref[...]
    shi = shi_ref[...]

    def norm_rope(h, g):
        ms = jnp.mean(h * h, axis=-1, keepdims=True)
        y = h * lax.rsqrt(ms + EPS) * g
        return (y * cos + pltpu.roll(y, HEAD_DIM - ROT_HALF, 1) * slo
                + pltpu.roll(y, ROT_HALF, 1) * shi)

    o = 0
    hq = jnp.dot(u, w_ref[:, o:o + attn_w], preferred_element_type=F32)
    o += attn_w
    qg = qg_ref[...]
    for h in range(attn_w // HEAD_DIM):
        qh = norm_rope(hq[:, h * HEAD_DIM:(h + 1) * HEAD_DIM], qg) * Q_SCALE
        qhT = qh.T
        qT_ref[0, h * HEAD_DIM:(h + 1) * HEAD_DIM, :] = qhT.astype(BF16)
        q8T_ref[0, h * HEAD_DIM:(h + 1) * HEAD_DIM, :] = qhT.astype(FP8)

    hkv = jnp.dot(u, w_ref[:, o:o + 2 * kv_w], preferred_element_type=F32)
    o += 2 * kv_w
    kg = kg_ref[...]
    for g in range(kv_w // HEAD_DIM):
        kh = norm_rope(hkv[:, g * HEAD_DIM:(g + 1) * HEAD_DIM], kg)
        k_ref[0, :, g * HEAD_DIM:(g + 1) * HEAD_DIM] = kh.astype(BF16)
        k8_ref[0, :, g * HEAD_DIM:(g + 1) * HEAD_DIM] = kh.astype(FP8)
        vh = hkv[:, kv_w + g * HEAD_DIM:kv_w + (g + 1) * HEAD_DIM]
        vT_ref[0, g, 0] = vh.T.astype(BF16)

    za = jnp.dot(u, w_ref[:, o:o + attn_w], preferred_element_type=F32)
    o += attn_w
    sza_ref[0] = (za * _sigmoid(za)).astype(BF16)

    xb_ref[0] = jnp.dot(u, w_ref[:, o:o + pool_w], preferred_element_type=F32)
    o += pool_w

    zb = jnp.dot(u, w_ref[:, o:o + pool_w], preferred_element_type=F32)
    o += pool_w
    szb_ref[0] = (zb * _sigmoid(zb)).astype(BF16)

    ga = jnp.dot(u, w_ref[:, o:o + d_model], preferred_element_type=F32)
    o += d_model
    sga_ref[0] = _sigmoid(ga).astype(BF16)

    gb = jnp.dot(u, w_ref[:, o:o + d_model], preferred_element_type=F32)
    sgb_ref[0] = _sigmoid(gb).astype(BF16)


def _project(cx, x, mod, w_in, qg, kg, cos_t, slo_t, shi_t, *, layer, attn_w, kv_w, pool_w):
    b, seq_len, d = x.shape
    n_in = w_in.shape[2]
    tl = TOKEN_TILE
    t_all = tl + seq_len
    nt = t_all // tl
    n_kv = kv_w // HEAD_DIM
    row = lambda bi, ti: (bi, ti, 0)
    latent_row = lambda bi, ti: (bi, jnp.maximum(ti - 1, 0), 0)
    out_shape = (
        jax.ShapeDtypeStruct((b, attn_w, t_all), BF16),
        jax.ShapeDtypeStruct((b, attn_w, t_all), FP8),
        jax.ShapeDtypeStruct((b, t_all, kv_w), BF16),
        jax.ShapeDtypeStruct((b, t_all, kv_w), FP8),
        jax.ShapeDtypeStruct((b, n_kv, nt, HEAD_DIM, tl), BF16),
        jax.ShapeDtypeStruct((b, t_all, attn_w), BF16),
        jax.ShapeDtypeStruct((b, t_all, pool_w), F32),
        jax.ShapeDtypeStruct((b, t_all, pool_w), BF16),
        jax.ShapeDtypeStruct((b, t_all, d), BF16),
        jax.ShapeDtypeStruct((b, t_all, d), BF16),
    )
    out_specs = (
        pl.BlockSpec((1, attn_w, tl), lambda bi, ti: (bi, 0, ti)),
        pl.BlockSpec((1, attn_w, tl), lambda bi, ti: (bi, 0, ti)),
        pl.BlockSpec((1, tl, kv_w), row),
        pl.BlockSpec((1, tl, kv_w), row),
        pl.BlockSpec((1, n_kv, 1, HEAD_DIM, tl), lambda bi, ti: (bi, 0, ti, 0, 0)),
        pl.BlockSpec((1, tl, attn_w), row),
        pl.BlockSpec((1, tl, pool_w), row),
        pl.BlockSpec((1, tl, pool_w), row),
        pl.BlockSpec((1, tl, d), row),
        pl.BlockSpec((1, tl, d), row),
    )
    in_specs = [
        pl.BlockSpec((1, tl, d), lambda bi, ti: (bi, 0, 0)),
        pl.BlockSpec((1, tl, d), latent_row),
        pl.BlockSpec((None, 1, 1, 1, 3 * d),
                     lambda bi, ti: (layer, bi, jnp.minimum(ti, 1), 0, 0)),
        pl.BlockSpec((None, d, n_in), lambda bi, ti: (layer, 0, 0)),
        pl.BlockSpec((None, 1, HEAD_DIM), lambda bi, ti: (layer, 0, 0)),
        pl.BlockSpec((None, 1, HEAD_DIM), lambda bi, ti: (layer, 0, 0)),
        pl.BlockSpec((tl, HEAD_DIM), lambda bi, ti: (ti, 0)),
        pl.BlockSpec((tl, HEAD_DIM), lambda bi, ti: (ti, 0)),
        pl.BlockSpec((tl, HEAD_DIM), lambda bi, ti: (ti, 0)),
    ]
    return pl.pallas_call(
        functools.partial(_proj_kernel, d_model=d, attn_w=attn_w, kv_w=kv_w, pool_w=pool_w),
        grid=(b, nt),
        in_specs=in_specs,
        out_specs=out_specs,
        out_shape=out_shape,
        compiler_params=pltpu.CompilerParams(
            dimension_semantics=("arbitrary", "arbitrary"), vmem_limit_bytes=VMEM_LIMIT_BYTES),
        name="projection",
    )(cx, x, mod, w_in, qg, kg, cos_t, slo_t, shi_t)


def _col_reduce(x, op):
    sl = SUBLANES
    nblk = x.shape[0] // sl
    accs = [x[sl * j:sl * j + sl] for j in range(min(REDUCE_CHAINS, nblk))]
    for i in range(len(accs), nblk):
        accs[i % REDUCE_CHAINS] = op(accs[i % REDUCE_CHAINS], x[sl * i:sl * i + sl])
    while len(accs) > 1:
        accs = [op(accs[2 * i], accs[2 * i + 1]) for i in range(len(accs) // 2)]
    return accs[0]


def _attn_kernel(qT_ref, k_ref, vT_ref, o_ref, acc_ref, m_ref, l_ref,
                 *, n_latent_steps, bounded):
    t = pl.program_id(2)

    def keys(j0, nj):
        start = j0 * KV_SUB
        if not isinstance(start, int):
            start = pl.multiple_of(start, KV_SUB)
        return k_ref[0, pl.ds(start, nj * KV_SUB), :]

    def scores(kc, r):
        qT = qT_ref[0, r * HEAD_DIM:(r + 1) * HEAD_DIM, :]
        return jnp.dot(kc, qT, preferred_element_type=F32)

    def head_update(r, s, j0, nj, first):
        if bounded:
            p = jnp.exp2(s)
        else:
            cmax = jnp.max(_col_reduce(s, jnp.maximum), axis=0, keepdims=True)
            m_new = cmax if first else jnp.maximum(m_ref[r], cmax)
            p = jnp.exp2(s - m_new)
        lsum = jnp.sum(_col_reduce(p, jnp.add), axis=0, keepdims=True)
        pb = p.astype(BF16)
        pv = jnp.dot(vT_ref[0, 0, j0], pb[0:KV_SUB], preferred_element_type=F32)
        for jj in range(1, nj):
            pv += jnp.dot(vT_ref[0, 0, j0 + jj], pb[jj * KV_SUB:(jj + 1) * KV_SUB],
                          preferred_element_type=F32)
        if first:
            l_ref[r] = lsum
            acc_ref[r] = pv
        elif bounded:
            l_ref[r] += lsum
            acc_ref[r] += pv
        else:
            alpha = jnp.exp2(m_ref[r] - m_new)
            l_ref[r] = alpha * l_ref[r] + lsum
            acc_ref[r] = alpha * acc_ref[r] + pv
        if not bounded:
            m_ref[r] = m_new

    def run_chunks(chunks, first):
        s = scores(keys(*chunks[0]), 0)
        for ci, (j0, n) in enumerate(chunks):
            kc = keys(j0, n)
            for r in range(GROUP):
                if r + 1 < GROUP:
                    s_next = scores(kc, r + 1)
                elif ci + 1 < len(chunks):
                    s_next = scores(keys(*chunks[ci + 1]), 0)
                else:
                    s_next = None
                head_update(r, s, j0, n, first and ci == 0)
                s = s_next

    nj = KV_SUBS_PER_STEP
    if bounded:
        @pl.when(t == 0)
        def _context_queries():
            run_chunks([(0, 1)], True)

        @pl.when(t > 0)
        def _latent_queries():
            run_chunks([(0, 1 + nj)] + [(1 + u * nj, nj) for u in range(1, n_latent_steps)], True)
    else:
        run_chunks([(0, 1)], True)

        @pl.when(t > 0)
        def _latent_keys():
            def body(i, carry):
                run_chunks([(1 + i * nj, nj)], False)
                return carry

            lax.fori_loop(0, n_latent_steps, body, 0)

    for r in range(GROUP):
        o = acc_ref[r] * (1.0 / l_ref[r])
        o_ref[0, :, r * HEAD_DIM:(r + 1) * HEAD_DIM] = o.T.astype(BF16)


def _attention(qT, k, vT, *, bounded):
    b, attn_w, t_all = qT.shape
    n_kv, nt = vT.shape[1], vT.shape[2]
    tl = TOKEN_TILE
    gw = GROUP * HEAD_DIM
    n_latent_steps = (nt - 1) // KV_SUBS_PER_STEP
    return pl.pallas_call(
        functools.partial(_attn_kernel, n_latent_steps=n_latent_steps, bounded=bounded),
        grid=(b, n_kv, nt),
        in_specs=[
            pl.BlockSpec((1, gw, tl), lambda bi, g, ti: (bi, g, ti)),
            pl.BlockSpec((1, t_all, HEAD_DIM), lambda bi, g, ti: (bi, 0, g)),
            pl.BlockSpec((1, 1, nt, HEAD_DIM, KV_SUB), lambda bi, g, ti: (bi, g, 0, 0, 0)),
        ],
        out_specs=pl.BlockSpec((1, tl, gw), lambda bi, g, ti: (bi, ti, g)),
        out_shape=jax.ShapeDtypeStruct((b, t_all, attn_w), BF16),
        scratch_shapes=[
            pltpu.VMEM((GROUP, HEAD_DIM, tl), F32),
            pltpu.VMEM((GROUP, 1, tl), F32),
            pltpu.VMEM((GROUP, 1, tl), F32),
        ],
        compiler_params=pltpu.CompilerParams(
            dimension_semantics=("arbitrary", "arbitrary", "arbitrary"),
            vmem_limit_bytes=VMEM_LIMIT_BYTES),
        name="attention_bounded" if bounded else "attention_online",
    )(qT, k, vT)


def _bounded_path_ok(q_gain, k_gain):
    gq = jnp.max(jnp.abs(q_gain), axis=-1)
    gk = jnp.max(jnp.abs(k_gain), axis=-1)
    return ((HEAD_DIM * Q_SCALE * gq * gk < BOUNDED_SCORE_LIMIT)
            & (gq < FP8_GAIN_LIMIT) & (gk < FP8_GAIN_LIMIT))


def _attention_dispatch(qT, q8T, k, k8, vT, bounded_ok):
    return lax.cond(bounded_ok,
                    lambda: _attention(q8T, k8, vT, bounded=True),
                    lambda: _attention(qT, k, vT, bounded=False))


def _merge_kernel(oa_ref, ob_ref, szaa_ref, szab_ref, xba_ref, xbb_ref, xbp_ref, xbn_ref,
                  szba_ref, szbb_ref, sgaa_ref, sgab_ref, sgba_ref, sgbb_ref,
                  cx_ref, x_ref, mod_ref, wpool_ref, pscale_ref, wa_ref, wb_ref, wo_ref,
                  lng_ref, lnb_ref, cx_out_ref, x_out_ref, ext_ref,
                  *, d_model, n_tiles, ctx_len, seq_len, alpha_res):
    j = pl.program_id(1)
    tl = TOKEN_TILE
    h = POOL_HALO
    first = j == 0

    prev_ok = j >= 2
    next_ok = jnp.logical_and(j >= 1, 2 * j <= n_tiles - 2)
    ext_ref[0:h, :] = jnp.where(prev_ok, xbp_ref[0], 0.0)
    ext_ref[h:h + tl, :] = jnp.where(first, 0.0, xba_ref[0])
    ext_ref[h + tl:h + 2 * tl, :] = xbb_ref[0]
    ext_ref[h + 2 * tl:2 * h + 2 * tl, :] = jnp.where(next_ok, xbn_ref[0], 0.0)

    pos_a = jnp.maximum(2 * j - 2, 0) * tl
    pos_b = jnp.where(first, 0, (2 * j - 1) * tl)
    seq_b = jnp.where(first, ctx_len, seq_len)
    edge_row = lax.broadcasted_iota(jnp.int32, (h, POOL_GROUP), 0)

    def inv_count(pos, seq, a, bb):
        cnt = jnp.minimum(pos + bb, seq) - jnp.maximum(pos - a, 0)
        return 1.0 / cnt.astype(F32)

    def pooled(hi, gi, w):
        a = w // 2
        bb = w - a
        base = h + hi * tl
        lanes = slice(gi * POOL_GROUP, (gi + 1) * POOL_GROUP)
        pos, seq = (pos_a, seq_len) if hi == 0 else (pos_b, seq_b)
        acc = ext_ref[base - a:base - a + tl, lanes]
        for jj in range(-a + 1, bb):
            acc = acc + ext_ref[base + jj:base + jj + tl, lanes]
        mean = jnp.concatenate([
            acc[0:h] * inv_count(pos + edge_row, seq, a, bb),
            acc[h:tl - h] * (1.0 / w),
            acc[tl - h:tl] * inv_count(pos + (tl - h) + edge_row, seq, a, bb)], axis=0)
        return mean - ext_ref[base:base + tl, lanes]

    halves = ((oa_ref, szaa_ref, szba_ref, sgaa_ref, sgba_ref),
              (ob_ref, szab_ref, szbb_ref, sgab_ref, sgbb_ref))
    ya, yb = [], []
    for hi, (o_ref, sza_ref, szb_ref, _, _) in enumerate(halves):
        a_in = (o_ref[0].astype(F32) * sza_ref[0].astype(F32)).astype(BF16)
        ya.append(jnp.dot(a_in, wa_ref[...], preferred_element_type=F32))
        parts = [jnp.dot(pooled(hi, gi, w).astype(BF16), wpool_ref[gi],
                         preferred_element_type=F32) for gi, w in enumerate(POOL_WINDOWS)]
        p_pool = jnp.concatenate(parts, axis=-1) * pscale_ref[...]
        b_in = (p_pool * szb_ref[0].astype(F32)).astype(BF16)
        yb.append(jnp.dot(b_in, wb_ref[...], preferred_element_type=F32))
    yo = []
    for hi, (_, _, _, sga_ref, sgb_ref) in enumerate(halves):
        y = (sga_ref[0].astype(F32) * ya[hi] + sgb_ref[0].astype(F32) * yb[hi]).astype(BF16)
        yo.append(jnp.dot(y, wo_ref[...], preferred_element_type=F32))

    gate_ctx = mod_ref[0, 0][:, 2 * d_model:]
    gate_lat = mod_ref[0, 1][:, 2 * d_model:]
    x_in = (x_ref[0, 0:tl, :], jnp.where(first, cx_ref[0], x_ref[0, tl:2 * tl, :]))
    gates = (gate_lat, jnp.where(first, gate_ctx, gate_lat))
    for hi in range(2):
        r = alpha_res * x_in[hi] + gates[hi] * yo[hi]
        mu = jnp.mean(r, axis=-1, keepdims=True)
        rc = r - mu
        var = jnp.mean(rc * rc, axis=-1, keepdims=True)
        res = rc * lax.rsqrt(var + EPS) * lng_ref[...] + lnb_ref[...]
        x_out_ref[0, hi * tl:(hi + 1) * tl, :] = res
        if hi == 1:
            cx_out_ref[0] = res


def _merge(o, sza, xb, szb, sga, sgb, cx, x, mod, wpool, pscale, wa, wb, wo, lng, lnb,
           *, layer, ctx_len, alpha_res):
    b, seq_len, d = x.shape
    attn_w = o.shape[2]
    pool_w = xb.shape[2]
    tl = TOKEN_TILE
    nt = (tl + seq_len) // tl
    assert nt % 2 == 1
    n_steps = (nt + 1) // 2
    hb = tl // POOL_HALO
    n_hblocks = nt * hb
    tile_a = lambda bi, j: (bi, jnp.maximum(2 * j - 1, 0), 0)
    tile_b = lambda bi, j: (bi, 2 * j, 0)
    per_layer = lambda bi, j: (layer, 0, 0)

    def pair(width):
        return [pl.BlockSpec((1, tl, width), tile_a), pl.BlockSpec((1, tl, width), tile_b)]

    in_specs = (
        pair(attn_w) + pair(attn_w) + pair(pool_w) + [
            pl.BlockSpec((1, POOL_HALO, pool_w),
                         lambda bi, j: (bi, jnp.maximum((2 * j - 1) * hb - 1, 0), 0)),
            pl.BlockSpec((1, POOL_HALO, pool_w),
                         lambda bi, j: (bi, jnp.minimum((2 * j + 1) * hb, n_hblocks - 1), 0)),
        ] + pair(pool_w) + pair(d) + pair(d) + [
            pl.BlockSpec((1, tl, d), lambda bi, j: (bi, 0, 0)),
            pl.BlockSpec((1, 2 * tl, d), lambda bi, j: (bi, jnp.maximum(j - 1, 0), 0)),
            pl.BlockSpec((None, 1, 2, 1, 3 * d), lambda bi, j: (layer, bi, 0, 0, 0)),
            pl.BlockSpec((None,) + wpool.shape[1:], lambda bi, j: (layer, 0, 0, 0)),
            pl.BlockSpec((None, 1, pool_w), per_layer),
            pl.BlockSpec((None,) + wa.shape[1:], per_layer),
            pl.BlockSpec((None,) + wb.shape[1:], per_layer),
            pl.BlockSpec((None,) + wo.shape[1:], per_layer),
            pl.BlockSpec((None, 1, d), per_layer),
            pl.BlockSpec((None, 1, d), per_layer),
        ])
    return pl.pallas_call(
        functools.partial(_merge_kernel, d_model=d, n_tiles=nt, ctx_len=ctx_len,
                          seq_len=seq_len, alpha_res=alpha_res),
        grid=(b, n_steps),
        in_specs=in_specs,
        out_specs=(
            pl.BlockSpec((1, tl, d), lambda bi, j: (bi, jnp.minimum(j, 1), 0)),
            pl.BlockSpec((1, 2 * tl, d), lambda bi, j: (bi, jnp.maximum(j - 1, 0), 0)),
        ),
        out_shape=(
            jax.ShapeDtypeStruct((b, 2 * tl, d), F32),
            jax.ShapeDtypeStruct((b, seq_len, d), F32),
        ),
        scratch_shapes=[pltpu.VMEM((2 * tl + 2 * POOL_HALO, pool_w), F32)],
        compiler_params=pltpu.CompilerParams(
            dimension_semantics=("arbitrary", "arbitrary"), vmem_limit_bytes=VMEM_LIMIT_BYTES),
        name="merge",
    )(o, o, sza, sza, xb, xb, xb, xb, szb, szb, sga, sga, sgb, sgb, cx, x, mod,
      wpool, pscale, wa, wb, wo, lng, lnb)


def _rope_tables(seq_len, ctx_len):
    n_freq = ROT_HALF
    n_rows = seq_len // GRID_W
    inv_freq = 1.0 / (ROPE_THETA ** (jnp.arange(n_freq, dtype=F32) / n_freq))
    ar = jnp.arange(n_rows, dtype=jnp.int32).astype(F32)[:, None] * inv_freq[None, :]
    ac = jnp.arange(GRID_W, dtype=jnp.int32).astype(F32)[:, None] * inv_freq[None, :]

    def grid_table(row_part, col_part):
        rp = jnp.broadcast_to(jnp.concatenate(row_part, axis=-1)[:, None, :],
                              (n_rows, GRID_W, 2 * n_freq))
        cp = jnp.broadcast_to(jnp.concatenate(col_part, axis=-1)[None, :, :],
                              (n_rows, GRID_W, 2 * n_freq))
        return jnp.concatenate([rp, cp], axis=-1).reshape(seq_len, HEAD_DIM)

    cos_r, sin_r, cos_c, sin_c = jnp.cos(ar), jnp.sin(ar), jnp.cos(ac), jnp.sin(ac)
    zr, zc = jnp.zeros_like(sin_r), jnp.zeros_like(sin_c)
    cos = grid_table([cos_r, cos_r], [cos_c, cos_c])
    slo = grid_table([-sin_r, zr], [-sin_c, zc])
    shi = grid_table([zr, sin_r], [zc, sin_c])
    ones = jnp.ones((ctx_len, HEAD_DIM), F32)
    zeros = jnp.zeros((ctx_len, HEAD_DIM), F32)
    return (jnp.concatenate([ones, cos], axis=0),
            jnp.concatenate([zeros, slo], axis=0),
            jnp.concatenate([zeros, shi], axis=0))


def kernel(x, c, ctx, c_ctx, w_mod, b_mod, w_in, q_norm, k_norm, w_pool, pool_scale,
           w_br_a, w_br_b, w_out, ln_g, ln_b):
    b, seq_len, d = x.shape
    ctx_len = ctx.shape[1]
    depth = w_in.shape[0]
    attn_w = w_br_a.shape[1]
    pool_w = w_br_b.shape[1]
    kv_w = (w_in.shape[2] - 2 * attn_w - 2 * pool_w - 2 * d) // 2
    assert ctx_len == TOKEN_TILE and seq_len % (TOKEN_TILE * KV_SUBS_PER_STEP) == 0
    assert attn_w // HEAD_DIM == GROUP * N_KV_HEADS and kv_w == N_KV_HEADS * HEAD_DIM
    assert seq_len % GRID_W == 0 and 1 + b <= SUBLANES
    alpha_res = (2 * depth) ** 0.25
    cx = ctx

    c_rows = jnp.zeros((SUBLANES, d), F32).at[0].set(c_ctx).at[1:1 + b].set(c)
    mod = _modulation(c_rows, w_mod, b_mod)

    cos_t, slo_t, shi_t = _rope_tables(seq_len, ctx_len)

    mod_ctx = jnp.broadcast_to(mod[:, 0:1], (depth, b, 3 * d))
    mod = jnp.stack([mod_ctx, mod[:, 1:1 + b]], axis=2).reshape(depth, b, 2, 1, 3 * d)

    w_in_b = w_in.astype(BF16)
    w_pool_b = w_pool.astype(BF16)
    w_a_b = w_br_a.astype(BF16)
    w_b_b = w_br_b.astype(BF16)
    w_o_b = w_out.astype(BF16)
    qg = q_norm.reshape(depth, 1, HEAD_DIM)
    kg = k_norm.reshape(depth, 1, HEAD_DIM)
    pscale = pool_scale.reshape(depth, 1, pool_w)
    lng = ln_g.reshape(depth, 1, d)
    lnb = ln_b.reshape(depth, 1, d)
    bounded_ok = _bounded_path_ok(q_norm, k_norm)

    for l in range(depth):
        qT, q8T, k, k8, vT, sza, xb, szb, sga, sgb = _project(
            cx, x, mod, w_in_b, qg, kg, cos_t, slo_t, shi_t,
            layer=l, attn_w=attn_w, kv_w=kv_w, pool_w=pool_w)
        o = _attention_dispatch(qT, q8T, k, k8, vT, bounded_ok[l])
        cx, x = _merge(o, sza, xb, szb, sga, sgb, cx, x, mod, w_pool_b, pscale,
                       w_a_b, w_b_b, w_o_b, lng, lnb,
                       layer=l, ctx_len=ctx_len, alpha_res=alpha_res)
    return x
```

```python
import functools
import math

import jax
import jax.numpy as jnp
from jax import lax
from jax.experimental import pallas as pl
from jax.experimental.pallas import tpu as pltpu

F32 = jnp.float32
BF16 = jnp.bfloat16
FP8 = jnp.float8_e4m3fn

SUBLANES = 8

HEAD_DIM = 128
N_KV_HEADS = 2
GROUP = 4
GRID_W = 64
ROT_HALF = HEAD_DIM // 4
POOL_WINDOWS = (2, 4, 8, 16)
POOL_GROUP = 128
POOL_HALO = max(POOL_WINDOWS) // 2
ROPE_THETA = 10000.0
EPS = 1e-6

TOKEN_TILE = 256
KV_SUB = 256
KV_SUBS_PER_STEP = 8
VMEM_LIMIT_BYTES = 56 * 1024 * 1024

Q_SCALE = HEAD_DIM ** -0.5 * math.log2(math.e)
REDUCE_CHAINS = 4
BOUNDED_SCORE_LIMIT = 24.0
FP8_GAIN_LIMIT = 4.0


def _sigmoid(v):
    return 1.0 / (1.0 + jnp.exp(-v))


def _mod_kernel(c_ref, w_ref, b_ref, o_ref):
    cv = c_ref[...]
    s = cv * _sigmoid(cv)
    o_ref[0] = jnp.dot(s, w_ref[0], preferred_element_type=F32) + b_ref[0]


def _modulation(c_rows, w_mod, b_mod):
    depth, d, d3 = w_mod.shape
    rows = c_rows.shape[0]
    return pl.pallas_call(
        _mod_kernel,
        grid=(depth,),
        in_specs=[
            pl.BlockSpec((rows, d), lambda l: (0, 0)),
            pl.BlockSpec((1, d, d3), lambda l: (l, 0, 0)),
            pl.BlockSpec((1, 1, d3), lambda l: (l, 0, 0)),
        ],
        out_specs=pl.BlockSpec((1, rows, d3), lambda l: (l, 0, 0)),
        out_shape=jax.ShapeDtypeStruct((depth, rows, d3), F32),
        compiler_params=pltpu.CompilerParams(
            dimension_semantics=("arbitrary",), vmem_limit_bytes=VMEM_LIMIT_BYTES),
        name="modulation",
    )(c_rows, w_mod, b_mod.reshape(depth, 1, d3))


def _proj_kernel(cx_ref, x_ref, mod_ref, w_ref, qg_ref, kg_ref, cos_ref, slo_ref, shi_ref,
                 qT_ref, q8T_ref, k_ref, k8_ref, vT_ref, sza_ref, xb_ref, szb_ref, sga_ref, sgb_ref,
                 *, d_model, attn_w, kv_w, pool_w):
    x = jnp.where(pl.program_id(1) == 0, cx_ref[0], x_ref[0])
    mod = mod_ref[0, 0]
    shift = mod[:, :d_model]
    scale = mod[:, d_model:2 * d_model]
    u = (x * (1.0 + scale) + shift).astype(BF16)

    cos = cos_ref[...]
    slo = slo_ref[...]
    shi = shi_ref[...]

    def norm_rope(h, g):
        ms = jnp.mean(h * h, axis=-1, keepdims=True)
        y = h * lax.rsqrt(ms + EPS) * g
        return (y * cos + pltpu.roll(y, HEAD_DIM - ROT_HALF, 1) * slo
                + pltpu.roll(y, ROT_HALF, 1) * shi)

    o = 0
    hq = jnp.dot(u, w_ref[:, o:o + attn_w], preferred_element_type=F32)
    o += attn_w
    qg = qg_ref[...]
    for h in range(attn_w // HEAD_DIM):
        qh = norm_rope(hq[:, h * HEAD_DIM:(h + 1) * HEAD_DIM], qg) * Q_SCALE
        qhT = qh.T
        qT_ref[0, h * HEAD_DIM:(h + 1) * HEAD_DIM, :] = qhT.astype(BF16)
        q8T_ref[0, h * HEAD_DIM:(h + 1) * HEAD_DIM, :] = qhT.astype(FP8)

    hkv = jnp.dot(u, w_ref[:, o:o + 2 * kv_w], preferred_element_type=F32)
    o += 2 * kv_w
    kg = kg_ref[...]
    for g in range(kv_w // HEAD_DIM):
        kh = norm_rope(hkv[:, g * HEAD_DIM:(g + 1) * HEAD_DIM], kg)
        k_ref[0, :, g * HEAD_DIM:(g + 1) * HEAD_DIM] = kh.astype(BF16)
        k8_ref[0, :, g * HEAD_DIM:(g + 1) * HEAD_DIM] = kh.astype(FP8)
        vh = hkv[:, kv_w + g * HEAD_DIM:kv_w + (g + 1) * HEAD_DIM]
        vT_ref[0, g, 0] = vh.T.astype(BF16)

    za = jnp.dot(u, w_ref[:, o:o + attn_w], preferred_element_type=F32)
    o += attn_w
    sza_ref[0] = (za * _sigmoid(za)).astype(BF16)

    xb_ref[0] = jnp.dot(u, w_ref[:, o:o + pool_w], preferred_element_type=F32)
    o += pool_w

    zb = jnp.dot(u, w_ref[:, o:o + pool_w], preferred_element_type=F32)
    o += pool_w
    szb_ref[0] = (zb * _sigmoid(zb)).astype(BF16)

    ga = jnp.dot(u, w_ref[:, o:o + d_model], preferred_element_type=F32)
    o += d_model
    sga_ref[0] = _sigmoid(ga).astype(BF16)

    gb = jnp.dot(u, w_ref[:, o:o + d_model], preferred_element_type=F32)
    sgb_ref[0] = _sigmoid(gb).astype(BF16)


def _project(cx, x, mod, w_in, qg, kg, cos_t, slo_t, shi_t, *, layer, attn_w, kv_w, pool_w):
    b, seq_len, d = x.shape
    n_in = w_in.shape[2]
    tl = TOKEN_TILE
    t_all = tl + seq_len
    nt = t_all // tl
    n_kv = kv_w // HEAD_DIM
    row = lambda bi, ti: (bi, ti, 0)
    latent_row = lambda bi, ti: (bi, jnp.maximum(ti - 1, 0), 0)
    out_shape = (
        jax.ShapeDtypeStruct((b, attn_w, t_all), BF16),
        jax.ShapeDtypeStruct((b, attn_w, t_all), FP8),
        jax.ShapeDtypeStruct((b, t_all, kv_w), BF16),
        jax.ShapeDtypeStruct((b, t_all, kv_w), FP8),
        jax.ShapeDtypeStruct((b, n_kv, nt, HEAD_DIM, tl), BF16),
        jax.ShapeDtypeStruct((b, t_all, attn_w), BF16),
        jax.ShapeDtypeStruct((b, t_all, pool_w), F32),
        jax.ShapeDtypeStruct((b, t_all, pool_w), BF16),
        jax.ShapeDtypeStruct((b, t_all, d), BF16),
        jax.ShapeDtypeStruct((b, t_all, d), BF16),
    )
    out_specs = (
        pl.BlockSpec((1, attn_w, tl), lambda bi, ti: (bi, 0, ti)),
        pl.BlockSpec((1, attn_w, tl), lambda bi, ti: (bi, 0, ti)),
        pl.BlockSpec((1, tl, kv_w), row),
        pl.BlockSpec((1, tl, kv_w), row),
        pl.BlockSpec((1, n_kv, 1, HEAD_DIM, tl), lambda bi, ti: (bi, 0, ti, 0, 0)),
        pl.BlockSpec((1, tl, attn_w), row),
        pl.BlockSpec((1, tl, pool_w), row),
        pl.BlockSpec((1, tl, pool_w), row),
        pl.BlockSpec((1, tl, d), row),
        pl.BlockSpec((1, tl, d), row),
    )
    in_specs = [
        pl.BlockSpec((1, tl, d), lambda bi, ti: (bi, 0, 0)),
        pl.BlockSpec((1, tl, d), latent_row),
        pl.BlockSpec((None, 1, 1, 1, 3 * d),
                     lambda bi, ti: (layer, bi, jnp.minimum(ti, 1), 0, 0)),
        pl.BlockSpec((None, d, n_in), lambda bi, ti: (layer, 0, 0)),
        pl.BlockSpec((None, 1, HEAD_DIM), lambda bi, ti: (layer, 0, 0)),
        pl.BlockSpec((None, 1, HEAD_DIM), lambda bi, ti: (layer, 0, 0)),
        pl.BlockSpec((tl, HEAD_DIM), lambda bi, ti: (ti, 0)),
        pl.BlockSpec((tl, HEAD_DIM), lambda bi, ti: (ti, 0)),
        pl.BlockSpec((tl, HEAD_DIM), lambda bi, ti: (ti, 0)),
    ]
    return pl.pallas_call(
        functools.partial(_proj_kernel, d_model=d, attn_w=attn_w, kv_w=kv_w, pool_w=pool_w),
        grid=(b, nt),
        in_specs=in_specs,
        out_specs=out_specs,
        out_shape=out_shape,
        compiler_params=pltpu.CompilerParams(
            dimension_semantics=("arbitrary", "arbitrary"), vmem_limit_bytes=VMEM_LIMIT_BYTES),
        name="projection",
    )(cx, x, mod, w_in, qg, kg, cos_t, slo_t, shi_t)


def _col_reduce(x, op):
    sl = SUBLANES
    nblk = x.shape[0] // sl
    accs = [x[sl * j:sl * j + sl] for j in range(min(REDUCE_CHAINS, nblk))]
    for i in range(len(accs), nblk):
        accs[i % REDUCE_CHAINS] = op(accs[i % REDUCE_CHAINS], x[sl * i:sl * i + sl])
    while len(accs) > 1:
        accs = [op(accs[2 * i], accs[2 * i + 1]) for i in range(len(accs) // 2)]
    return accs[0]


def _attn_kernel(qT_ref, k_ref, vT_ref, o_ref, acc_ref, m_ref, l_ref,
                 *, n_latent_steps, bounded):
    t = pl.program_id(2)

    def keys(j0, nj):
        start = j0 * KV_SUB
        if not isinstance(start, int):
            start = pl.multiple_of(start, KV_SUB)
        return k_ref[0, pl.ds(start, nj * KV_SUB), :]

    def scores(kc, r):
        qT = qT_ref[0, r * HEAD_DIM:(r + 1) * HEAD_DIM, :]
        return jnp.dot(kc, qT, preferred_element_type=F32)

    def head_update(r, s, j0, nj, first):
        if bounded:
            p = jnp.exp2(s)
        else:
            cmax = jnp.max(_col_reduce(s, jnp.maximum), axis=0, keepdims=True)
            m_new = cmax if first else jnp.maximum(m_ref[r], cmax)
            p = jnp.exp2(s - m_new)
        lsum = jnp.sum(_col_reduce(p, jnp.add), axis=0, keepdims=True)
        pb = p.astype(BF16)
        pv = jnp.dot(vT_ref[0, 0, j0], pb[0:KV_SUB], preferred_element_type=F32)
        for jj in range(1, nj):
            pv += jnp.dot(vT_ref[0, 0, j0 + jj], pb[jj * KV_SUB:(jj + 1) * KV_SUB],
                          preferred_element_type=F32)
        if first:
            l_ref[r] = lsum
            acc_ref[r] = pv
        elif bounded:
            l_ref[r] += lsum
            acc_ref[r] += pv
        else:
            alpha = jnp.exp2(m_ref[r] - m_new)
            l_ref[r] = alpha * l_ref[r] + lsum
            acc_ref[r] = alpha * acc_ref[r] + pv
        if not bounded:
            m_ref[r] = m_new

    def run_chunks(chunks, first):
        s = scores(keys(*chunks[0]), 0)
        for ci, (j0, n) in enumerate(chunks):
            kc = keys(j0, n)
            for r in range(GROUP):
                if r + 1 < GROUP:
                    s_next = scores(kc, r + 1)
                elif ci + 1 < len(chunks):
                    s_next = scores(keys(*chunks[ci + 1]), 0)
                else:
                    s_next = None
                head_update(r, s, j0, n, first and ci == 0)
                s = s_next

    nj = KV_SUBS_PER_STEP
    if bounded:
        @pl.when(t == 0)
        def _context_queries():
            run_chunks([(0, 1)], True)

        @pl.when(t > 0)
        def _latent_queries():
            run_chunks([(0, 1 + nj)] + [(1 + u * nj, nj) for u in range(1, n_latent_steps)], True)
    else:
        run_chunks([(0, 1)], True)

        @pl.when(t > 0)
        def _latent_keys():
            def body(i, carry):
                run_chunks([(1 + i * nj, nj)], False)
                return carry

            lax.fori_loop(0, n_latent_steps, body, 0)

    for r in range(GROUP):
        o = acc_ref[r] * (1.0 / l_ref[r])
        o_ref[0, :, r * HEAD_DIM:(r + 1) * HEAD_DIM] = o.T.astype(BF16)


def _attention(qT, k, vT, *, bounded):
    b, attn_w, t_all = qT.shape
    n_kv, nt = vT.shape[1], vT.shape[2]
    tl = TOKEN_TILE
    gw = GROUP * HEAD_DIM
    n_latent_steps = (nt - 1) // KV_SUBS_PER_STEP
    return pl.pallas_call(
        functools.partial(_attn_kernel, n_latent_steps=n_latent_steps, bounded=bounded),
        grid=(b, n_kv, nt),
        in_specs=[
            pl.BlockSpec((1, gw, tl), lambda bi, g, ti: (bi, g, ti)),
            pl.BlockSpec((1, t_all, HEAD_DIM), lambda bi, g, ti: (bi, 0, g)),
            pl.BlockSpec((1, 1, nt, HEAD_DIM, KV_SUB), lambda bi, g, ti: (bi, g, 0, 0, 0)),
        ],
        out_specs=pl.BlockSpec((1, tl, gw), lambda bi, g, ti: (bi, ti, g)),
        out_shape=jax.ShapeDtypeStruct((b, t_all, attn_w), BF16),
        scratch_shapes=[
            pltpu.VMEM((GROUP, HEAD_DIM, tl), F32),
            pltpu.VMEM((GROUP, 1, tl), F32),
            pltpu.VMEM((GROUP, 1, tl), F32),
        ],
        compiler_params=pltpu.CompilerParams(
            dimension_semantics=("arbitrary", "arbitrary", "arbitrary"),
            vmem_limit_bytes=VMEM_LIMIT_BYTES),
        name="attention_bounded" if bounded else "attention_online",
    )(qT, k, vT)


def _bounded_path_ok(q_gain, k_gain):
    gq = jnp.max(jnp.abs(q_gain), axis=-1)
    gk = jnp.max(jnp.abs(k_gain), axis=-1)
    return ((HEAD_DIM * Q_SCALE * gq * gk < BOUNDED_SCORE_LIMIT)
            & (gq < FP8_GAIN_LIMIT) & (gk < FP8_GAIN_LIMIT))


def _attention_dispatch(qT, q8T, k, k8, vT, bounded_ok):
    return lax.cond(bounded_ok,
                    lambda: _attention(q8T, k8, vT, bounded=True),
                    lambda: _attention(qT, k, vT, bounded=False))


def _merge_kernel(oa_ref, ob_ref, szaa_ref, szab_ref, xba_ref, xbb_ref, xbp_ref, xbn_ref,
                  szba_ref, szbb_ref, sgaa_ref, sgab_ref, sgba_ref, sgbb_ref,
                  cx_ref, x_ref, mod_ref, wpool_ref, pscale_ref, wa_ref, wb_ref, wo_ref,
                  lng_ref, lnb_ref, cx_out_ref, x_out_ref, ext_ref,
                  *, d_model, n_tiles, ctx_len, seq_len, alpha_res):
    j = pl.program_id(1)
    tl = TOKEN_TILE
    h = POOL_HALO
    first = j == 0

    prev_ok = j >= 2
    next_ok = jnp.logical_and(j >= 1, 2 * j <= n_tiles - 2)
    ext_ref[0:h, :] = jnp.where(prev_ok, xbp_ref[0], 0.0)
    ext_ref[h:h + tl, :] = jnp.where(first, 0.0, xba_ref[0])
    ext_ref[h + tl:h + 2 * tl, :] = xbb_ref[0]
    ext_ref[h + 2 * tl:2 * h + 2 * tl, :] = jnp.where(next_ok, xbn_ref[0], 0.0)

    pos_a = jnp.maximum(2 * j - 2, 0) * tl
    pos_b = jnp.where(first, 0, (2 * j - 1) * tl)
    seq_b = jnp.where(first, ctx_len, seq_len)
    edge_row = lax.broadcasted_iota(jnp.int32, (h, POOL_GROUP), 0)

    def inv_count(pos, seq, a, bb):
        cnt = jnp.minimum(pos + bb, seq) - jnp.maximum(pos - a, 0)
        return 1.0 / cnt.astype(F32)

    def pooled(hi, gi, w):
        a = w // 2
        bb = w - a
        base = h + hi * tl
        lanes = slice(gi * POOL_GROUP, (gi + 1) * POOL_GROUP)
        pos, seq = (pos_a, seq_len) if hi == 0 else (pos_b, seq_b)
        acc = ext_ref[base - a:base - a + tl, lanes]
        for jj in range(-a + 1, bb):
            acc = acc + ext_ref[base + jj:base + jj + tl, lanes]
        mean = jnp.concatenate([
            acc[0:h] * inv_count(pos + edge_row, seq, a, bb),
            acc[h:tl - h] * (1.0 / w),
            acc[tl - h:tl] * inv_count(pos + (tl - h) + edge_row, seq, a, bb)], axis=0)
        return mean - ext_ref[base:base + tl, lanes]

    halves = ((oa_ref, szaa_ref, szba_ref, sgaa_ref, sgba_ref),
              (ob_ref, szab_ref, szbb_ref, sgab_ref, sgbb_ref))
    ya, yb = [], []
    for hi, (o_ref, sza_ref, szb_ref, _, _) in enumerate(halves):
        a_in = (o_ref[0].astype(F32) * sza_ref[0].astype(F32)).astype(BF16)
        ya.append(jnp.dot(a_in, wa_ref[...], preferred_element_type=F32))
        parts = [jnp.dot(pooled(hi, gi, w).astype(BF16), wpool_ref[gi],
                         preferred_element_type=F32) for gi, w in enumerate(POOL_WINDOWS)]
        p_pool = jnp.concatenate(parts, axis=-1) * pscale_ref[...]
        b_in = (p_pool * szb_ref[0].astype(F32)).astype(BF16)
        yb.append(jnp.dot(b_in, wb_ref[...], preferred_element_type=F32))
    yo = []
    for hi, (_, _, _, sga_ref, sgb_ref) in enumerate(halves):
        y = (sga_ref[0].astype(F32) * ya[hi] + sgb_ref[0].astype(F32) * yb[hi]).astype(BF16)
        yo.append(jnp.dot(y, wo_ref[...], preferred_element_type=F32))

    gate_ctx = mod_ref[0, 0][:, 2 * d_model:]
    gate_lat = mod_ref[0, 1][:, 2 * d_model:]
    x_in = (x_ref[0, 0:tl, :], jnp.where(first, cx_ref[0], x_ref[0, tl:2 * tl, :]))
    gates = (gate_lat, jnp.where(first, gate_ctx, gate_lat))
    for hi in range(2):
        r = alpha_res * x_in[hi] + gates[hi] * yo[hi]
        mu = jnp.mean(r, axis=-1, keepdims=True)
        rc = r - mu
        var = jnp.mean(rc * rc, axis=-1, keepdims=True)
        res = rc * lax.rsqrt(var + EPS) * lng_ref[...] + lnb_ref[...]
        x_out_ref[0, hi * tl:(hi + 1) * tl, :] = res
        if hi == 1:
            cx_out_ref[0] = res


def _merge(o, sza, xb, szb, sga, sgb, cx, x, mod, wpool, pscale, wa, wb, wo, lng, lnb,
           *, layer, ctx_len, alpha_res):
    b, seq_len, d = x.shape
    attn_w = o.shape[2]
    pool_w = xb.shape[2]
    tl = TOKEN_TILE
    nt = (tl + seq_len) // tl
    assert nt % 2 == 1
    n_steps = (nt + 1) // 2
    hb = tl // POOL_HALO
    n_hblocks = nt * hb
    tile_a = lambda bi, j: (bi, jnp.maximum(2 * j - 1, 0), 0)
    tile_b = lambda bi, j: (bi, 2 * j, 0)
    per_layer = lambda bi, j: (layer, 0, 0)

    def pair(width):
        return [pl.BlockSpec((1, tl, width), tile_a), pl.BlockSpec((1, tl, width), tile_b)]

    in_specs = (
        pair(attn_w) + pair(attn_w) + pair(pool_w) + [
            pl.BlockSpec((1, POOL_HALO, pool_w),
                         lambda bi, j: (bi, jnp.maximum((2 * j - 1) * hb - 1, 0), 0)),
            pl.BlockSpec((1, POOL_HALO, pool_w),
                         lambda bi, j: (bi, jnp.minimum((2 * j + 1) * hb, n_hblocks - 1), 0)),
        ] + pair(pool_w) + pair(d) + pair(d) + [
            pl.BlockSpec((1, tl, d), lambda bi, j: (bi, 0, 0)),
            pl.BlockSpec((1, 2 * tl, d), lambda bi, j: (bi, jnp.maximum(j - 1, 0), 0)),
            pl.BlockSpec((None, 1, 2, 1, 3 * d), lambda bi, j: (layer, bi, 0, 0, 0)),
            pl.BlockSpec((None,) + wpool.shape[1:], lambda bi, j: (layer, 0, 0, 0)),
            pl.BlockSpec((None, 1, pool_w), per_layer),
            pl.BlockSpec((None,) + wa.shape[1:], per_layer),
            pl.BlockSpec((None,) + wb.shape[1:], per_layer),
            pl.BlockSpec((None,) + wo.shape[1:], per_layer),
            pl.BlockSpec((None, 1, d), per_layer),
            pl.BlockSpec((None, 1, d), per_layer),
        ])
    return pl.pallas_call(
        functools.partial(_merge_kernel, d_model=d, n_tiles=nt, ctx_len=ctx_len,
                          seq_len=seq_len, alpha_res=alpha_res),
        grid=(b, n_steps),
        in_specs=in_specs,
        out_specs=(
            pl.BlockSpec((1, tl, d), lambda bi, j: (bi, jnp.minimum(j, 1), 0)),
            pl.BlockSpec((1, 2 * tl, d), lambda bi, j: (bi, jnp.maximum(j - 1, 0), 0)),
        ),
        out_shape=(
            jax.ShapeDtypeStruct((b, 2 * tl, d), F32),
            jax.ShapeDtypeStruct((b, seq_len, d), F32),
        ),
        scratch_shapes=[pltpu.VMEM((2 * tl + 2 * POOL_HALO, pool_w), F32)],
        compiler_params=pltpu.CompilerParams(
            dimension_semantics=("arbitrary", "arbitrary"), vmem_limit_bytes=VMEM_LIMIT_BYTES),
        name="merge",
    )(o, o, sza, sza, xb, xb, xb, xb, szb, szb, sga, sga, sgb, sgb, cx, x, mod,
      wpool, pscale, wa, wb, wo, lng, lnb)


def _rope_tables(seq_len, ctx_len):
    n_freq = ROT_HALF
    n_rows = seq_len // GRID_W
    inv_freq = 1.0 / (ROPE_THETA ** (jnp.arange(n_freq, dtype=F32) / n_freq))
    ar = jnp.arange(n_rows, dtype=jnp.int32).astype(F32)[:, None] * inv_freq[None, :]
    ac = jnp.arange(GRID_W, dtype=jnp.int32).astype(F32)[:, None] * inv_freq[None, :]

    def grid_table(row_part, col_part):
        rp = jnp.broadcast_to(jnp.concatenate(row_part, axis=-1)[:, None, :],
                              (n_rows, GRID_W, 2 * n_freq))
        cp = jnp.broadcast_to(jnp.concatenate(col_part, axis=-1)[None, :, :],
                              (n_rows, GRID_W, 2 * n_freq))
        return jnp.concatenate([rp, cp], axis=-1).reshape(seq_len, HEAD_DIM)

    cos_r, sin_r, cos_c, sin_c = jnp.cos(ar), jnp.sin(ar), jnp.cos(ac), jnp.sin(ac)
    zr, zc = jnp.zeros_like(sin_r), jnp.zeros_like(sin_c)
    cos = grid_table([cos_r, cos_r], [cos_c, cos_c])
    slo = grid_table([-sin_r, zr], [-sin_c, zc])
    shi = grid_table([zr, sin_r], [zc, sin_c])
    ones = jnp.ones((ctx_len, HEAD_DIM), F32)
    zeros = jnp.zeros((ctx_len, HEAD_DIM), F32)
    return (jnp.concatenate([ones, cos], axis=0),
            jnp.concatenate([zeros, slo], axis=0),
            jnp.concatenate([zeros, shi], axis=0))


def kernel(x, c, ctx, c_ctx, w_mod, b_mod, w_in, q_norm, k_norm, w_pool, pool_scale,
           w_br_a, w_br_b, w_out, ln_g, ln_b):
    b, seq_len, d = x.shape
    ctx_len = ctx.shape[1]
    depth = w_in.shape[0]
    attn_w = w_br_a.shape[1]
    pool_w = w_br_b.shape[1]
    kv_w = (w_in.shape[2] - 2 * attn_w - 2 * pool_w - 2 * d) // 2
    assert ctx_len == TOKEN_TILE and seq_len % (TOKEN_TILE * KV_SUBS_PER_STEP) == 0
    assert attn_w // HEAD_DIM == GROUP * N_KV_HEADS and kv_w == N_KV_HEADS * HEAD_DIM
    assert seq_len % GRID_W == 0 and 1 + b <= SUBLANES
    alpha_res = (2 * depth) ** 0.25
    cx = ctx

    c_rows = jnp.zeros((SUBLANES, d), F32).at[0].set(c_ctx).at[1:1 + b].set(c)
    mod = _modulation(c_rows, w_mod, b_mod)

    cos_t, slo_t, shi_t = _rope_tables(seq_len, ctx_len)

    mod_ctx = jnp.broadcast_to(mod[:, 0:1], (depth, b, 3 * d))
    mod = jnp.stack([mod_ctx, mod[:, 1:1 + b]], axis=2).reshape(depth, b, 2, 1, 3 * d)

    w_in_b = w_in.astype(BF16)
    w_pool_b = w_pool.astype(BF16)
    w_a_b = w_br_a.astype(BF16)
    w_b_b = w_br_b.astype(BF16)
    w_o_b = w_out.astype(BF16)
    qg = q_norm.reshape(depth, 1, HEAD_DIM)
    kg = k_norm.reshape(depth, 1, HEAD_DIM)
    pscale = pool_scale.reshape(depth, 1, pool_w)
    lng = ln_g.reshape(depth, 1, d)
    lnb = ln_b.reshape(depth, 1, d)
    bounded_ok = _bounded_path_ok(q_norm, k_norm)

    for l in range(depth):
        qT, q8T, k, k8, vT, sza, xb, szb, sga, sgb = _project(
            cx, x, mod, w_in_b, qg, kg, cos_t, slo_t, shi_t,
            layer=l, attn_w=attn_w, kv_w=kv_w, pool_w=pool_w)
        o = _attention_dispatch(qT, q8T, k, k8, vT, bounded_ok[l])
        cx, x = _merge(o, sza, xb, szb, sga, sgb, cx, x, mod, w_pool_b, pscale,
                       w_a_b, w_b_b, w_o_b, lng, lnb,
                       layer=l, ctx_len=ctx_len, alpha_res=alpha_res)
    return x
```

```python
import functools
import math

import jax
import jax.numpy as jnp
from jax import lax
from jax.experimental import pallas as pl
from jax.experimental.pallas import tpu as pltpu

F32 = jnp.float32
BF16 = jnp.bfloat16
FP8 = jnp.float8_e4m3fn

SUBLANES = 8

HEAD_DIM = 128
N_KV_HEADS = 2
GROUP = 4
GRID_W = 64
ROT_HALF = HEAD_DIM // 4
POOL_WINDOWS = (2, 4, 8, 16)
POOL_GROUP = 128
POOL_HALO = max(POOL_WINDOWS) // 2
ROPE_THETA = 10000.0
EPS = 1e-6

TOKEN_TILE = 256
KV_SUB = 256
KV_SUBS_PER_STEP = 8
VMEM_LIMIT_BYTES = 56 * 1024 * 1024

Q_SCALE = HEAD_DIM ** -0.5 * math.log2(math.e)
REDUCE_CHAINS = 4
BOUNDED_SCORE_LIMIT = 24.0
FP8_GAIN_LIMIT = 4.0


def _sigmoid(v):
    return 1.0 / (1.0 + jnp.exp(-v))


def _mod_kernel(c_ref, w_ref, b_ref, o_ref):
    cv = c_ref[...]
    s = cv * _sigmoid(cv)
    o_ref[0] = jnp.dot(s, w_ref[0], preferred_element_type=F32) + b_ref[0]


def _modulation(c_rows, w_mod, b_mod):
    depth, d, d3 = w_mod.shape
    rows = c_rows.shape[0]
    return pl.pallas_call(
        _mod_kernel,
        grid=(depth,),
        in_specs=[
            pl.BlockSpec((rows, d), lambda l: (0, 0)),
            pl.BlockSpec((1, d, d3), lambda l: (l, 0, 0)),
            pl.BlockSpec((1, 1, d3), lambda l: (l, 0, 0)),
        ],
        out_specs=pl.BlockSpec((1, rows, d3), lambda l: (l, 0, 0)),
        out_shape=jax.ShapeDtypeStruct((depth, rows, d3), F32),
        compiler_params=pltpu.CompilerParams(
            dimension_semantics=("arbitrary",), vmem_limit_bytes=VMEM_LIMIT_BYTES),
        name="modulation",
    )(c_rows, w_mod, b_mod.reshape(depth, 1, d3))


def _proj_kernel(cx_ref, x_ref, mod_ref, w_ref, qg_ref, kg_ref, cos_ref, slo_ref, shi_ref,
                 qT_ref, q8T_ref, k_ref, k8_ref, vT_ref, sza_ref, xb_ref, szb_ref, sga_ref, sgb_ref,
                 *, d_model, attn_w, kv_w, pool_w):
    x = jnp.where(pl.program_id(1) == 0, cx_ref[0], x_ref[0])
    mod = mod_ref[0, 0]
    shift = mod[:, :d_model]
    scale = mod[:, d_model:2 * d_model]
    u = (x * (1.0 + scale) + shift).astype(BF16)

    cos = cos_ref[...]
    slo = slo_ref[...]
    shi = shi_ref[...]

    def norm_rope(h, g):
        ms = jnp.mean(h * h, axis=-1, keepdims=True)
        y = h * lax.rsqrt(ms + EPS) * g
        return (y * cos + pltpu.roll(y, HEAD_DIM - ROT_HALF, 1) * slo
                + pltpu.roll(y, ROT_HALF, 1) * shi)

    o = 0
    hq = jnp.dot(u, w_ref[:, o:o + attn_w], preferred_element_type=F32)
    o += attn_w
    qg = qg_ref[...]
    for h in range(attn_w // HEAD_DIM):
        qh = norm_rope(hq[:, h * HEAD_DIM:(h + 1) * HEAD_DIM], qg) * Q_SCALE
        qhT = qh.T
        qT_ref[0, h * HEAD_DIM:(h + 1) * HEAD_DIM, :] = qhT.astype(BF16)
        q8T_ref[0, h * HEAD_DIM:(h + 1) * HEAD_DIM, :] = qhT.astype(FP8)

    hkv = jnp.dot(u, w_ref[:, o:o + 2 * kv_w], preferred_element_type=F32)
    o += 2 * kv_w
    kg = kg_ref[...]
    for g in range(kv_w // HEAD_DIM):
        kh = norm_rope(hkv[:, g * HEAD_DIM:(g + 1) * HEAD_DIM], kg)
        k_ref[0, :, g * HEAD_DIM:(g + 1) * HEAD_DIM] = kh.astype(BF16)
        k8_ref[0, :, g * HEAD_DIM:(g + 1) * HEAD_DIM] = kh.astype(FP8)
        vh = hkv[:, kv_w + g * HEAD_DIM:kv_w + (g + 1) * HEAD_DIM]
        vT_ref[0, g, 0] = vh.T.astype(BF16)

    za = jnp.dot(u, w_ref[:, o:o + attn_w], preferred_element_type=F32)
    o += attn_w
    sza_ref[0] = (za * _sigmoid(za)).astype(BF16)

    xb_ref[0] = jnp.dot(u, w_ref[:, o:o + pool_w], preferred_element_type=F32)
    o += pool_w

    zb = jnp.dot(u, w_ref[:, o:o + pool_w], preferred_element_type=F32)
    o += pool_w
    szb_ref[0] = (zb * _sigmoid(zb)).astype(BF16)

    ga = jnp.dot(u, w_ref[:, o:o + d_model], preferred_element_type=F32)
    o += d_model
    sga_ref[0] = _sigmoid(ga).astype(BF16)

    gb = jnp.dot(u, w_ref[:, o:o + d_model], preferred_element_type=F32)
    sgb_ref[0] = _sigmoid(gb).astype(BF16)


def _project(cx, x, mod, w_in, qg, kg, cos_t, slo_t, shi_t, *, layer, attn_w, kv_w, pool_w):
    b, seq_len, d = x.shape
    n_in = w_in.shape[2]
    tl = TOKEN_TILE
    t_all = tl + seq_len
    nt = t_all // tl
    n_kv = kv_w // HEAD_DIM
    row = lambda bi, ti: (bi, ti, 0)
    latent_row = lambda bi, ti: (bi, jnp.maximum(ti - 1, 0), 0)
    out_shape = (
        jax.ShapeDtypeStruct((b, attn_w, t_all), BF16),
        jax.ShapeDtypeStruct((b, attn_w, t_all), FP8),
        jax.ShapeDtypeStruct((b, t_all, kv_w), BF16),
        jax.ShapeDtypeStruct((b, t_all, kv_w), FP8),
        jax.ShapeDtypeStruct((b, n_kv, nt, HEAD_DIM, tl), BF16),
        jax.ShapeDtypeStruct((b, t_all, attn_w), BF16),
        jax.ShapeDtypeStruct((b, t_all, pool_w), F32),
        jax.ShapeDtypeStruct((b, t_all, pool_w), BF16),
        jax.ShapeDtypeStruct((b, t_all, d), BF16),
        jax.ShapeDtypeStruct((b, t_all, d), BF16),
    )
    out_specs = (
        pl.BlockSpec((1, attn_w, tl), lambda bi, ti: (bi, 0, ti)),
        pl.BlockSpec((1, attn_w, tl), lambda bi, ti: (bi, 0, ti)),
        pl.BlockSpec((1, tl, kv_w), row),
        pl.BlockSpec((1, tl, kv_w), row),
        pl.BlockSpec((1, n_kv, 1, HEAD_DIM, tl), lambda bi, ti: (bi, 0, ti, 0, 0)),
        pl.BlockSpec((1, tl, attn_w), row),
        pl.BlockSpec((1, tl, pool_w), row),
        pl.BlockSpec((1, tl, pool_w), row),
        pl.BlockSpec((1, tl, d), row),
        pl.BlockSpec((1, tl, d), row),
    )
    in_specs = [
        pl.BlockSpec((1, tl, d), lambda bi, ti: (bi, 0, 0)),
        pl.BlockSpec((1, tl, d), latent_row),
        pl.BlockSpec((None, 1, 1, 1, 3 * d),
                     lambda bi, ti: (layer, bi, jnp.minimum(ti, 1), 0, 0)),
        pl.BlockSpec((None, d, n_in), lambda bi, ti: (layer, 0, 0)),
        pl.BlockSpec((None, 1, HEAD_DIM), lambda bi, ti: (layer, 0, 0)),
        pl.BlockSpec((None, 1, HEAD_DIM), lambda bi, ti: (layer, 0, 0)),
        pl.BlockSpec((tl, HEAD_DIM), lambda bi, ti: (ti, 0)),
        pl.BlockSpec((tl, HEAD_DIM), lambda bi, ti: (ti, 0)),
        pl.BlockSpec((tl, HEAD_DIM), lambda bi, ti: (ti, 0)),
    ]
    return pl.pallas_call(
        functools.partial(_proj_kernel, d_model=d, attn_w=attn_w, kv_w=kv_w, pool_w=pool_w),
        grid=(b, nt),
        in_specs=in_specs,
        out_specs=out_specs,
        out_shape=out_shape,
        compiler_params=pltpu.CompilerParams(
            dimension_semantics=("arbitrary", "arbitrary"), vmem_limit_bytes=VMEM_LIMIT_BYTES),
        name="projection",
    )(cx, x, mod, w_in, qg, kg, cos_t, slo_t, shi_t)


def _col_reduce(x, op):
    sl = SUBLANES
    nblk = x.shape[0] // sl
    accs = [x[sl * j:sl * j + sl] for j in range(min(REDUCE_CHAINS, nblk))]
    for i in range(len(accs), nblk):
        accs[i % REDUCE_CHAINS] = op(accs[i % REDUCE_CHAINS], x[sl * i:sl * i + sl])
    while len(accs) > 1:
        accs = [op(accs[2 * i], accs[2 * i + 1]) for i in range(len(accs) // 2)]
    return accs[0]


def _attn_kernel(qa_ref, qb_ref, k_ref, vT_ref, o_ref, acc_ref, m_ref, l_ref,
                 *, n_latent_steps, bounded):
    tl = qa_ref.shape[2]
    for sub, (qT_ref, is_context) in enumerate(((qa_ref, pl.program_id(2) == 0), (qb_ref, False))):
        _attn_tile(qT_ref, k_ref, vT_ref, o_ref.at[0, pl.ds(sub * tl, tl)],
                   acc_ref.at[sub], m_ref.at[sub], l_ref.at[sub],
                   is_context=is_context, n_latent_steps=n_latent_steps, bounded=bounded)


def _attn_tile(qT_ref, k_ref, vT_ref, o_ref, acc_ref, m_ref, l_ref,
               *, is_context, n_latent_steps, bounded):
    maybe_context = is_context is not False

    def keys(j0, nj):
        start = j0 * KV_SUB
        if not isinstance(start, int):
            start = pl.multiple_of(start, KV_SUB)
        return k_ref[0, pl.ds(start, nj * KV_SUB), :]

    def scores(kc, r):
        qT = qT_ref[0, r * HEAD_DIM:(r + 1) * HEAD_DIM, :]
        return jnp.dot(kc, qT, preferred_element_type=F32)

    def head_update(r, s, j0, nj, first):
        if bounded:
            p = jnp.exp2(s)
        else:
            cmax = jnp.max(_col_reduce(s, jnp.maximum), axis=0, keepdims=True)
            m_new = cmax if first else jnp.maximum(m_ref[r], cmax)
            p = jnp.exp2(s - m_new)
        lsum = jnp.sum(_col_reduce(p, jnp.add), axis=0, keepdims=True)
        pb = p.astype(BF16)
        pv = jnp.dot(vT_ref[0, 0, j0], pb[0:KV_SUB], preferred_element_type=F32)
        for jj in range(1, nj):
            pv += jnp.dot(vT_ref[0, 0, j0 + jj], pb[jj * KV_SUB:(jj + 1) * KV_SUB],
                          preferred_element_type=F32)
        if first:
            l_ref[r] = lsum
            acc_ref[r] = pv
        elif bounded:
            l_ref[r] += lsum
            acc_ref[r] += pv
        else:
            alpha = jnp.exp2(m_ref[r] - m_new)
            l_ref[r] = alpha * l_ref[r] + lsum
            acc_ref[r] = alpha * acc_ref[r] + pv
        if not bounded:
            m_ref[r] = m_new

    def run_chunks(chunks, first):
        s = scores(keys(*chunks[0]), 0)
        for ci, (j0, n) in enumerate(chunks):
            kc = keys(j0, n)
            for r in range(GROUP):
                if r + 1 < GROUP:
                    s_next = scores(kc, r + 1)
                elif ci + 1 < len(chunks):
                    s_next = scores(keys(*chunks[ci + 1]), 0)
                else:
                    s_next = None
                head_update(r, s, j0, n, first and ci == 0)
                s = s_next

    nj = KV_SUBS_PER_STEP
    if bounded:
        def latent_queries():
            run_chunks([(0, 1 + nj)] + [(1 + u * nj, nj) for u in range(1, n_latent_steps)], True)

        if maybe_context:
            pl.when(is_context)(lambda: run_chunks([(0, 1)], True))
            pl.when(jnp.logical_not(is_context))(latent_queries)
        else:
            latent_queries()
    else:
        run_chunks([(0, 1)], True)

        def latent_keys():
            def body(i, carry):
                run_chunks([(1 + i * nj, nj)], False)
                return carry

            lax.fori_loop(0, n_latent_steps, body, 0)

        if maybe_context:
            pl.when(jnp.logical_not(is_context))(latent_keys)
        else:
            latent_keys()

    for r in range(GROUP):
        o = acc_ref[r] * (1.0 / l_ref[r])
        o_ref[:, r * HEAD_DIM:(r + 1) * HEAD_DIM] = o.T.astype(BF16)


def _attention(qT, k, vT, *, bounded):
    b, attn_w, t_all = qT.shape
    n_kv, nt = vT.shape[1], vT.shape[2]
    tl = TOKEN_TILE
    gw = GROUP * HEAD_DIM
    n_latent_steps = (nt - 1) // KV_SUBS_PER_STEP
    n_pairs = (nt + 1) // 2
    return pl.pallas_call(
        functools.partial(_attn_kernel, n_latent_steps=n_latent_steps, bounded=bounded),
        grid=(b, n_kv, n_pairs),
        in_specs=[
            pl.BlockSpec((1, gw, tl), lambda bi, g, j: (bi, g, 2 * j)),
            pl.BlockSpec((1, gw, tl), lambda bi, g, j: (bi, g, jnp.minimum(2 * j + 1, nt - 1))),
            pl.BlockSpec((1, t_all, HEAD_DIM), lambda bi, g, j: (bi, 0, g)),
            pl.BlockSpec((1, 1, nt, HEAD_DIM, KV_SUB), lambda bi, g, j: (bi, g, 0, 0, 0)),
        ],
        out_specs=pl.BlockSpec((1, 2 * tl, gw), lambda bi, g, j: (bi, j, g)),
        out_shape=jax.ShapeDtypeStruct((b, n_pairs * 2 * tl, attn_w), BF16),
        scratch_shapes=[
            pltpu.VMEM((2, GROUP, HEAD_DIM, tl), F32),
            pltpu.VMEM((2, GROUP, 1, tl), F32),
            pltpu.VMEM((2, GROUP, 1, tl), F32),
        ],
        compiler_params=pltpu.CompilerParams(
            dimension_semantics=("arbitrary", "arbitrary", "arbitrary"),
            vmem_limit_bytes=VMEM_LIMIT_BYTES),
        name="attention_bounded" if bounded else "attention_online",
    )(qT, qT, k, vT)


def _bounded_path_ok(q_gain, k_gain):
    gq = jnp.max(jnp.abs(q_gain), axis=-1)
    gk = jnp.max(jnp.abs(k_gain), axis=-1)
    return ((HEAD_DIM * Q_SCALE * gq * gk < BOUNDED_SCORE_LIMIT)
            & (gq < FP8_GAIN_LIMIT) & (gk < FP8_GAIN_LIMIT))


def _attention_dispatch(qT, q8T, k, k8, vT, bounded_ok):
    return lax.cond(bounded_ok,
                    lambda: _attention(q8T, k8, vT, bounded=True),
                    lambda: _attention(qT, k, vT, bounded=False))


def _merge_kernel(oa_ref, ob_ref, szaa_ref, szab_ref, xba_ref, xbb_ref, xbp_ref, xbn_ref,
                  szba_ref, szbb_ref, sgaa_ref, sgab_ref, sgba_ref, sgbb_ref,
                  cx_ref, x_ref, mod_ref, wpool_ref, pscale_ref, wa_ref, wb_ref, wo_ref,
                  lng_ref, lnb_ref, cx_out_ref, x_out_ref, ext_ref,
                  *, d_model, n_tiles, ctx_len, seq_len, alpha_res):
    j = pl.program_id(1)
    tl = TOKEN_TILE
    h = POOL_HALO
    first = j == 0

    prev_ok = j >= 2
    next_ok = jnp.logical_and(j >= 1, 2 * j <= n_tiles - 2)
    ext_ref[0:h, :] = jnp.where(prev_ok, xbp_ref[0], 0.0)
    ext_ref[h:h + tl, :] = jnp.where(first, 0.0, xba_ref[0])
    ext_ref[h + tl:h + 2 * tl, :] = xbb_ref[0]
    ext_ref[h + 2 * tl:2 * h + 2 * tl, :] = jnp.where(next_ok, xbn_ref[0], 0.0)

    pos_a = jnp.maximum(2 * j - 2, 0) * tl
    pos_b = jnp.where(first, 0, (2 * j - 1) * tl)
    seq_b = jnp.where(first, ctx_len, seq_len)
    edge_row = lax.broadcasted_iota(jnp.int32, (h, POOL_GROUP), 0)

    def inv_count(pos, seq, a, bb):
        cnt = jnp.minimum(pos + bb, seq) - jnp.maximum(pos - a, 0)
        return 1.0 / cnt.astype(F32)

    def pooled(hi, gi, w):
        a = w // 2
        bb = w - a
        base = h + hi * tl
        lanes = slice(gi * POOL_GROUP, (gi + 1) * POOL_GROUP)
        pos, seq = (pos_a, seq_len) if hi == 0 else (pos_b, seq_b)
        acc = ext_ref[base - a:base - a + tl, lanes]
        for jj in range(-a + 1, bb):
            acc = acc + ext_ref[base + jj:base + jj + tl, lanes]
        mean = jnp.concatenate([
            acc[0:h] * inv_count(pos + edge_row, seq, a, bb),
            acc[h:tl - h] * (1.0 / w),
            acc[tl - h:tl] * inv_count(pos + (tl - h) + edge_row, seq, a, bb)], axis=0)
        return mean - ext_ref[base:base + tl, lanes]

    halves = ((oa_ref, szaa_ref, szba_ref, sgaa_ref, sgba_ref),
              (ob_ref, szab_ref, szbb_ref, sgab_ref, sgbb_ref))
    ya, yb = [], []
    for hi, (o_ref, sza_ref, szb_ref, _, _) in enumerate(halves):
        a_in = (o_ref[0].astype(F32) * sza_ref[0].astype(F32)).astype(BF16)
        ya.append(jnp.dot(a_in, wa_ref[...], preferred_element_type=F32))
        parts = [jnp.dot(pooled(hi, gi, w).astype(BF16), wpool_ref[gi],
                         preferred_element_type=F32) for gi, w in enumerate(POOL_WINDOWS)]
        p_pool = jnp.concatenate(parts, axis=-1) * pscale_ref[...]
        b_in = (p_pool * szb_ref[0].astype(F32)).astype(BF16)
        yb.append(jnp.dot(b_in, wb_ref[...], preferred_element_type=F32))
    yo = []
    for hi, (_, _, _, sga_ref, sgb_ref) in enumerate(halves):
        y = (sga_ref[0].astype(F32) * ya[hi] + sgb_ref[0].astype(F32) * yb[hi]).astype(BF16)
        yo.append(jnp.dot(y, wo_ref[...], preferred_element_type=F32))

    gate_ctx = mod_ref[0, 0][:, 2 * d_model:]
    gate_lat = mod_ref[0, 1][:, 2 * d_model:]
    x_in = (x_ref[0, 0:tl, :], jnp.where(first, cx_ref[0], x_ref[0, tl:2 * tl, :]))
    gates = (gate_lat, jnp.where(first, gate_ctx, gate_lat))
    for hi in range(2):
        r = alpha_res * x_in[hi] + gates[hi] * yo[hi]
        mu = jnp.mean(r, axis=-1, keepdims=True)
        rc = r - mu
        var = jnp.mean(rc * rc, axis=-1, keepdims=True)
        res = rc * lax.rsqrt(var + EPS) * lng_ref[...] + lnb_ref[...]
        x_out_ref[0, hi * tl:(hi + 1) * tl, :] = res
        if hi == 1:
            cx_out_ref[0] = res


def _merge(o, sza, xb, szb, sga, sgb, cx, x, mod, wpool, pscale, wa, wb, wo, lng, lnb,
           *, layer, ctx_len, alpha_res):
    b, seq_len, d = x.shape
    attn_w = o.shape[2]
    pool_w = xb.shape[2]
    tl = TOKEN_TILE
    nt = (tl + seq_len) // tl
    assert nt % 2 == 1
    n_steps = (nt + 1) // 2
    hb = tl // POOL_HALO
    n_hblocks = nt * hb
    tile_a = lambda bi, j: (bi, jnp.maximum(2 * j - 1, 0), 0)
    tile_b = lambda bi, j: (bi, 2 * j, 0)
    per_layer = lambda bi, j: (layer, 0, 0)

    def pair(width):
        return [pl.BlockSpec((1, tl, width), tile_a), pl.BlockSpec((1, tl, width), tile_b)]

    in_specs = (
        pair(attn_w) + pair(attn_w) + pair(pool_w) + [
            pl.BlockSpec((1, POOL_HALO, pool_w),
                         lambda bi, j: (bi, jnp.maximum((2 * j - 1) * hb - 1, 0), 0)),
            pl.BlockSpec((1, POOL_HALO, pool_w),
                         lambda bi, j: (bi, jnp.minimum((2 * j + 1) * hb, n_hblocks - 1), 0)),
        ] + pair(pool_w) + pair(d) + pair(d) + [
            pl.BlockSpec((1, tl, d), lambda bi, j: (bi, 0, 0)),
            pl.BlockSpec((1, 2 * tl, d), lambda bi, j: (bi, jnp.maximum(j - 1, 0), 0)),
            pl.BlockSpec((None, 1, 2, 1, 3 * d), lambda bi, j: (layer, bi, 0, 0, 0)),
            pl.BlockSpec((None,) + wpool.shape[1:], lambda bi, j: (layer, 0, 0, 0)),
            pl.BlockSpec((None, 1, pool_w), per_layer),
            pl.BlockSpec((None,) + wa.shape[1:], per_layer),
            pl.BlockSpec((None,) + wb.shape[1:], per_layer),
            pl.BlockSpec((None,) + wo.shape[1:], per_layer),
            pl.BlockSpec((None, 1, d), per_layer),
            pl.BlockSpec((None, 1, d), per_layer),
        ])
    return pl.pallas_call(
        functools.partial(_merge_kernel, d_model=d, n_tiles=nt, ctx_len=ctx_len,
                          seq_len=seq_len, alpha_res=alpha_res),
        grid=(b, n_steps),
        in_specs=in_specs,
        out_specs=(
            pl.BlockSpec((1, tl, d), lambda bi, j: (bi, jnp.minimum(j, 1), 0)),
            pl.BlockSpec((1, 2 * tl, d), lambda bi, j: (bi, jnp.maximum(j - 1, 0), 0)),
        ),
        out_shape=(
            jax.ShapeDtypeStruct((b, 2 * tl, d), F32),
            jax.ShapeDtypeStruct((b, seq_len, d), F32),
        ),
        scratch_shapes=[pltpu.VMEM((2 * tl + 2 * POOL_HALO, pool_w), F32)],
        compiler_params=pltpu.CompilerParams(
            dimension_semantics=("arbitrary", "arbitrary"), vmem_limit_bytes=VMEM_LIMIT_BYTES),
        name="merge",
    )(o, o, sza, sza, xb, xb, xb, xb, szb, szb, sga, sga, sgb, sgb, cx, x, mod,
      wpool, pscale, wa, wb, wo, lng, lnb)


def _rope_tables(seq_len, ctx_len):
    n_freq = ROT_HALF
    n_rows = seq_len // GRID_W
    inv_freq = 1.0 / (ROPE_THETA ** (jnp.arange(n_freq, dtype=F32) / n_freq))
    ar = jnp.arange(n_rows, dtype=jnp.int32).astype(F32)[:, None] * inv_freq[None, :]
    ac = jnp.arange(GRID_W, dtype=jnp.int32).astype(F32)[:, None] * inv_freq[None, :]

    def grid_table(row_part, col_part):
        rp = jnp.broadcast_to(jnp.concatenate(row_part, axis=-1)[:, None, :],
                              (n_rows, GRID_W, 2 * n_freq))
        cp = jnp.broadcast_to(jnp.concatenate(col_part, axis=-1)[None, :, :],
                              (n_rows, GRID_W, 2 * n_freq))
        return jnp.concatenate([rp, cp], axis=-1).reshape(seq_len, HEAD_DIM)

    cos_r, sin_r, cos_c, sin_c = jnp.cos(ar), jnp.sin(ar), jnp.cos(ac), jnp.sin(ac)
    zr, zc = jnp.zeros_like(sin_r), jnp.zeros_like(sin_c)
    cos = grid_table([cos_r, cos_r], [cos_c, cos_c])
    slo = grid_table([-sin_r, zr], [-sin_c, zc])
    shi = grid_table([zr, sin_r], [zc, sin_c])
    ones = jnp.ones((ctx_len, HEAD_DIM), F32)
    zeros = jnp.zeros((ctx_len, HEAD_DIM), F32)
    return (jnp.concatenate([ones, cos], axis=0),
            jnp.concatenate([zeros, slo], axis=0),
            jnp.concatenate([zeros, shi], axis=0))


def kernel(x, c, ctx, c_ctx, w_mod, b_mod, w_in, q_norm, k_norm, w_pool, pool_scale,
           w_br_a, w_br_b, w_out, ln_g, ln_b):
    b, seq_len, d = x.shape
    ctx_len = ctx.shape[1]
    depth = w_in.shape[0]
    attn_w = w_br_a.shape[1]
    pool_w = w_br_b.shape[1]
    kv_w = (w_in.shape[2] - 2 * attn_w - 2 * pool_w - 2 * d) // 2
    assert ctx_len == TOKEN_TILE and seq_len % (TOKEN_TILE * KV_SUBS_PER_STEP) == 0
    assert attn_w // HEAD_DIM == GROUP * N_KV_HEADS and kv_w == N_KV_HEADS * HEAD_DIM
    assert seq_len % GRID_W == 0 and 1 + b <= SUBLANES
    alpha_res = (2 * depth) ** 0.25
    cx = ctx

    c_rows = jnp.zeros((SUBLANES, d), F32).at[0].set(c_ctx).at[1:1 + b].set(c)
    mod = _modulation(c_rows, w_mod, b_mod)

    cos_t, slo_t, shi_t = _rope_tables(seq_len, ctx_len)

    mod_ctx = jnp.broadcast_to(mod[:, 0:1], (depth, b, 3 * d))
    mod = jnp.stack([mod_ctx, mod[:, 1:1 + b]], axis=2).reshape(depth, b, 2, 1, 3 * d)

    w_in_b = w_in.astype(BF16)
    w_pool_b = w_pool.astype(BF16)
    w_a_b = w_br_a.astype(BF16)
    w_b_b = w_br_b.astype(BF16)
    w_o_b = w_out.astype(BF16)
    qg = q_norm.reshape(depth, 1, HEAD_DIM)
    kg = k_norm.reshape(depth, 1, HEAD_DIM)
    pscale = pool_scale.reshape(depth, 1, pool_w)
    lng = ln_g.reshape(depth, 1, d)
    lnb = ln_b.reshape(depth, 1, d)
    bounded_ok = _bounded_path_ok(q_norm, k_norm)

    for l in range(depth):
        qT, q8T, k, k8, vT, sza, xb, szb, sga, sgb = _project(
            cx, x, mod, w_in_b, qg, kg, cos_t, slo_t, shi_t,
            layer=l, attn_w=attn_w, kv_w=kv_w, pool_w=pool_w)
        o = _attention_dispatch(qT, q8T, k, k8, vT, bounded_ok[l])
        cx, x = _merge(o, sza, xb, szb, sga, sgb, cx, x, mod, w_pool_b, pscale,
                       w_a_b, w_b_b, w_o_b, lng, lnb,
                       layer=l, ctx_len=ctx_len, alpha_res=alpha_res)
    return x
```

```python
import functools
import math

import jax
import jax.numpy as jnp
from jax import lax
from jax.experimental import pallas as pl
from jax.experimental.pallas import tpu as pltpu

F32 = jnp.float32
BF16 = jnp.bfloat16
FP8 = jnp.float8_e4m3fn

SUBLANES = 8

HEAD_DIM = 128
N_KV_HEADS = 2
GROUP = 4
GRID_W = 64
ROT_HALF = HEAD_DIM // 4
POOL_WINDOWS = (2, 4, 8, 16)
POOL_GROUP = 128
POOL_HALO = max(POOL_WINDOWS) // 2
ROPE_THETA = 10000.0
EPS = 1e-6

TOKEN_TILE = 256
KV_SUB = 256
KV_SUBS_PER_STEP = 8
VMEM_LIMIT_BYTES = 56 * 1024 * 1024

Q_SCALE = HEAD_DIM ** -0.5 * math.log2(math.e)
REDUCE_CHAINS = 4
BOUNDED_SCORE_LIMIT = 24.0
FP8_GAIN_LIMIT = 4.0


def _sigmoid(v):
    return 1.0 / (1.0 + jnp.exp(-v))


def _mod_kernel(c_ref, w_ref, b_ref, o_ref):
    cv = c_ref[...]
    s = cv * _sigmoid(cv)
    o_ref[0] = jnp.dot(s, w_ref[0], preferred_element_type=F32) + b_ref[0]


def _modulation(c_rows, w_mod, b_mod):
    depth, d, d3 = w_mod.shape
    rows = c_rows.shape[0]
    return pl.pallas_call(
        _mod_kernel,
        grid=(depth,),
        in_specs=[
            pl.BlockSpec((rows, d), lambda l: (0, 0)),
            pl.BlockSpec((1, d, d3), lambda l: (l, 0, 0)),
            pl.BlockSpec((1, 1, d3), lambda l: (l, 0, 0)),
        ],
        out_specs=pl.BlockSpec((1, rows, d3), lambda l: (l, 0, 0)),
        out_shape=jax.ShapeDtypeStruct((depth, rows, d3), F32),
        compiler_params=pltpu.CompilerParams(
            dimension_semantics=("arbitrary",), vmem_limit_bytes=VMEM_LIMIT_BYTES),
        name="modulation",
    )(c_rows, w_mod, b_mod.reshape(depth, 1, d3))


def _proj_kernel(cx_ref, x_ref, mod_ref, w_ref, qg_ref, kg_ref, cos_ref, slo_ref, shi_ref,
                 qT_ref, q8T_ref, k_ref, k8_ref, vT_ref, sza_ref, xb_ref, szb_ref, sga_ref, sgb_ref,
                 *, d_model, attn_w, kv_w, pool_w):
    x = jnp.where(pl.program_id(1) == 0, cx_ref[0], x_ref[0])
    mod = mod_ref[0, 0]
    shift = mod[:, :d_model]
    scale = mod[:, d_model:2 * d_model]
    u = (x * (1.0 + scale) + shift).astype(BF16)

    cos = cos_ref[...]
    slo = slo_ref[...]
    shi = shi_ref[...]

    def norm_rope(h, g):
        ms = jnp.mean(h * h, axis=-1, keepdims=True)
        y = h * lax.rsqrt(ms + EPS) * g
        return (y * cos + pltpu.roll(y, HEAD_DIM - ROT_HALF, 1) * slo
                + pltpu.roll(y, ROT_HALF, 1) * shi)

    o = 0
    hq = jnp.dot(u, w_ref[:, o:o + attn_w], preferred_element_type=F32)
    o += attn_w
    qg = qg_ref[...]
    for h in range(attn_w // HEAD_DIM):
        qh = norm_rope(hq[:, h * HEAD_DIM:(h + 1) * HEAD_DIM], qg) * Q_SCALE
        qhT = qh.T
        qT_ref[0, h * HEAD_DIM:(h + 1) * HEAD_DIM, :] = qhT.astype(BF16)
        q8T_ref[0, h * HEAD_DIM:(h + 1) * HEAD_DIM, :] = qhT.astype(FP8)

    hkv = jnp.dot(u, w_ref[:, o:o + 2 * kv_w], preferred_element_type=F32)
    o += 2 * kv_w
    kg = kg_ref[...]
    for g in range(kv_w // HEAD_DIM):
        kh = norm_rope(hkv[:, g * HEAD_DIM:(g + 1) * HEAD_DIM], kg)
        k_ref[0, :, g * HEAD_DIM:(g + 1) * HEAD_DIM] = kh.astype(BF16)
        k8_ref[0, :, g * HEAD_DIM:(g + 1) * HEAD_DIM] = kh.astype(FP8)
        vh = hkv[:, kv_w + g * HEAD_DIM:kv_w + (g + 1) * HEAD_DIM]
        vT_ref[0, g, 0] = vh.T.astype(BF16)

    za = jnp.dot(u, w_ref[:, o:o + attn_w], preferred_element_type=F32)
    o += attn_w
    sza_ref[0] = (za * _sigmoid(za)).astype(BF16)

    xb_ref[0] = jnp.dot(u, w_ref[:, o:o + pool_w], preferred_element_type=F32)
    o += pool_w

    zb = jnp.dot(u, w_ref[:, o:o + pool_w], preferred_element_type=F32)
    o += pool_w
    szb_ref[0] = (zb * _sigmoid(zb)).astype(BF16)

    ga = jnp.dot(u, w_ref[:, o:o + d_model], preferred_element_type=F32)
    o += d_model
    sga_ref[0] = _sigmoid(ga).astype(BF16)

    gb = jnp.dot(u, w_ref[:, o:o + d_model], preferred_element_type=F32)
    sgb_ref[0] = _sigmoid(gb).astype(BF16)


def _project(cx, x, mod, w_in, qg, kg, cos_t, slo_t, shi_t, *, layer, attn_w, kv_w, pool_w):
    b, seq_len, d = x.shape
    n_in = w_in.shape[2]
    tl = TOKEN_TILE
    t_all = tl + seq_len
    nt = t_all // tl
    n_kv = kv_w // HEAD_DIM
    row = lambda bi, ti: (bi, ti, 0)
    latent_row = lambda bi, ti: (bi, jnp.maximum(ti - 1, 0), 0)
    out_shape = (
        jax.ShapeDtypeStruct((b, attn_w, t_all), BF16),
        jax.ShapeDtypeStruct((b, attn_w, t_all), FP8),
        jax.ShapeDtypeStruct((b, t_all, kv_w), BF16),
        jax.ShapeDtypeStruct((b, t_all, kv_w), FP8),
        jax.ShapeDtypeStruct((b, n_kv, nt, HEAD_DIM, tl), BF16),
        jax.ShapeDtypeStruct((b, t_all, attn_w), BF16),
        jax.ShapeDtypeStruct((b, t_all, pool_w), F32),
        jax.ShapeDtypeStruct((b, t_all, pool_w), BF16),
        jax.ShapeDtypeStruct((b, t_all, d), BF16),
        jax.ShapeDtypeStruct((b, t_all, d), BF16),
    )
    out_specs = (
        pl.BlockSpec((1, attn_w, tl), lambda bi, ti: (bi, 0, ti)),
        pl.BlockSpec((1, attn_w, tl), lambda bi, ti: (bi, 0, ti)),
        pl.BlockSpec((1, tl, kv_w), row),
        pl.BlockSpec((1, tl, kv_w), row),
        pl.BlockSpec((1, n_kv, 1, HEAD_DIM, tl), lambda bi, ti: (bi, 0, ti, 0, 0)),
        pl.BlockSpec((1, tl, attn_w), row),
        pl.BlockSpec((1, tl, pool_w), row),
        pl.BlockSpec((1, tl, pool_w), row),
        pl.BlockSpec((1, tl, d), row),
        pl.BlockSpec((1, tl, d), row),
    )
    in_specs = [
        pl.BlockSpec((1, tl, d), lambda bi, ti: (bi, 0, 0)),
        pl.BlockSpec((1, tl, d), latent_row),
        pl.BlockSpec((None, 1, 1, 1, 3 * d),
                     lambda bi, ti: (layer, bi, jnp.minimum(ti, 1), 0, 0)),
        pl.BlockSpec((None, d, n_in), lambda bi, ti: (layer, 0, 0)),
        pl.BlockSpec((None, 1, HEAD_DIM), lambda bi, ti: (layer, 0, 0)),
        pl.BlockSpec((None, 1, HEAD_DIM), lambda bi, ti: (layer, 0, 0)),
        pl.BlockSpec((tl, HEAD_DIM), lambda bi, ti: (ti, 0)),
        pl.BlockSpec((tl, HEAD_DIM), lambda bi, ti: (ti, 0)),
        pl.BlockSpec((tl, HEAD_DIM), lambda bi, ti: (ti, 0)),
    ]
    return pl.pallas_call(
        functools.partial(_proj_kernel, d_model=d, attn_w=attn_w, kv_w=kv_w, pool_w=pool_w),
        grid=(b, nt),
        in_specs=in_specs,
        out_specs=out_specs,
        out_shape=out_shape,
        compiler_params=pltpu.CompilerParams(
            dimension_semantics=("arbitrary", "arbitrary"), vmem_limit_bytes=VMEM_LIMIT_BYTES),
        name="projection",
    )(cx, x, mod, w_in, qg, kg, cos_t, slo_t, shi_t)


def _col_reduce(x, op):
    sl = SUBLANES
    nblk = x.shape[0] // sl
    accs = [x[sl * j:sl * j + sl] for j in range(min(REDUCE_CHAINS, nblk))]
    for i in range(len(accs), nblk):
        accs[i % REDUCE_CHAINS] = op(accs[i % REDUCE_CHAINS], x[sl * i:sl * i + sl])
    while len(accs) > 1:
        accs = [op(accs[2 * i], accs[2 * i + 1]) for i in range(len(accs) // 2)]
    return accs[0]


def _attn_kernel(qT_ref, k_ref, vT_ref, o_ref, acc_ref, m_ref, l_ref,
                 *, n_latent_steps, bounded):
    t = pl.program_id(2)

    def keys(j0, nj):
        start = j0 * KV_SUB
        if not isinstance(start, int):
            start = pl.multiple_of(start, KV_SUB)
        return k_ref[0, pl.ds(start, nj * KV_SUB), :]

    def scores(kc, r):
        qT = qT_ref[0, r * HEAD_DIM:(r + 1) * HEAD_DIM, :]
        return jnp.dot(kc, qT, preferred_element_type=F32)

    def head_update(r, s, j0, nj, first):
        if bounded:
            p = jnp.exp2(s)
        else:
            cmax = jnp.max(_col_reduce(s, jnp.maximum), axis=0, keepdims=True)
            m_new = cmax if first else jnp.maximum(m_ref[r], cmax)
            p = jnp.exp2(s - m_new)
        lsum = jnp.sum(_col_reduce(p, jnp.add), axis=0, keepdims=True)
        pb = p.astype(BF16)
        pv = jnp.dot(vT_ref[0, 0, j0], pb[0:KV_SUB], preferred_element_type=F32)
        for jj in range(1, nj):
            pv += jnp.dot(vT_ref[0, 0, j0 + jj], pb[jj * KV_SUB:(jj + 1) * KV_SUB],
                          preferred_element_type=F32)
        if first:
            l_ref[r] = lsum
            acc_ref[r] = pv
        elif bounded:
            l_ref[r] += lsum
            acc_ref[r] += pv
        else:
            alpha = jnp.exp2(m_ref[r] - m_new)
            l_ref[r] = alpha * l_ref[r] + lsum
            acc_ref[r] = alpha * acc_ref[r] + pv
        if not bounded:
            m_ref[r] = m_new

    def run_chunks(chunks, first):
        s = scores(keys(*chunks[0]), 0)
        for ci, (j0, n) in enumerate(chunks):
            kc = keys(j0, n)
            for r in range(GROUP):
                if r + 1 < GROUP:
                    s_next = scores(kc, r + 1)
                elif ci + 1 < len(chunks):
                    s_next = scores(keys(*chunks[ci + 1]), 0)
                else:
                    s_next = None
                head_update(r, s, j0, n, first and ci == 0)
                s = s_next

    nj = KV_SUBS_PER_STEP
    if bounded:
        @pl.when(t == 0)
        def _context_queries():
            run_chunks([(0, 1)], True)

        @pl.when(t > 0)
        def _latent_queries():
            run_chunks([(0, 1 + nj)] + [(1 + u * nj, nj) for u in range(1, n_latent_steps)], True)
    else:
        run_chunks([(0, 1)], True)

        @pl.when(t > 0)
        def _latent_keys():
            def body(i, carry):
                run_chunks([(1 + i * nj, nj)], False)
                return carry

            lax.fori_loop(0, n_latent_steps, body, 0)

    for r in range(GROUP):
        o = acc_ref[r] * (1.0 / l_ref[r])
        o_ref[0, :, r * HEAD_DIM:(r + 1) * HEAD_DIM] = o.T.astype(BF16)


def _attention(qT, k, vT, *, bounded):
    b, attn_w, t_all = qT.shape
    n_kv, nt = vT.shape[1], vT.shape[2]
    tl = TOKEN_TILE
    gw = GROUP * HEAD_DIM
    n_latent_steps = (nt - 1) // KV_SUBS_PER_STEP
    return pl.pallas_call(
        functools.partial(_attn_kernel, n_latent_steps=n_latent_steps, bounded=bounded),
        grid=(b, n_kv, nt),
        in_specs=[
            pl.BlockSpec((1, gw, tl), lambda bi, g, ti: (bi, g, ti)),
            pl.BlockSpec((1, t_all, HEAD_DIM), lambda bi, g, ti: (bi, 0, g)),
            pl.BlockSpec((1, 1, nt, HEAD_DIM, KV_SUB), lambda bi, g, ti: (bi, g, 0, 0, 0)),
        ],
        out_specs=pl.BlockSpec((1, tl, gw), lambda bi, g, ti: (bi, ti, g)),
        out_shape=jax.ShapeDtypeStruct((b, t_all, attn_w), BF16),
        scratch_shapes=[
            pltpu.VMEM((GROUP, HEAD_DIM, tl), F32),
            pltpu.VMEM((GROUP, 1, tl), F32),
            pltpu.VMEM((GROUP, 1, tl), F32),
        ],
        compiler_params=pltpu.CompilerParams(
            dimension_semantics=("arbitrary", "arbitrary", "arbitrary"),
            vmem_limit_bytes=VMEM_LIMIT_BYTES),
        name="attention_bounded" if bounded else "attention_online",
    )(qT, k, vT)


def _bounded_path_ok(q_gain, k_gain):
    gq = jnp.max(jnp.abs(q_gain), axis=-1)
    gk = jnp.max(jnp.abs(k_gain), axis=-1)
    return ((HEAD_DIM * Q_SCALE * gq * gk < BOUNDED_SCORE_LIMIT)
            & (gq < FP8_GAIN_LIMIT) & (gk < FP8_GAIN_LIMIT))


def _attention_dispatch(qT, q8T, k, k8, vT, bounded_ok):
    return lax.cond(bounded_ok,
                    lambda: _attention(qT, k, vT, bounded=True),
                    lambda: _attention(qT, k, vT, bounded=False))


def _merge_kernel(oa_ref, ob_ref, szaa_ref, szab_ref, xba_ref, xbb_ref, xbp_ref, xbn_ref,
                  szba_ref, szbb_ref, sgaa_ref, sgab_ref, sgba_ref, sgbb_ref,
                  cx_ref, x_ref, mod_ref, wpool_ref, pscale_ref, wa_ref, wb_ref, wo_ref,
                  lng_ref, lnb_ref, cx_out_ref, x_out_ref, ext_ref,
                  *, d_model, n_tiles, ctx_len, seq_len, alpha_res):
    j = pl.program_id(1)
    tl = TOKEN_TILE
    h = POOL_HALO
    first = j == 0

    prev_ok = j >= 2
    next_ok = jnp.logical_and(j >= 1, 2 * j <= n_tiles - 2)
    ext_ref[0:h, :] = jnp.where(prev_ok, xbp_ref[0], 0.0)
    ext_ref[h:h + tl, :] = jnp.where(first, 0.0, xba_ref[0])
    ext_ref[h + tl:h + 2 * tl, :] = xbb_ref[0]
    ext_ref[h + 2 * tl:2 * h + 2 * tl, :] = jnp.where(next_ok, xbn_ref[0], 0.0)

    pos_a = jnp.maximum(2 * j - 2, 0) * tl
    pos_b = jnp.where(first, 0, (2 * j - 1) * tl)
    seq_b = jnp.where(first, ctx_len, seq_len)
    edge_row = lax.broadcasted_iota(jnp.int32, (h, POOL_GROUP), 0)

    def inv_count(pos, seq, a, bb):
        cnt = jnp.minimum(pos + bb, seq) - jnp.maximum(pos - a, 0)
        return 1.0 / cnt.astype(F32)

    def pooled(hi, gi, w):
        a = w // 2
        bb = w - a
        base = h + hi * tl
        lanes = slice(gi * POOL_GROUP, (gi + 1) * POOL_GROUP)
        pos, seq = (pos_a, seq_len) if hi == 0 else (pos_b, seq_b)
        acc = ext_ref[base - a:base - a + tl, lanes]
        for jj in range(-a + 1, bb):
            acc = acc + ext_ref[base + jj:base + jj + tl, lanes]
        mean = jnp.concatenate([
            acc[0:h] * inv_count(pos + edge_row, seq, a, bb),
            acc[h:tl - h] * (1.0 / w),
            acc[tl - h:tl] * inv_count(pos + (tl - h) + edge_row, seq, a, bb)], axis=0)
        return mean - ext_ref[base:base + tl, lanes]

    halves = ((oa_ref, szaa_ref, szba_ref, sgaa_ref, sgba_ref),
              (ob_ref, szab_ref, szbb_ref, sgab_ref, sgbb_ref))
    ya, yb = [], []
    for hi, (o_ref, sza_ref, szb_ref, _, _) in enumerate(halves):
        a_in = (o_ref[0].astype(F32) * sza_ref[0].astype(F32)).astype(BF16)
        ya.append(jnp.dot(a_in, wa_ref[...], preferred_element_type=F32))
        parts = [jnp.dot(pooled(hi, gi, w).astype(BF16), wpool_ref[gi],
                         preferred_element_type=F32) for gi, w in enumerate(POOL_WINDOWS)]
        p_pool = jnp.concatenate(parts, axis=-1) * pscale_ref[...]
        b_in = (p_pool * szb_ref[0].astype(F32)).astype(BF16)
        yb.append(jnp.dot(b_in, wb_ref[...], preferred_element_type=F32))
    yo = []
    for hi, (_, _, _, sga_ref, sgb_ref) in enumerate(halves):
        y = (sga_ref[0].astype(F32) * ya[hi] + sgb_ref[0].astype(F32) * yb[hi]).astype(BF16)
        yo.append(jnp.dot(y, wo_ref[...], preferred_element_type=F32))

    gate_ctx = mod_ref[0, 0][:, 2 * d_model:]
    gate_lat = mod_ref[0, 1][:, 2 * d_model:]
    x_in = (x_ref[0, 0:tl, :], jnp.where(first, cx_ref[0], x_ref[0, tl:2 * tl, :]))
    gates = (gate_lat, jnp.where(first, gate_ctx, gate_lat))
    for hi in range(2):
        r = alpha_res * x_in[hi] + gates[hi] * yo[hi]
        mu = jnp.mean(r, axis=-1, keepdims=True)
        rc = r - mu
        var = jnp.mean(rc * rc, axis=-1, keepdims=True)
        res = rc * lax.rsqrt(var + EPS) * lng_ref[...] + lnb_ref[...]
        x_out_ref[0, hi * tl:(hi + 1) * tl, :] = res
        if hi == 1:
            cx_out_ref[0] = res


def _merge(o, sza, xb, szb, sga, sgb, cx, x, mod, wpool, pscale, wa, wb, wo, lng, lnb,
           *, layer, ctx_len, alpha_res):
    b, seq_len, d = x.shape
    attn_w = o.shape[2]
    pool_w = xb.shape[2]
    tl = TOKEN_TILE
    nt = (tl + seq_len) // tl
    assert nt % 2 == 1
    n_steps = (nt + 1) // 2
    hb = tl // POOL_HALO
    n_hblocks = nt * hb
    tile_a = lambda bi, j: (bi, jnp.maximum(2 * j - 1, 0), 0)
    tile_b = lambda bi, j: (bi, 2 * j, 0)
    per_layer = lambda bi, j: (layer, 0, 0)

    def pair(width):
        return [pl.BlockSpec((1, tl, width), tile_a), pl.BlockSpec((1, tl, width), tile_b)]

    in_specs = (
        pair(attn_w) + pair(attn_w) + pair(pool_w) + [
            pl.BlockSpec((1, POOL_HALO, pool_w),
                         lambda bi, j: (bi, jnp.maximum((2 * j - 1) * hb - 1, 0), 0)),
            pl.BlockSpec((1, POOL_HALO, pool_w),
                         lambda bi, j: (bi, jnp.minimum((2 * j + 1) * hb, n_hblocks - 1), 0)),
        ] + pair(pool_w) + pair(d) + pair(d) + [
            pl.BlockSpec((1, tl, d), lambda bi, j: (bi, 0, 0)),
            pl.BlockSpec((1, 2 * tl, d), lambda bi, j: (bi, jnp.maximum(j - 1, 0), 0)),
            pl.BlockSpec((None, 1, 2, 1, 3 * d), lambda bi, j: (layer, bi, 0, 0, 0)),
            pl.BlockSpec((None,) + wpool.shape[1:], lambda bi, j: (layer, 0, 0, 0)),
            pl.BlockSpec((None, 1, pool_w), per_layer),
            pl.BlockSpec((None,) + wa.shape[1:], per_layer),
            pl.BlockSpec((None,) + wb.shape[1:], per_layer),
            pl.BlockSpec((None,) + wo.shape[1:], per_layer),
            pl.BlockSpec((None, 1, d), per_layer),
            pl.BlockSpec((None, 1, d), per_layer),
        ])
    return pl.pallas_call(
        functools.partial(_merge_kernel, d_model=d, n_tiles=nt, ctx_len=ctx_len,
                          seq_len=seq_len, alpha_res=alpha_res),
        grid=(b, n_steps),
        in_specs=in_specs,
        out_specs=(
            pl.BlockSpec((1, tl, d), lambda bi, j: (bi, jnp.minimum(j, 1), 0)),
            pl.BlockSpec((1, 2 * tl, d), lambda bi, j: (bi, jnp.maximum(j - 1, 0), 0)),
        ),
        out_shape=(
            jax.ShapeDtypeStruct((b, 2 * tl, d), F32),
            jax.ShapeDtypeStruct((b, seq_len, d), F32),
        ),
        scratch_shapes=[pltpu.VMEM((2 * tl + 2 * POOL_HALO, pool_w), F32)],
        compiler_params=pltpu.CompilerParams(
            dimension_semantics=("arbitrary", "arbitrary"), vmem_limit_bytes=VMEM_LIMIT_BYTES),
        name="merge",
    )(o, o, sza, sza, xb, xb, xb, xb, szb, szb, sga, sga, sgb, sgb, cx, x, mod,
      wpool, pscale, wa, wb, wo, lng, lnb)


def _rope_tables(seq_len, ctx_len):
    n_freq = ROT_HALF
    n_rows = seq_len // GRID_W
    inv_freq = 1.0 / (ROPE_THETA ** (jnp.arange(n_freq, dtype=F32) / n_freq))
    ar = jnp.arange(n_rows, dtype=jnp.int32).astype(F32)[:, None] * inv_freq[None, :]
    ac = jnp.arange(GRID_W, dtype=jnp.int32).astype(F32)[:, None] * inv_freq[None, :]

    def grid_table(row_part, col_part):
        rp = jnp.broadcast_to(jnp.concatenate(row_part, axis=-1)[:, None, :],
                              (n_rows, GRID_W, 2 * n_freq))
        cp = jnp.broadcast_to(jnp.concatenate(col_part, axis=-1)[None, :, :],
                              (n_rows, GRID_W, 2 * n_freq))
        return jnp.concatenate([rp, cp], axis=-1).reshape(seq_len, HEAD_DIM)

    cos_r, sin_r, cos_c, sin_c = jnp.cos(ar), jnp.sin(ar), jnp.cos(ac), jnp.sin(ac)
    zr, zc = jnp.zeros_like(sin_r), jnp.zeros_like(sin_c)
    cos = grid_table([cos_r, cos_r], [cos_c, cos_c])
    slo = grid_table([-sin_r, zr], [-sin_c, zc])
    shi = grid_table([zr, sin_r], [zc, sin_c])
    ones = jnp.ones((ctx_len, HEAD_DIM), F32)
    zeros = jnp.zeros((ctx_len, HEAD_DIM), F32)
    return (jnp.concatenate([ones, cos], axis=0),
            jnp.concatenate([zeros, slo], axis=0),
            jnp.concatenate([zeros, shi], axis=0))


def kernel(x, c, ctx, c_ctx, w_mod, b_mod, w_in, q_norm, k_norm, w_pool, pool_scale,
           w_br_a, w_br_b, w_out, ln_g, ln_b):
    b, seq_len, d = x.shape
    ctx_len = ctx.shape[1]
    depth = w_in.shape[0]
    attn_w = w_br_a.shape[1]
    pool_w = w_br_b.shape[1]
    kv_w = (w_in.shape[2] - 2 * attn_w - 2 * pool_w - 2 * d) // 2
    assert ctx_len == TOKEN_TILE and seq_len % (TOKEN_TILE * KV_SUBS_PER_STEP) == 0
    assert attn_w // HEAD_DIM == GROUP * N_KV_HEADS and kv_w == N_KV_HEADS * HEAD_DIM
    assert seq_len % GRID_W == 0 and 1 + b <= SUBLANES
    alpha_res = (2 * depth) ** 0.25
    cx = ctx

    c_rows = jnp.zeros((SUBLANES, d), F32).at[0].set(c_ctx).at[1:1 + b].set(c)
    mod = _modulation(c_rows, w_mod, b_mod)

    cos_t, slo_t, shi_t = _rope_tables(seq_len, ctx_len)

    mod_ctx = jnp.broadcast_to(mod[:, 0:1], (depth, b, 3 * d))
    mod = jnp.stack([mod_ctx, mod[:, 1:1 + b]], axis=2).reshape(depth, b, 2, 1, 3 * d)

    w_in_b = w_in.astype(BF16)
    w_pool_b = w_pool.astype(BF16)
    w_a_b = w_br_a.astype(BF16)
    w_b_b = w_br_b.astype(BF16)
    w_o_b = w_out.astype(BF16)
    qg = q_norm.reshape(depth, 1, HEAD_DIM)
    kg = k_norm.reshape(depth, 1, HEAD_DIM)
    pscale = pool_scale.reshape(depth, 1, pool_w)
    lng = ln_g.reshape(depth, 1, d)
    lnb = ln_b.reshape(depth, 1, d)
    bounded_ok = _bounded_path_ok(q_norm, k_norm)

    for l in range(depth):
        qT, q8T, k, k8, vT, sza, xb, szb, sga, sgb = _project(
            cx, x, mod, w_in_b, qg, kg, cos_t, slo_t, shi_t,
            layer=l, attn_w=attn_w, kv_w=kv_w, pool_w=pool_w)
        o = _attention_dispatch(qT, q8T, k, k8, vT, bounded_ok[l])
        cx, x = _merge(o, sza, xb, szb, sga, sgb, cx, x, mod, w_pool_b, pscale,
                       w_a_b, w_b_b, w_o_b, lng, lnb,
                       layer=l, ctx_len=ctx_len, alpha_res=alpha_res)
    return x
```
